```python
import math
import jax, jax.numpy as jnp
from jax import lax
import numpy as np

D_MODEL = 1024
BATCH = 2
SEQ = 16384
DEPTH = 2

N_MIXERS = 2
N_FOX_LAYERS = (DEPTH + N_MIXERS - 1) // N_MIXERS
N_NSA_LAYERS = DEPTH // N_MIXERS
HEAD_DIM = 64
FOX_HEADS = D_MODEL // HEAD_DIM
NSA_HEADS = D_MODEL // HEAD_DIM
NSA_KV_GROUPS = 4
NSA_HPG = NSA_HEADS // NSA_KV_GROUPS
CMP_LEN = 32
CMP_STRIDE = 16
CMP_HIDDEN = 2 * HEAD_DIM
SEL_BLOCK = 64
SEL_TOPK = 16
WINDOW = 512
Q_BLOCK = 128
REL_BUCKETS = 32
REL_MAX_DIST = 128
FF_HIDDEN = -(-8 * D_MODEL // (3 * 256)) * 256
DN_ALPHA = (2 * DEPTH) ** 0.25
DN_BETA = (8 * DEPTH) ** -0.25
LN_EPS = 1e-5
NEG = -1e30
FORCE = 1e9
TINY = 1e-30

kernel_name = 'fox_nsa_hybrid_deepnorm_adaln'


def layer_norm(x, g, b):
    xf = x.astype(jnp.float32)
    mu = xf.mean(-1, keepdims=True)
    var = jnp.square(xf - mu).mean(-1, keepdims=True)
    return ((xf - mu) * lax.rsqrt(var + LN_EPS) * g + b).astype(x.dtype)


def t5_bucket(dist):
    max_exact = REL_BUCKETS // 2
    d = jnp.maximum(dist, 0)
    large = max_exact + (jnp.log(jnp.maximum(d, 1).astype(jnp.float32) / max_exact)
                         / math.log(REL_MAX_DIST / max_exact) * (REL_BUCKETS - max_exact)).astype(jnp.int32)
    large = jnp.minimum(large, REL_BUCKETS - 1)
    return jnp.where(d < max_exact, d, large)


def masked_softmax(s, valid):
    s = jnp.where(valid, s, NEG)
    p = jnp.exp(s - s.max(-1, keepdims=True)) * valid
    return p / jnp.maximum(p.sum(-1, keepdims=True), TINY)


def fox_attention(h, w_in, b_f, w_out):
    B, S, _ = h.shape
    H, DH = FOX_HEADS, HEAD_DIM
    proj = jnp.einsum('bsd,de->bse', h, w_in)
    q, k, v, f_logit = jnp.split(proj, [D_MODEL, 2 * D_MODEL, 3 * D_MODEL], axis=-1)
    heads = lambda t: t.reshape(B, S, H, DH).transpose(0, 2, 1, 3)
    q = heads(q) * (DH ** -0.5)
    k = heads(k)
    v = heads(v)
    cum_logf = jnp.cumsum(jax.nn.log_sigmoid((f_logit + b_f).astype(jnp.float32)), axis=1).transpose(0, 2, 1)
    ar = jnp.arange(Q_BLOCK)

    def q_block(i):
        t0 = i * Q_BLOCK
        qi = lax.dynamic_slice_in_dim(q, t0, Q_BLOCK, axis=2)
        Fi = lax.dynamic_slice_in_dim(cum_logf, t0, Q_BLOCK, axis=2)

        def kv_step(j, carry):
            m, l, acc = carry
            s0 = j * Q_BLOCK
            kj = lax.dynamic_slice_in_dim(k, s0, Q_BLOCK, axis=2)
            vj = lax.dynamic_slice_in_dim(v, s0, Q_BLOCK, axis=2)
            Fj = lax.dynamic_slice_in_dim(cum_logf, s0, Q_BLOCK, axis=2)
            s = jnp.einsum('bhqd,bhkd->bhqk', qi, kj, preferred_element_type=jnp.float32) \
                + (Fi[..., :, None] - Fj[..., None, :])
            s = jnp.where((t0 + ar)[:, None] >= (s0 + ar)[None, :], s, NEG)
            m_new = jnp.maximum(m, s.max(-1))
            p = jnp.exp(s - m_new[..., None])
            corr = jnp.exp(m - m_new)
            acc = acc * corr[..., None] + jnp.einsum('bhqk,bhkd->bhqd', p, vj, preferred_element_type=jnp.float32)
            return (m_new, l * corr + p.sum(-1), acc)

        init = (jnp.full((B, H, Q_BLOCK), NEG, jnp.float32),
                jnp.zeros((B, H, Q_BLOCK), jnp.float32),
                jnp.zeros((B, H, Q_BLOCK, DH), jnp.float32))
        m, l, acc = lax.fori_loop(0, i + 1, kv_step, init)
        return acc / l[..., None]

    out = lax.map(q_block, jnp.arange(S // Q_BLOCK))
    out = out.transpose(1, 0, 3, 2, 4).reshape(B, S, D_MODEL).astype(h.dtype)
    return out @ w_out


def nsa_attention(h, w_in, b_gate, cmp_pos, cmp_w1, cmp_w2, w_out, rel_bias):
    B, S, _ = h.shape
    G, HPG, DH = NSA_KV_GROUPS, NSA_HPG, HEAD_DIM
    KVW = G * DH
    n_cmp = (S - CMP_LEN) // CMP_STRIDE + 1
    n_sel = S // SEL_BLOCK
    k_top = min(SEL_TOPK, n_sel)
    R = SEL_BLOCK // CMP_STRIDE
    O = CMP_LEN // CMP_STRIDE - 1

    proj = jnp.einsum('bsd,de->bse', h, w_in)
    cuts = [D_MODEL + n * KVW for n in range(7)]
    q, k_c, v_c, k_s, v_s, k_w, v_w, g_logit = jnp.split(proj, cuts, axis=-1)
    q = q.reshape(B, S, G, HPG, DH).transpose(0, 2, 3, 1, 4) * (DH ** -0.5)
    heads = lambda t: t.reshape(B, S, G, DH).transpose(0, 2, 1, 3)
    k_c, v_c, k_s, v_s, k_w, v_w = [heads(t) for t in (k_c, v_c, k_s, v_s, k_w, v_w)]
    gates = jax.nn.sigmoid((g_logit + b_gate).astype(jnp.float32)).reshape(B, S, 3, G, HPG).transpose(2, 0, 3, 4, 1)

    blk_idx = jnp.arange(n_cmp)[:, None] * CMP_STRIDE + jnp.arange(CMP_LEN)[None, :]

    def compress(t, pos, w1, w2):
        blocks = (t[:, :, blk_idx] + pos).reshape(B, G, n_cmp, CMP_LEN * DH)
        return jax.nn.silu(blocks @ w1) @ w2

    k_cmp = compress(k_c, cmp_pos[0], cmp_w1[0], cmp_w2[0])
    v_cmp = compress(v_c, cmp_pos[1], cmp_w1[1], cmp_w2[1])
    cmp_end = jnp.arange(n_cmp) * CMP_STRIDE + CMP_LEN - 1

    k_sb = k_s.reshape(B, G, n_sel, SEL_BLOCK, DH)
    v_sb = v_s.reshape(B, G, n_sel, SEL_BLOCK, DH)
    k_wp = jnp.pad(k_w, ((0, 0), (0, 0), (WINDOW, 0), (0, 0)))
    v_wp = jnp.pad(v_w, ((0, 0), (0, 0), (WINDOW, 0), (0, 0)))

    rb_group = rel_bias.reshape(REL_BUCKETS, G, HPG).transpose(1, 0, 2)
    b_ix = jnp.arange(B)[:, None, None, None]
    g_ix = jnp.arange(G)[None, :, None, None]
    blk = jnp.arange(n_sel)

    def head_bias(dist):
        return rel_bias[t5_bucket(dist)].transpose(2, 0, 1).reshape(G, HPG, *dist.shape)

    def q_block(i):
        t0 = i * Q_BLOCK
        qi = lax.dynamic_slice_in_dim(q, t0, Q_BLOCK, axis=3)
        tpos = t0 + jnp.arange(Q_BLOCK)

        dist_c = tpos[:, None] - cmp_end[None, :]
        s_c = jnp.einsum('bghqd,bgnd->bghqn', qi, k_cmp, preferred_element_type=jnp.float32) + head_bias(dist_c)
        p_c = masked_softmax(s_c, dist_c >= 0)
        o_c = jnp.einsum('bghqn,bgnd->bghqd', p_c, v_cmp, preferred_element_type=jnp.float32)

        imp = jnp.pad(p_c.sum(2), ((0, 0), (0, 0), (0, 0), (O, max(R * n_sel - n_cmp, 0))))
        imp_sel = sum(imp[..., a:a + R * n_sel:R] for a in range(R + O))
        cur = tpos // SEL_BLOCK
        forced = (blk[None, :] == 0) | (blk[None, :] == cur[:, None]) | (blk[None, :] == cur[:, None] - 1)
        score = jnp.where(forced, FORCE, jnp.where(blk[None, :] <= cur[:, None], imp_sel, NEG))
        top_s, top_i = lax.top_k(score, k_top)

        ksel = k_sb[b_ix, g_ix, top_i]
        vsel = v_sb[b_ix, g_ix, top_i]
        spos = top_i[..., None] * SEL_BLOCK + jnp.arange(SEL_BLOCK)
        dist_s = tpos[:, None, None] - spos
        valid_s = (dist_s >= 0) & (top_s[..., None] > 0.5 * NEG)
        bias_s = rb_group[g_ix[..., None], t5_bucket(dist_s)].transpose(0, 1, 5, 2, 3, 4)
        s_s = jnp.einsum('bghqd,bgqkcd->bghqkc', qi, ksel, preferred_element_type=jnp.float32) + bias_s
        shp = s_s.shape
        p_s = masked_softmax(s_s.reshape(B, G, HPG, Q_BLOCK, -1), valid_s.reshape(B, G, 1, Q_BLOCK, -1))
        o_s = jnp.einsum('bghqkc,bgqkcd->bghqd', p_s.reshape(shp), vsel, preferred_element_type=jnp.float32)

        kw = lax.dynamic_slice_in_dim(k_wp, t0, Q_BLOCK + WINDOW, axis=2)
        vw = lax.dynamic_slice_in_dim(v_wp, t0, Q_BLOCK + WINDOW, axis=2)
        spos_w = t0 - WINDOW + jnp.arange(Q_BLOCK + WINDOW)
        dist_w = tpos[:, None] - spos_w[None, :]
        valid_w = (dist_w >= 0) & (dist_w < WINDOW) & (spos_w[None, :] >= 0)
        s_w = jnp.einsum('bghqd,bgkd->bghqk', qi, kw, preferred_element_type=jnp.float32) + head_bias(dist_w)
        p_w = masked_softmax(s_w, valid_w)
        o_w = jnp.einsum('bghqk,bgkd->bghqd', p_w, vw, preferred_element_type=jnp.float32)

        g = lax.dynamic_slice_in_dim(gates, t0, Q_BLOCK, axis=4)[..., None]
        return g[0] * o_c + g[1] * o_s + g[2] * o_w

    out = lax.map(q_block, jnp.arange(S // Q_BLOCK))
    out = out.transpose(1, 0, 4, 2, 3, 5).reshape(B, S, D_MODEL).astype(h.dtype)
    return out @ w_out


def swiglu(h, w_in, w_out):
    a, b = jnp.split(h @ w_in, 2, axis=-1)
    return (jax.nn.silu(a) * b) @ w_out


def setup_inputs(seed: int = 0) -> dict:
    key = jax.random.key(seed)
    ks = jax.random.split(key, 20)
    D = D_MODEL
    f32 = jnp.float32
    nrm = lambda k, shape, scale: scale * jax.random.normal(k, shape, f32)
    kvw = NSA_KV_GROUPS * HEAD_DIM
    fox_cols = 3 * D + FOX_HEADS
    fox_colscale = jnp.concatenate([jnp.ones(2 * D, f32), jnp.full((D,), DN_BETA, f32), jnp.full((FOX_HEADS,), 0.5, f32)])
    nsa_cols = D + 6 * kvw + 3 * NSA_HEADS
    beta_v = jnp.full((kvw,), DN_BETA, f32)
    nsa_colscale = jnp.concatenate([jnp.ones(D + kvw, f32), beta_v, jnp.ones(kvw, f32), beta_v,
                                    jnp.ones(kvw, f32), beta_v, jnp.ones(3 * NSA_HEADS, f32)])
    return {
        'x': nrm(ks[0], (BATCH, SEQ, D), 1.0),
        'c': nrm(ks[1], (BATCH, D), 1.0),
        'ada_w': nrm(ks[2], (DEPTH, D, 6 * D), 0.2 * D ** -0.5),
        'ada_b': nrm(ks[3], (DEPTH, 6 * D), 0.02),
        'ln_g': 1.0 + nrm(ks[4], (DEPTH, 2, D), 0.02),
        'ln_b': nrm(ks[5], (DEPTH, 2, D), 0.02),
        'fox_w_in': nrm(ks[6], (N_FOX_LAYERS, D, fox_cols), D ** -0.5) * fox_colscale,
        'fox_b_f': jnp.linspace(1.0, 6.0, FOX_HEADS, dtype=f32)[None, :] + nrm(ks[7], (N_FOX_LAYERS, FOX_HEADS), 0.1),
        'fox_w_out': nrm(ks[8], (N_FOX_LAYERS, D, D), DN_BETA * D ** -0.5),
        'nsa_w_in': nrm(ks[9], (N_NSA_LAYERS, D, nsa_cols), D ** -0.5) * nsa_colscale,
        'nsa_b_gate': nrm(ks[10], (N_NSA_LAYERS, 3 * NSA_HEADS), 0.1),
        'nsa_cmp_pos': nrm(ks[11], (N_NSA_LAYERS, 2, CMP_LEN, HEAD_DIM), 0.1),
        'nsa_cmp_w1': nrm(ks[12], (N_NSA_LAYERS, 2, CMP_LEN * HEAD_DIM, CMP_HIDDEN), (CMP_LEN * HEAD_DIM) ** -0.5),
        'nsa_cmp_w2': nrm(ks[13], (N_NSA_LAYERS, 2, CMP_HIDDEN, HEAD_DIM), CMP_HIDDEN ** -0.5),
        'nsa_w_out': nrm(ks[14], (N_NSA_LAYERS, D, D), DN_BETA * D ** -0.5),
        'rel_bias': nrm(ks[15], (REL_BUCKETS, NSA_HEADS), 0.5),
        'ffn_w_in': nrm(ks[16], (DEPTH, D, 2 * FF_HIDDEN), DN_BETA * D ** -0.5),
        'ffn_w_out': nrm(ks[17], (DEPTH, FF_HIDDEN, D), DN_BETA * FF_HIDDEN ** -0.5),
    }


def reference(x, c, ada_w, ada_b, ln_g, ln_b, fox_w_in, fox_b_f, fox_w_out,
              nsa_w_in, nsa_b_gate, nsa_cmp_pos, nsa_cmp_w1, nsa_cmp_w2, nsa_w_out,
              rel_bias, ffn_w_in, ffn_w_out):
    cs = jax.nn.silu(c)
    for i in range(DEPTH):
        mod = cs @ ada_w[i] + ada_b[i]
        sh_a, sc_a, g_a, sh_f, sc_f, g_f = jnp.split(mod, 6, axis=-1)
        h = x * (1.0 + sc_a[:, None]) + sh_a[:, None]
        j = i // N_MIXERS
        if i % N_MIXERS == 0:
            y = fox_attention(h, fox_w_in[j], fox_b_f[j], fox_w_out[j])
        else:
            y = nsa_attention(h, nsa_w_in[j], nsa_b_gate[j], nsa_cmp_pos[j], nsa_cmp_w1[j],
                              nsa_cmp_w2[j], nsa_w_out[j], rel_bias)
        x = layer_norm(DN_ALPHA * x + (1.0 + g_a[:, None]) * y, ln_g[i, 0], ln_b[i, 0])
        h = x * (1.0 + sc_f[:, None]) + sh_f[:, None]
        y = swiglu(h, ffn_w_in[i], ffn_w_out[i])
        x = layer_norm(DN_ALPHA * x + (1.0 + g_f[:, None]) * y, ln_g[i, 1], ln_b[i, 1])
    return x
```

```python
import functools
import math

import numpy as np
import jax
import jax.numpy as jnp
from jax import lax
from jax.experimental import pallas as pl
from jax.experimental.pallas import tpu as pltpu

F32 = jnp.float32
BF16 = jnp.bfloat16
HIGHEST = lax.Precision.HIGHEST

D_MODEL = 1024
HEAD_DIM = 64
N_HEADS = 16
N_GROUPS = 4
HEADS_PER_GROUP = 4
KV_WIDTH = N_GROUPS * HEAD_DIM
CMP_LEN = 32
CMP_STRIDE = 16
CMP_HIDDEN = 128
SEL_BLOCK = 64
SEL_TOPK = 16
WINDOW = 512
REL_BUCKETS = 32
FF_HIDDEN = 2816
DEPTH = 2
DN_ALPHA = (2 * DEPTH) ** 0.25
LN_EPS = 1e-5
NEG = -1e30
TINY = 1e-30

LANES = 128
VMEM_LIMIT = 56 * 1024 * 1024

ROW_TILE = 512
FOX_T = 512
NSA_T = 256
FFN_CHUNK = 1408
N_FORCED = 3

BUCKET_START = (0, 1, 2, 3, 4, 5, 6, 7, 8, 9, 10, 11, 12, 13, 14, 15,
                16, 19, 21, 24, 27, 31, 35, 40, 46, 52, 59, 67, 77, 87, 99, 113)
FAR_BUCKET = REL_BUCKETS - 1
FAR_DIST = BUCKET_START[FAR_BUCKET]

NT_DIMS = (((1,), (1,)), ((), ()))


def _params(*sem):
    return pltpu.CompilerParams(dimension_semantics=sem, vmem_limit_bytes=VMEM_LIMIT)


def _dot(a, b, **kw):
    return jnp.dot(a, b, preferred_element_type=F32, **kw)


def _dot_nt(a, b):
    return lax.dot_general(a, b, NT_DIMS, preferred_element_type=F32)


def _split3(v):
    hi = v.astype(BF16)
    r = v - hi.astype(F32)
    mid = r.astype(BF16)
    lo = (r - mid.astype(F32)).astype(BF16)
    return hi, mid, lo


def _layer_norm(z, g, b):
    mu = jnp.mean(z, axis=-1, keepdims=True)
    zc = z - mu
    var = jnp.mean(zc * zc, axis=-1, keepdims=True)
    return zc * lax.rsqrt(var + LN_EPS) * g + b


def _bucket_bias(dist, rb_ref, head):
    bias = jnp.full(dist.shape, rb_ref[0, head], F32)
    for k in range(1, REL_BUCKETS):
        bias = jnp.where(dist >= BUCKET_START[k], rb_ref[k, head], bias)
    return bias


def _mod_kernel(c_ref, w_ref, b_ref, o_ref):
    c = c_ref[...]
    cs = c / (1.0 + jnp.exp(-c))
    o_ref[0] = _dot(cs, w_ref[0], precision=HIGHEST) + b_ref[0]


def _modulation(c, ada_w, ada_b):
    B = c.shape[0]
    depth, _, n = ada_w.shape
    rows = 8
    c_pad = jnp.pad(c, ((0, rows - B), (0, 0)))
    tn = 1536
    out = pl.pallas_call(
        _mod_kernel,
        grid=(depth, n // tn),
        in_specs=[
            pl.BlockSpec((rows, D_MODEL), lambda l, j: (0, 0)),
            pl.BlockSpec((1, D_MODEL, tn), lambda l, j: (l, 0, j)),
            pl.BlockSpec((1, 1, tn), lambda l, j: (l, 0, j)),
        ],
        out_specs=pl.BlockSpec((1, rows, tn), lambda l, j: (l, 0, j)),
        out_shape=jax.ShapeDtypeStruct((depth, rows, n), F32),
        compiler_params=_params("arbitrary", "arbitrary"),
    )(c_pad, ada_w, ada_b.reshape(depth, 1, n))
    return out[:, :B]


def _fox_proj_kernel(x_ref, sc_ref, sh_ref, wk_ref, wqt_ref, wvt_ref, wfh_ref, wfl_ref,
                     bf_ref, place_ref, kp_ref, qt_ref, vt_ref, carry_ref):
    tm = x_ref.shape[1]

    @pl.when(pl.program_id(1) == 0)
    def _():
        carry_ref[...] = jnp.zeros_like(carry_ref)

    h = x_ref[0] * (1.0 + sc_ref[0]) + sh_ref[0]
    hb = h.astype(BF16)
    hl = (h - hb.astype(F32)).astype(BF16)

    f = _dot(hb, wfh_ref[...]) + _dot(hl, wfh_ref[...]) + _dot(hb, wfl_ref[...])
    z = f + bf_ref[...]
    logf = jnp.minimum(z, 0.0) - jnp.log(1.0 + jnp.exp(-jnp.abs(z)))
    row = lax.broadcasted_iota(jnp.int32, (tm, tm), 0)
    col = lax.broadcasted_iota(jnp.int32, (tm, tm), 1)
    lower = (col <= row).astype(F32)
    cum = _dot(lower, logf, precision=HIGHEST) + carry_ref[0:1, :]
    carry_ref[...] = jnp.broadcast_to(cum[tm - 1:tm, :], carry_ref.shape)

    hi, mid, lo = [p.astype(F32) for p in _split3(-cum)]
    lane = lax.broadcasted_iota(jnp.int32, (tm, LANES), 1)
    pieces = jnp.where(lane < N_HEADS, hi, jnp.where(lane < 2 * N_HEADS, mid, lo)).astype(BF16)
    kp = _dot(hb, wk_ref[...]) + _dot(pieces, place_ref[...])
    qt = _dot_nt(wqt_ref[...], hb)
    qrow = lax.broadcasted_iota(jnp.int32, qt.shape, 0) % LANES
    qt = jnp.where((qrow >= HEAD_DIM) & (qrow < HEAD_DIM + 3), 1.0, qt)
    vt = _dot_nt(wvt_ref[...], hb)
    for hd in range(N_HEADS):
        kp_ref[0, hd] = kp[:, hd * LANES:(hd + 1) * LANES].astype(BF16)
        qt_ref[0, hd] = qt[hd * LANES:(hd + 1) * LANES, :].astype(BF16)
        vt_ref[0, hd, 0] = vt[hd * HEAD_DIM:(hd + 1) * HEAD_DIM, :].astype(BF16)


def _fox_proj(x, sc, sh, w_in, b_f):
    B, S, _ = x.shape
    tm = FOX_T
    scale = HEAD_DIM ** -0.5
    wq = (w_in[:, :D_MODEL] * scale).reshape(D_MODEL, N_HEADS, HEAD_DIM)
    wk = w_in[:, D_MODEL:2 * D_MODEL].reshape(D_MODEL, N_HEADS, HEAD_DIM)
    wv = w_in[:, 2 * D_MODEL:3 * D_MODEL]
    wf = w_in[:, 3 * D_MODEL:]
    pad = ((0, 0), (0, 0), (0, LANES - HEAD_DIM))
    wk_p = jnp.pad(wk, pad).reshape(D_MODEL, N_HEADS * LANES).astype(BF16)
    wqt_p = jnp.pad(wq, pad).reshape(D_MODEL, N_HEADS * LANES).T.astype(BF16)
    wvt = wv.T.astype(BF16)
    wf_rep = jnp.pad(jnp.tile(wf, (1, 3)), ((0, 0), (0, LANES - 3 * N_HEADS)))
    wf_hi = wf_rep.astype(BF16)
    wf_lo = (wf_rep - wf_hi.astype(F32)).astype(BF16)
    bf_rep = jnp.pad(jnp.tile(b_f, 3), (0, LANES - 3 * N_HEADS)).reshape(1, LANES)
    place = np.zeros((LANES, N_HEADS * LANES), np.float32)
    for r in range(3):
        for hd in range(N_HEADS):
            place[r * N_HEADS + hd, hd * LANES + HEAD_DIM + r] = 1.0
    place = jnp.asarray(place, BF16)

    full = lambda a: pl.BlockSpec(a.shape, lambda b, i: (0,) * a.ndim)
    return pl.pallas_call(
        _fox_proj_kernel,
        grid=(B, S // tm),
        in_specs=[
            pl.BlockSpec((1, tm, D_MODEL), lambda b, i: (b, i, 0)),
            pl.BlockSpec((1, 1, D_MODEL), lambda b, i: (b, 0, 0)),
            pl.BlockSpec((1, 1, D_MODEL), lambda b, i: (b, 0, 0)),
            full(wk_p), full(wqt_p), full(wvt), full(wf_hi), full(wf_lo), full(bf_rep), full(place),
        ],
        out_specs=[
            pl.BlockSpec((1, N_HEADS, tm, LANES), lambda b, i: (b, 0, i, 0)),
            pl.BlockSpec((1, N_HEADS, LANES, tm), lambda b, i: (b, 0, 0, i)),
            pl.BlockSpec((1, N_HEADS, 1, HEAD_DIM, tm), lambda b, i: (b, 0, i, 0, 0)),
        ],
        out_shape=[
            jax.ShapeDtypeStruct((B, N_HEADS, S, LANES), BF16),
            jax.ShapeDtypeStruct((B, N_HEADS, LANES, S), BF16),
            jax.ShapeDtypeStruct((B, N_HEADS, S // tm, HEAD_DIM, tm), BF16),
        ],
        scratch_shapes=[pltpu.VMEM((8, LANES), F32)],
        compiler_params=_params("arbitrary", "arbitrary"),
    )(x, sc, sh, wk_p, wqt_p, wvt, wf_hi, wf_lo, bf_rep, place)


def _flash_step(s, v_t, m, l, acc):
    m_new = jnp.maximum(m, jnp.max(s, axis=0, keepdims=True))
    p = jnp.exp(s - m_new)
    corr = jnp.exp(m - m_new)
    l_new = l * corr + jnp.sum(p, axis=0, keepdims=True)
    acc_new = acc * corr + _dot(v_t, p.astype(BF16))
    return m_new, l_new, acc_new


def _fox_attn_kernel(qt_ref, kp_ref, vt_ref, o_ref):
    t = qt_ref.shape[3]
    i = pl.program_id(2)
    qt = qt_ref[0, 0]

    def scores(j):
        k = kp_ref[0, 0, pl.ds(pl.multiple_of(j * t, t), t), :]
        return _dot(k, qt)

    def body(j, carry):
        return _flash_step(scores(j), vt_ref[0, 0, j], *carry)

    init = (jnp.full((1, t), NEG, F32), jnp.zeros((1, t), F32), jnp.zeros((HEAD_DIM, t), F32))
    carry = lax.fori_loop(0, i, body, init)
    key = lax.broadcasted_iota(jnp.int32, (t, t), 0)
    qry = lax.broadcasted_iota(jnp.int32, (t, t), 1)
    s = jnp.where(key <= qry, scores(i), NEG)
    m, l, acc = _flash_step(s, vt_ref[0, 0, i], *carry)
    o_ref[0] = (acc / l).astype(o_ref.dtype)


def _fox_attention(qt, kp, vt):
    B, H, S, _ = kp.shape
    t = FOX_T
    return pl.pallas_call(
        _fox_attn_kernel,
        grid=(B, H, S // t),
        in_specs=[
            pl.BlockSpec((1, 1, LANES, t), lambda b, h, i: (b, h, 0, i)),
            pl.BlockSpec((1, 1, S, LANES), lambda b, h, i: (b, h, 0, 0)),
            pl.BlockSpec((1, 1, S // t, HEAD_DIM, t), lambda b, h, i: (b, h, 0, 0, 0)),
        ],
        out_specs=pl.BlockSpec((1, HEAD_DIM, t), lambda b, h, i: (b, h, i)),
        out_shape=jax.ShapeDtypeStruct((B, H * HEAD_DIM, S), BF16),
        compiler_params=_params("arbitrary", "arbitrary", "arbitrary"),
    )(qt, kp, vt)


def _outproj_kernel(at_ref, w_ref, x_ref, gate_ref, g_ref, b_ref, o_ref):
    tm = x_ref.shape[1]
    r = lax.broadcasted_iota(jnp.int32, (tm, tm), 0)
    c = lax.broadcasted_iota(jnp.int32, (tm, tm), 1)
    eye = (r == c).astype(BF16)
    a = _dot_nt(eye, at_ref[0]).astype(BF16)
    y = _dot(a, w_ref[...])
    z = DN_ALPHA * x_ref[0] + (1.0 + gate_ref[0]) * y
    o_ref[0] = _layer_norm(z, g_ref[...], b_ref[...])


def _outproj_ln(attn_t, w_out, x, gate, ln_g, ln_b):
    B, S, _ = x.shape
    tm = ROW_TILE
    vec = pl.BlockSpec((1, D_MODEL), lambda b, i: (0, 0))
    return pl.pallas_call(
        _outproj_kernel,
        grid=(B, S // tm),
        in_specs=[
            pl.BlockSpec((1, D_MODEL, tm), lambda b, i: (b, 0, i)),
            pl.BlockSpec((D_MODEL, D_MODEL), lambda b, i: (0, 0)),
            pl.BlockSpec((1, tm, D_MODEL), lambda b, i: (b, i, 0)),
            pl.BlockSpec((1, 1, D_MODEL), lambda b, i: (b, 0, 0)),
            vec, vec,
        ],
        out_specs=pl.BlockSpec((1, tm, D_MODEL), lambda b, i: (b, i, 0)),
        out_shape=jax.ShapeDtypeStruct((B, S, D_MODEL), F32),
        compiler_params=_params("arbitrary", "arbitrary"),
    )(attn_t, w_out.astype(BF16), x, gate, ln_g.reshape(1, -1), ln_b.reshape(1, -1))


def _ffn_kernel(x_ref, sc_ref, sh_ref, gate_ref, wa_ref, wb_ref, wo_ref, g_ref, b_ref,
                o_ref, acc_ref):
    c = pl.program_id(2)
    x = x_ref[0]
    hb = (x * (1.0 + sc_ref[0]) + sh_ref[0]).astype(BF16)
    a = _dot(hb, wa_ref[...])
    b = _dot(hb, wb_ref[...])
    u = (a / (1.0 + jnp.exp(-a)) * b).astype(BF16)
    y = _dot(u, wo_ref[...])

    @pl.when(c == 0)
    def _():
        acc_ref[...] = y

    @pl.when(c > 0)
    def _():
        acc_ref[...] += y

    @pl.when(c == pl.num_programs(2) - 1)
    def _():
        z = DN_ALPHA * x + (1.0 + gate_ref[0]) * acc_ref[...]
        o_ref[0] = _layer_norm(z, g_ref[...], b_ref[...])


def _ffn_ln(x, sc, sh, gate, w_in, w_out, ln_g, ln_b):
    B, S, _ = x.shape
    tm = ROW_TILE
    nc = FF_HIDDEN // FFN_CHUNK
    w_in = w_in.astype(BF16)
    mod = pl.BlockSpec((1, 1, D_MODEL), lambda b, i, c: (b, 0, 0))
    vec = pl.BlockSpec((1, D_MODEL), lambda b, i, c: (0, 0))
    return pl.pallas_call(
        _ffn_kernel,
        grid=(B, S // tm, nc),
        in_specs=[
            pl.BlockSpec((1, tm, D_MODEL), lambda b, i, c: (b, i, 0)),
            mod, mod, mod,
            pl.BlockSpec((D_MODEL, FFN_CHUNK), lambda b, i, c: (0, c)),
            pl.BlockSpec((D_MODEL, FFN_CHUNK), lambda b, i, c: (0, nc + c)),
            pl.BlockSpec((FFN_CHUNK, D_MODEL), lambda b, i, c: (c, 0)),
            vec, vec,
        ],
        out_specs=pl.BlockSpec((1, tm, D_MODEL), lambda b, i, c: (b, i, 0)),
        out_shape=jax.ShapeDtypeStruct((B, S, D_MODEL), F32),
        scratch_shapes=[pltpu.VMEM((tm, D_MODEL), F32)],
        compiler_params=_params("arbitrary", "arbitrary", "arbitrary"),
    )(x, sc, sh, gate, w_in, w_in, w_out.astype(BF16), ln_g.reshape(1, -1), ln_b.reshape(1, -1))


def _nsa_proj_kernel(x_ref, sc_ref, sh_ref, wqt_ref, wnat_ref, wvt_ref, wgt_ref, bg_ref,
                     qt_ref, kc_ref, vc_ref, ks_ref, kw_ref, vst_ref, vwt_ref, gt_ref):
    tm = x_ref.shape[1]
    t = NSA_T
    hb = (x_ref[0] * (1.0 + sc_ref[0]) + sh_ref[0]).astype(BF16)

    qt = _dot_nt(wqt_ref[...], hb).astype(BF16)
    for g in range(N_GROUPS):
        for hh in range(HEADS_PER_GROUP):
            r0 = (g * HEADS_PER_GROUP + hh) * HEAD_DIM
            for c in range(tm // t):
                qt_ref[0, g, c, :, hh * t:(hh + 1) * t] = qt[r0:r0 + HEAD_DIM, c * t:(c + 1) * t]

    nat = _dot(hb, wnat_ref[...])
    lane = lax.broadcasted_iota(jnp.int32, (tm, LANES), 1)
    ones = ((lane == HEAD_DIM) | (lane == HEAD_DIM + 1)).astype(F32)
    off_ks = 2 * KV_WIDTH
    off_kw = off_ks + N_GROUPS * LANES
    for g in range(N_GROUPS):
        kc_ref[0, g] = nat[:, g * HEAD_DIM:(g + 1) * HEAD_DIM]
        vc_ref[0, g] = nat[:, KV_WIDTH + g * HEAD_DIM:KV_WIDTH + (g + 1) * HEAD_DIM]
        ks_ref[0, g] = (nat[:, off_ks + g * LANES:off_ks + (g + 1) * LANES] + ones).astype(BF16)
        kw_ref[0, g] = nat[:, off_kw + g * LANES:off_kw + (g + 1) * LANES].astype(BF16)

    vt = _dot_nt(wvt_ref[...], hb).astype(BF16)
    for g in range(N_GROUPS):
        for c in range(tm // t):
            vst_ref[0, g, c] = vt[g * HEAD_DIM:(g + 1) * HEAD_DIM, c * t:(c + 1) * t]
            vwt_ref[0, g, c] = vt[KV_WIDTH + g * HEAD_DIM:KV_WIDTH + (g + 1) * HEAD_DIM, c * t:(c + 1) * t]

    gl = _dot_nt(wgt_ref[...], hb) + bg_ref[...]
    gt_ref[0] = 1.0 / (1.0 + jnp.exp(-gl))


def _nsa_proj(x, sc, sh, w_in, b_gate):
    B, S, _ = x.shape
    tm = ROW_TILE
    t = NSA_T
    scale = HEAD_DIM ** -0.5
    cuts = [D_MODEL + n * KV_WIDTH for n in range(7)]
    wq, wkc, wvc, wks, wvs, wkw, wvw, wg = jnp.split(w_in, cuts, axis=1)
    wqt = (wq * scale).T.astype(BF16)
    padk = lambda w: jnp.pad(w.reshape(D_MODEL, N_GROUPS, HEAD_DIM),
                             ((0, 0), (0, 0), (0, LANES - HEAD_DIM))).reshape(D_MODEL, N_GROUPS * LANES)
    wnat = jnp.concatenate([wkc, wvc, padk(wks), padk(wkw)], axis=1).astype(BF16)
    wvt = jnp.concatenate([wvs, wvw], axis=1).T.astype(BF16)
    n_gate = 3 * N_HEADS
    wgt = wg.T.astype(BF16)
    bg = jnp.broadcast_to(b_gate.reshape(n_gate, 1), (n_gate, tm))

    full = lambda a: pl.BlockSpec(a.shape, lambda b, i: (0,) * a.ndim)
    nat_spec = pl.BlockSpec((1, N_GROUPS, tm, HEAD_DIM), lambda b, i: (b, 0, i, 0))
    pad_spec = pl.BlockSpec((1, N_GROUPS, tm, LANES), lambda b, i: (b, 0, i, 0))
    vt_spec = pl.BlockSpec((1, N_GROUPS, tm // t, HEAD_DIM, t), lambda b, i: (b, 0, i, 0, 0))
    return pl.pallas_call(
        _nsa_proj_kernel,
        grid=(B, S // tm),
        in_specs=[
            pl.BlockSpec((1, tm, D_MODEL), lambda b, i: (b, i, 0)),
            pl.BlockSpec((1, 1, D_MODEL), lambda b, i: (b, 0, 0)),
            pl.BlockSpec((1, 1, D_MODEL), lambda b, i: (b, 0, 0)),
            full(wqt), full(wnat), full(wvt), full(wgt), full(bg),
        ],
        out_specs=[
            pl.BlockSpec((1, N_GROUPS, tm // t, HEAD_DIM, HEADS_PER_GROUP * t), lambda b, i: (b, 0, i, 0, 0)),
            nat_spec, nat_spec, pad_spec, pad_spec, vt_spec, vt_spec,
            pl.BlockSpec((1, n_gate, tm), lambda b, i: (b, 0, i)),
        ],
        out_shape=[
            jax.ShapeDtypeStruct((B, N_GROUPS, S // t, HEAD_DIM, HEADS_PER_GROUP * t), BF16),
            jax.ShapeDtypeStruct((B, N_GROUPS, S, HEAD_DIM), F32),
            jax.ShapeDtypeStruct((B, N_GROUPS, S, HEAD_DIM), F32),
            jax.ShapeDtypeStruct((B, N_GROUPS, S, LANES), BF16),
            jax.ShapeDtypeStruct((B, N_GROUPS, S, LANES), BF16),
            jax.ShapeDtypeStruct((B, N_GROUPS, S // t, HEAD_DIM, t), BF16),
            jax.ShapeDtypeStruct((B, N_GROUPS, S // t, HEAD_DIM, t), BF16),
            jax.ShapeDtypeStruct((B, n_gate, S), F32),
        ],
        compiler_params=_params("arbitrary", "arbitrary"),
    )(x, sc, sh, wqt, wnat, wvt, wgt, bg)


def _compress_kernel(tk_ref, tv_ref, pos_ref, w1_ref, w2k_ref, w2vt_ref, kc_ref, vct_ref):
    n = tk_ref.shape[2]
    half = CMP_STRIDE * HEAD_DIM

    def hidden(t_ref, idx):
        t16 = t_ref[0, 0]
        xa = (t16 + pos_ref[idx, 0:1, :]).astype(BF16)
        xb = (t16 + pos_ref[idx, 1:2, :]).astype(BF16)
        first = _dot(xa, w1_ref[idx, :half, :])
        second = _dot(xb, w1_ref[idx, half:, :])
        pre = first + pltpu.roll(second, n - 1, 0)
        return (pre / (1.0 + jnp.exp(-pre))).astype(BF16)

    kc = _dot(hidden(tk_ref, 0), w2k_ref[...])
    lane = lax.broadcasted_iota(jnp.int32, kc.shape, 1)
    ones = ((lane == HEAD_DIM) | (lane == HEAD_DIM + 1)).astype(F32)
    kc_ref[0, 0] = (kc + ones).astype(BF16)
    vct_ref[0, 0] = _dot_nt(w2vt_ref[...], hidden(tv_ref, 1)).astype(BF16)


def _compress(kc, vc, cmp_pos, cmp_w1, cmp_w2):
    B, G, S, _ = kc.shape
    n = S // CMP_STRIDE
    width = CMP_STRIDE * HEAD_DIM
    tk = kc.reshape(B, G, n, width)
    tv = vc.reshape(B, G, n, width)
    pos = cmp_pos.reshape(2, 2, width)
    w1 = cmp_w1.astype(BF16)
    w2k = jnp.pad(cmp_w2[0], ((0, 0), (0, LANES - HEAD_DIM))).astype(BF16)
    w2vt = cmp_w2[1].T.astype(BF16)
    full = lambda a: pl.BlockSpec(a.shape, lambda b, g: (0,) * a.ndim)
    t_spec = pl.BlockSpec((1, 1, n, width), lambda b, g: (b, g, 0, 0))
    return pl.pallas_call(
        _compress_kernel,
        grid=(B, G),
        in_specs=[t_spec, t_spec, full(pos), full(w1), full(w2k), full(w2vt)],
        out_specs=[
            pl.BlockSpec((1, 1, n, LANES), lambda b, g: (b, g, 0, 0)),
            pl.BlockSpec((1, 1, HEAD_DIM, n), lambda b, g: (b, g, 0, 0)),
        ],
        out_shape=[
            jax.ShapeDtypeStruct((B, G, n, LANES), BF16),
            jax.ShapeDtypeStruct((B, G, HEAD_DIM, n), BF16),
        ],
        compiler_params=_params("arbitrary", "arbitrary"),
    )(tk, tv, pos, w1, w2k, w2vt)


def _bias_tiles_kernel(rb_ref, sel_ref, win_ref):
    t = sel_ref.shape[2]
    hd = pl.program_id(0)
    key = lax.broadcasted_iota(jnp.int32, (t, t), 0)
    qry = lax.broadcasted_iota(jnp.int32, (t, t), 1)
    far = rb_ref[FAR_BUCKET, hd]
    d0 = qry - key
    b0 = _bucket_bias(d0, rb_ref, hd)
    b1 = _bucket_bias(d0 + t, rb_ref, hd)
    b2 = _bucket_bias(d0 + 2 * t, rb_ref, hd)
    sel_ref[0, 0] = b1 - far
    sel_ref[1, 0] = jnp.where(d0 >= 0, b0 - far, NEG)
    win_ref[0, 0] = jnp.where(d0 + 2 * t < WINDOW, b2, NEG)
    win_ref[1, 0] = b1
    win_ref[2, 0] = jnp.where(d0 >= 0, b0, NEG)


def _bias_tiles(rel_bias):
    t = NSA_T
    assert WINDOW == 2 * t
    spec = lambda n: pl.BlockSpec((n, 1, t, t), lambda h: (0, h, 0, 0))
    return pl.pallas_call(
        _bias_tiles_kernel,
        grid=(N_HEADS,),
        in_specs=[pl.BlockSpec(memory_space=pltpu.SMEM)],
        out_specs=[spec(2), spec(3)],
        out_shape=[
            jax.ShapeDtypeStruct((2, N_HEADS, t, t), F32),
            jax.ShapeDtypeStruct((3, N_HEADS, t, t), F32),
        ],
        compiler_params=_params("arbitrary"),
    )(rel_bias)


def _far_bias_rows(rb_ref, g, width):
    t = width // HEADS_PER_GROUP
    lane = lax.broadcasted_iota(jnp.int32, (HEAD_DIM, width), 1)
    row = lax.broadcasted_iota(jnp.int32, (HEAD_DIM, width), 0)
    far = jnp.zeros((HEAD_DIM, width), F32)
    for hh in range(HEADS_PER_GROUP):
        far = jnp.where(lane >= hh * t, rb_ref[FAR_BUCKET, g * HEADS_PER_GROUP + hh], far)
    hi = far.astype(BF16).astype(F32)
    return jnp.where(row == 0, hi, jnp.where(row == 1, far - hi, 0.0)).astype(BF16)


def _nsa_cmp_kernel(rb_ref, qt_ref, kc_ref, vct_ref, at_ref, oc_ref, sb_ref, s_ref, p_ref):
    g = pl.program_id(1)
    i = pl.program_id(2)
    width = qt_ref.shape[4]
    t = width // HEADS_PER_GROUP
    n = kc_ref.shape[2]
    nb = at_ref.shape[0]
    t0 = i * t

    qp = jnp.concatenate([qt_ref[0, 0, 0], _far_bias_rows(rb_ref, g, width)], axis=0)
    s_ref[...] = _dot(kc_ref[0, 0], qp)

    band = t // CMP_STRIDE + 16
    assert (FAR_DIST + CMP_LEN - 1) <= 16 * CMP_STRIDE and band <= n
    r0 = pl.multiple_of(jnp.clip(t0 // CMP_STRIDE - 16, 0, n - band), 8)
    blk_r = r0 + lax.broadcasted_iota(jnp.int32, (band, t), 0)
    qry_r = t0 + lax.broadcasted_iota(jnp.int32, (band, t), 1)
    dist_r = qry_r - (blk_r * CMP_STRIDE + CMP_LEN - 1)
    for hh in range(HEADS_PER_GROUP):
        hd = g * HEADS_PER_GROUP + hh
        corr = _bucket_bias(dist_r, rb_ref, hd) - rb_ref[FAR_BUCKET, hd]
        s_ref[pl.ds(r0, band), hh * t:(hh + 1) * t] += corr

    blk = lax.broadcasted_iota(jnp.int32, (n, t), 0)
    qry = t0 + lax.broadcasted_iota(jnp.int32, (n, t), 1)
    valid = qry >= blk * CMP_STRIDE + CMP_LEN - 1
    imp = jnp.zeros((n, t), F32)
    for hh in range(HEADS_PER_GROUP):
        s = jnp.where(valid, s_ref[:, hh * t:(hh + 1) * t], NEG)
        m = jnp.max(s, axis=0, keepdims=True)
        p = jnp.where(valid, jnp.exp(s - m), 0.0)
        l = jnp.sum(p, axis=0, keepdims=True)
        p = p * (1.0 / jnp.maximum(l, TINY))
        p_ref[:, hh * t:(hh + 1) * t] = p.astype(BF16)
        imp = imp + p
    oc_ref[0, 0, 0] = _dot(vct_ref[0, 0], p_ref[...])

    hi, mid, lo = _split3(imp)
    at = at_ref[...]
    imp_sel = _dot(at, hi) + _dot(at, mid) + _dot(at, lo)

    sblk = lax.broadcasted_iota(jnp.int32, (nb, t), 0)
    cur = (t0 + lax.broadcasted_iota(jnp.int32, (nb, t), 1)) // SEL_BLOCK
    forced = (sblk == 0) | (sblk == cur) | (sblk == cur - 1)
    cand = jnp.where((sblk >= 1) & (sblk <= cur - 2), imp_sel, -1.0)
    chosen = forced.astype(F32)
    sblk_f = sblk.astype(F32)
    for _ in range(SEL_TOPK - N_FORCED):
        best = jnp.max(cand, axis=0, keepdims=True)
        hit = (cand == best) & (best >= 0.0)
        first = jnp.min(jnp.where(hit, sblk_f, float(nb)), axis=0, keepdims=True)
        pick = sblk_f == first
        chosen = jnp.where(pick, 1.0, chosen)
        cand = jnp.where(pick, -1.0, cand)
    sb_ref[0, 0, 0] = jnp.where(chosen > 0.5, 0.0, NEG).astype(BF16)


def _nsa_cmp(rel_bias, qt, kcmp, vcmp_t, n_sel):
    B, G, nq, _, width = qt.shape
    t = width // HEADS_PER_GROUP
    n = kcmp.shape[2]
    n_cmp = n - 1
    nb = -(-n_sel // LANES) * LANES
    R = SEL_BLOCK // CMP_STRIDE
    at = np.zeros((nb, n), np.float32)
    for j in range(n_sel):
        lo, hi = max(R * j - 1, 0), min(R * j + R - 1, n_cmp - 1)
        at[j, lo:hi + 1] = 1.0
    at = jnp.asarray(at, BF16)
    return pl.pallas_call(
        _nsa_cmp_kernel,
        grid=(B, G, nq),
        in_specs=[
            pl.BlockSpec(memory_space=pltpu.SMEM),
            pl.BlockSpec((1, 1, 1, HEAD_DIM, width), lambda b, g, i: (b, g, i, 0, 0)),
            pl.BlockSpec((1, 1, n, LANES), lambda b, g, i: (b, g, 0, 0)),
            pl.BlockSpec((1, 1, HEAD_DIM, n), lambda b, g, i: (b, g, 0, 0)),
            pl.BlockSpec((nb, n), lambda b, g, i: (0, 0)),
        ],
        out_specs=[
            pl.BlockSpec((1, 1, 1, HEAD_DIM, width), lambda b, g, i: (b, g, i, 0, 0)),
            pl.BlockSpec((1, 1, 1, nb, t), lambda b, g, i: (b, g, i, 0, 0)),
        ],
        out_shape=[
            jax.ShapeDtypeStruct((B, G, nq, HEAD_DIM, width), F32),
            jax.ShapeDtypeStruct((B, G, nq, nb, t), BF16),
        ],
        scratch_shapes=[pltpu.VMEM((n, width), F32), pltpu.VMEM((n, width), BF16)],
        compiler_params=_params("arbitrary", "arbitrary", "arbitrary"),
    )(rel_bias, qt, kcmp, vcmp_t, at)


def _nsa_win_kernel(qt_ref, k0_ref, k1_ref, k2_ref, v0_ref, v1_ref, v2_ref, wb_ref, ow_ref):
    i = pl.program_id(2)
    width = qt_ref.shape[4]
    t = width // HEADS_PER_GROUP
    qt = qt_ref[0, 0, 0]
    qp = jnp.concatenate([qt, jnp.zeros_like(qt)], axis=0)
    k_refs = (k0_ref, k1_ref, k2_ref)
    v_refs = (v0_ref, v1_ref, v2_ref)
    scores = []
    for c in range(3):
        s = _dot(k_refs[c][0, 0], qp)
        missing = jnp.where(i - (2 - c) < 0, NEG, 0.0)
        s = jnp.concatenate([s[:, hh * t:(hh + 1) * t] + (wb_ref[c, hh] + missing)
                             for hh in range(HEADS_PER_GROUP)], axis=1)
        scores.append(s)
    m = jnp.maximum(jnp.maximum(jnp.max(scores[0], axis=0, keepdims=True),
                                jnp.max(scores[1], axis=0, keepdims=True)),
                    jnp.max(scores[2], axis=0, keepdims=True))
    l = jnp.zeros_like(m)
    acc = jnp.zeros((HEAD_DIM, width), F32)
    for c in range(3):
        p = jnp.exp(scores[c] - m)
        l = l + jnp.sum(p, axis=0, keepdims=True)
        acc = acc + _dot(v_refs[c][0, 0, 0], p.astype(BF16))
    ow_ref[0, 0, 0] = acc / l


def _nsa_window(qt, kw, vwt, win_bias):
    B, G, nq, _, width = qt.shape
    t = width // HEADS_PER_GROUP
    k_spec = lambda back: pl.BlockSpec((1, 1, t, LANES), lambda b, g, i: (b, g, jnp.maximum(i - back, 0), 0))
    v_spec = lambda back: pl.BlockSpec((1, 1, 1, HEAD_DIM, t),
                                       lambda b, g, i: (b, g, jnp.maximum(i - back, 0), 0, 0))
    return pl.pallas_call(
        _nsa_win_kernel,
        grid=(B, G, nq),
        in_specs=[
            pl.BlockSpec((1, 1, 1, HEAD_DIM, width), lambda b, g, i: (b, g, i, 0, 0)),
            k_spec(2), k_spec(1), k_spec(0), v_spec(2), v_spec(1), v_spec(0),
            pl.BlockSpec((3, HEADS_PER_GROUP, t, t), lambda b, g, i: (0, g, 0, 0)),
        ],
        out_specs=pl.BlockSpec((1, 1, 1, HEAD_DIM, width), lambda b, g, i: (b, g, i, 0, 0)),
        out_shape=jax.ShapeDtypeStruct((B, G, nq, HEAD_DIM, width), F32),
        compiler_params=_params("arbitrary", "arbitrary", "arbitrary"),
    )(qt, kw, kw, kw, vwt, vwt, vwt, win_bias)


def _nsa_sel_kernel(rb_ref, qt_ref, sb_ref, ks_ref, e_ref, vst_ref, cb_ref, oc_ref, ow_ref, gt_ref,
                    o_ref, qp_ref):
    g = pl.program_id(1)
    i = pl.program_id(2)
    width = qt_ref.shape[4]
    t = width // HEADS_PER_GROUP
    nb = sb_ref.shape[3]
    chunks_per_phase = LANES * SEL_BLOCK // t

    qp_ref[0:HEAD_DIM, :] = qt_ref[0, 0, 0]
    qp_ref[HEAD_DIM:LANES, :] = _far_bias_rows(rb_ref, g, width)

    def load_phase(ph):
        sb = sb_ref[0, 0, 0, pl.ds(pl.multiple_of(ph * LANES, LANES), LANES), :]
        qp_ref[LANES:2 * LANES, :] = jnp.concatenate([sb] * HEADS_PER_GROUP, axis=1)

    def scores(j):
        rows = pl.ds(pl.multiple_of(j * t, t), t)
        kp = jnp.concatenate([ks_ref[0, 0, rows, :], e_ref[rows, :]], axis=1)
        return _dot(kp, qp_ref[...])

    def far_body(j, carry):
        return _flash_step(scores(j), vst_ref[0, 0, j], *carry)

    def near_step(j, which, carry):
        s = scores(j)
        s = jnp.concatenate([s[:, hh * t:(hh + 1) * t] + cb_ref[which, hh]
                             for hh in range(HEADS_PER_GROUP)], axis=1)
        return _flash_step(s, vst_ref[0, 0, j], *carry)

    carry = (jnp.full((1, width), NEG, F32), jnp.zeros((1, width), F32), jnp.zeros((HEAD_DIM, width), F32))
    n_far = jnp.maximum(i - 1, 0)
    for ph in range(nb // LANES):
        load_phase(ph)
        lo = ph * chunks_per_phase
        hi = jnp.clip(n_far, lo, lo + chunks_per_phase)
        carry = lax.fori_loop(lo, hi, far_body, carry)

    def with_prev(c):
        load_phase((i - 1) // chunks_per_phase)
        return near_step(i - 1, 0, c)

    carry = lax.cond(i >= 1, with_prev, lambda c: c, carry)
    load_phase(i // chunks_per_phase)
    m, l, acc = near_step(i, 1, carry)

    out = gt_ref[0, 0, 0, 0:1, :] * oc_ref[0, 0, 0] + gt_ref[0, 0, 0, 1:2, :] * (acc / l) \
        + gt_ref[0, 0, 0, 2:3, :] * ow_ref[0, 0, 0]
    for hh in range(HEADS_PER_GROUP):
        o_ref[0, hh * HEAD_DIM:(hh + 1) * HEAD_DIM, :] = out[:, hh * t:(hh + 1) * t].astype(o_ref.dtype)


def _nsa_select_combine(rel_bias, qt, sel_bias, ks, vst, sel_corr, oc, ow, gates):
    B, G, nq, _, width = qt.shape
    t = width // HEADS_PER_GROUP
    S = nq * t
    nb = sel_bias.shape[3]
    blocks = (np.arange(S) // SEL_BLOCK) % LANES
    onehot = jnp.asarray(blocks[:, None] == np.arange(LANES)[None, :], BF16)
    tile = pl.BlockSpec((1, 1, 1, HEAD_DIM, width), lambda b, g, i: (b, g, i, 0, 0))
    return pl.pallas_call(
        _nsa_sel_kernel,
        grid=(B, G, nq),
        in_specs=[
            pl.BlockSpec(memory_space=pltpu.SMEM),
            tile,
            pl.BlockSpec((1, 1, 1, nb, t), lambda b, g, i: (b, g, i, 0, 0)),
            pl.BlockSpec((1, 1, S, LANES), lambda b, g, i: (b, g, 0, 0)),
            pl.BlockSpec((S, LANES), lambda b, g, i: (0, 0)),
            pl.BlockSpec((1, 1, nq, HEAD_DIM, t), lambda b, g, i: (b, g, 0, 0, 0)),
            pl.BlockSpec((2, HEADS_PER_GROUP, t, t), lambda b, g, i: (0, g, 0, 0)),
            tile, tile,
            pl.BlockSpec((1, 1, 1, 8, width), lambda b, g, i: (b, g, i, 0, 0)),
        ],
        out_specs=pl.BlockSpec((1, HEADS_PER_GROUP * HEAD_DIM, t), lambda b, g, i: (b, g, i)),
        out_shape=jax.ShapeDtypeStruct((B, D_MODEL, S), BF16),
        scratch_shapes=[pltpu.VMEM((2 * LANES, width), BF16)],
        compiler_params=_params("arbitrary", "arbitrary", "arbitrary"),
    )(rel_bias, qt, sel_bias, ks, onehot, vst, sel_corr, oc, ow, gates)


def _nsa_attention_t(x, sc, sh, w_in, b_gate, cmp_pos, cmp_w1, cmp_w2, rel_bias):
    B, S, _ = x.shape
    t = NSA_T
    n_sel = S // SEL_BLOCK
    assert S % ROW_TILE == 0 and n_sel >= SEL_TOPK and S // CMP_STRIDE >= t // CMP_STRIDE + 16
    qt, kc, vc, ks, kw, vst, vwt, gt = _nsa_proj(x, sc, sh, w_in, b_gate)
    kcmp, vcmp_t = _compress(kc, vc, cmp_pos, cmp_w1, cmp_w2)
    sel_corr, win_bias = _bias_tiles(rel_bias)
    oc, sel_bias = _nsa_cmp(rel_bias, qt, kcmp, vcmp_t, n_sel)
    ow = _nsa_window(qt, kw, vwt, win_bias)
    gates = gt.reshape(B, 3, N_GROUPS, HEADS_PER_GROUP, S // t, t).transpose(0, 2, 4, 1, 3, 5)
    gates = jnp.pad(gates.reshape(B, N_GROUPS, S // t, 3, HEADS_PER_GROUP * t),
                    ((0, 0), (0, 0), (0, 0), (0, 5), (0, 0)))
    return _nsa_select_combine(rel_bias, qt, sel_bias, ks, vst, sel_corr, oc, ow, gates)


def kernel(x, c, ada_w, ada_b, ln_g, ln_b, fox_w_in, fox_b_f, fox_w_out, nsa_w_in, nsa_b_gate,
           nsa_cmp_pos, nsa_cmp_w1, nsa_cmp_w2, nsa_w_out, rel_bias, ffn_w_in, ffn_w_out):
    B, S, _ = x.shape
    assert S % FOX_T == 0 and S % ROW_TILE == 0 and FOX_T == ROW_TILE
    mod = _modulation(c, ada_w, ada_b)
    for layer in range(DEPTH):
        sh_a, sc_a, g_a, sh_f, sc_f, g_f = [m.reshape(B, 1, D_MODEL) for m in jnp.split(mod[layer], 6, axis=-1)]
        j = layer // 2
        if layer % 2 == 0:
            kp, qt, vt = _fox_proj(x, sc_a, sh_a, fox_w_in[j], fox_b_f[j])
            attn_t = _fox_attention(qt, kp, vt)
            w_out = fox_w_out[j]
        else:
            attn_t = _nsa_attention_t(x, sc_a, sh_a, nsa_w_in[j], nsa_b_gate[j], nsa_cmp_pos[j],
                                      nsa_cmp_w1[j], nsa_cmp_w2[j], rel_bias)
            w_out = nsa_w_out[j]
        x = _outproj_ln(attn_t, w_out, x, g_a, ln_g[layer, 0], ln_b[layer, 0])
        x = _ffn_ln(x, sc_f, sh_f, g_f, ffn_w_in[layer], ffn_w_out[layer], ln_g[layer, 1], ln_b[layer, 1])
    return x
```

```python
import functools
import math

import numpy as np
import jax
import jax.numpy as jnp
from jax import lax
from jax.experimental import pallas as pl
from jax.experimental.pallas import tpu as pltpu

F32 = jnp.float32
BF16 = jnp.bfloat16
HIGHEST = lax.Precision.HIGHEST

D_MODEL = 1024
HEAD_DIM = 64
N_HEADS = 16
N_GROUPS = 4
HEADS_PER_GROUP = 4
KV_WIDTH = N_GROUPS * HEAD_DIM
CMP_LEN = 32
CMP_STRIDE = 16
CMP_HIDDEN = 128
SEL_BLOCK = 64
SEL_TOPK = 16
WINDOW = 512
REL_BUCKETS = 32
FF_HIDDEN = 2816
DEPTH = 2
DN_ALPHA = (2 * DEPTH) ** 0.25
LN_EPS = 1e-5
NEG = -1e30
TINY = 1e-30

LANES = 128
VMEM_LIMIT = 56 * 1024 * 1024

ROW_TILE = 512
FOX_T = 1024
NSA_T = 512
FFN_CHUNK = 1408
N_FORCED = 3

BUCKET_START = (0, 1, 2, 3, 4, 5, 6, 7, 8, 9, 10, 11, 12, 13, 14, 15,
                16, 19, 21, 24, 27, 31, 35, 40, 46, 52, 59, 67, 77, 87, 99, 113)
FAR_BUCKET = REL_BUCKETS - 1
FAR_DIST = BUCKET_START[FAR_BUCKET]

LOG2E = math.log2(math.e)
QUERY_SUB = 256
KEY_CHUNK = 256
NT_DIMS = (((1,), (1,)), ((), ()))


def _params(*sem):
    return pltpu.CompilerParams(dimension_semantics=sem, vmem_limit_bytes=VMEM_LIMIT)


def _dot(a, b, **kw):
    return jnp.dot(a, b, preferred_element_type=F32, **kw)


def _dot_nt(a, b):
    return lax.dot_general(a, b, NT_DIMS, preferred_element_type=F32)


def _split3(v):
    hi = v.astype(BF16)
    r = v - hi.astype(F32)
    mid = r.astype(BF16)
    lo = (r - mid.astype(F32)).astype(BF16)
    return hi, mid, lo


def _layer_norm(z, g, b):
    mu = jnp.mean(z, axis=-1, keepdims=True)
    zc = z - mu
    var = jnp.mean(zc * zc, axis=-1, keepdims=True)
    return zc * lax.rsqrt(var + LN_EPS) * g + b


def _bucket_bias(dist, rb_ref, head):
    bias = jnp.full(dist.shape, rb_ref[0, head], F32)
    for k in range(1, REL_BUCKETS):
        bias = jnp.where(dist >= BUCKET_START[k], rb_ref[k, head], bias)
    return bias


def _mod_kernel(c_ref, w_ref, b_ref, o_ref):
    c = c_ref[...]
    cs = c / (1.0 + jnp.exp(-c))
    o_ref[0] = _dot(cs, w_ref[0], precision=HIGHEST) + b_ref[0]


def _modulation(c, ada_w, ada_b):
    B = c.shape[0]
    depth, _, n = ada_w.shape
    rows = 8
    c_pad = jnp.pad(c, ((0, rows - B), (0, 0)))
    tn = 1536
    out = pl.pallas_call(
        _mod_kernel, name="adaln_mod",
        grid=(depth, n // tn),
        in_specs=[
            pl.BlockSpec((rows, D_MODEL), lambda l, j: (0, 0)),
            pl.BlockSpec((1, D_MODEL, tn), lambda l, j: (l, 0, j)),
            pl.BlockSpec((1, 1, tn), lambda l, j: (l, 0, j)),
        ],
        out_specs=pl.BlockSpec((1, rows, tn), lambda l, j: (l, 0, j)),
        out_shape=jax.ShapeDtypeStruct((depth, rows, n), F32),
        compiler_params=_params("arbitrary", "arbitrary"),
    )(c_pad, ada_w, ada_b.reshape(depth, 1, n))
    return out[:, :B]


def _fox_proj_kernel(x_ref, sc_ref, sh_ref, wk_ref, wqt_ref, wvt_ref, wfh_ref, wfl_ref,
                     bf_ref, place_ref, kp_ref, qt_ref, vt_ref, carry_ref):
    tm = x_ref.shape[1]

    @pl.when(pl.program_id(1) == 0)
    def _():
        carry_ref[...] = jnp.zeros_like(carry_ref)

    h = x_ref[0] * (1.0 + sc_ref[0]) + sh_ref[0]
    hb = h.astype(BF16)
    hl = (h - hb.astype(F32)).astype(BF16)

    f = _dot(hb, wfh_ref[...]) + _dot(hl, wfh_ref[...]) + _dot(hb, wfl_ref[...])
    z = f + bf_ref[...]
    logf = jnp.minimum(z, 0.0) - jnp.log(1.0 + jnp.exp(-jnp.abs(z)))
    row = lax.broadcasted_iota(jnp.int32, (tm, tm), 0)
    col = lax.broadcasted_iota(jnp.int32, (tm, tm), 1)
    lower = (col <= row).astype(F32)
    cum = _dot(lower, logf, precision=HIGHEST) + carry_ref[0:1, :]
    carry_ref[...] = jnp.broadcast_to(cum[tm - 1:tm, :], carry_ref.shape)

    hi, mid, lo = [p.astype(F32) for p in _split3(-LOG2E * cum)]
    lane = lax.broadcasted_iota(jnp.int32, (tm, LANES), 1)
    pieces = jnp.where(lane < N_HEADS, hi, jnp.where(lane < 2 * N_HEADS, mid, lo)).astype(BF16)
    kp = _dot(hb, wk_ref[...]) + _dot(pieces, place_ref[...])
    qt = _dot_nt(wqt_ref[...], hb)
    qrow = lax.broadcasted_iota(jnp.int32, qt.shape, 0) % LANES
    qt = jnp.where((qrow >= HEAD_DIM) & (qrow < HEAD_DIM + 3), 1.0, qt)
    vt = _dot_nt(wvt_ref[...], hb)
    for hd in range(N_HEADS):
        kp_ref[0, hd] = kp[:, hd * LANES:(hd + 1) * LANES].astype(BF16)
        qt_ref[0, hd] = qt[hd * LANES:(hd + 1) * LANES, :].astype(BF16)
        for c in range(tm // KEY_CHUNK):
            vt_ref[0, hd, c] = vt[hd * HEAD_DIM:(hd + 1) * HEAD_DIM,
                                  c * KEY_CHUNK:(c + 1) * KEY_CHUNK].astype(BF16)


def _fox_proj(x, sc, sh, w_in, b_f):
    B, S, _ = x.shape
    tm = ROW_TILE
    scale = LOG2E * HEAD_DIM ** -0.5
    wq = (w_in[:, :D_MODEL] * scale).reshape(D_MODEL, N_HEADS, HEAD_DIM)
    wk = w_in[:, D_MODEL:2 * D_MODEL].reshape(D_MODEL, N_HEADS, HEAD_DIM)
    wv = w_in[:, 2 * D_MODEL:3 * D_MODEL]
    wf = w_in[:, 3 * D_MODEL:]
    pad = ((0, 0), (0, 0), (0, LANES - HEAD_DIM))
    wk_p = jnp.pad(wk, pad).reshape(D_MODEL, N_HEADS * LANES).astype(BF16)
    wqt_p = jnp.pad(wq, pad).reshape(D_MODEL, N_HEADS * LANES).T.astype(BF16)
    wvt = wv.T.astype(BF16)
    wf_rep = jnp.pad(jnp.tile(wf, (1, 3)), ((0, 0), (0, LANES - 3 * N_HEADS)))
    wf_hi = wf_rep.astype(BF16)
    wf_lo = (wf_rep - wf_hi.astype(F32)).astype(BF16)
    bf_rep = jnp.pad(jnp.tile(b_f, 3), (0, LANES - 3 * N_HEADS)).reshape(1, LANES)
    place = np.zeros((LANES, N_HEADS * LANES), np.float32)
    for r in range(3):
        for hd in range(N_HEADS):
            place[r * N_HEADS + hd, hd * LANES + HEAD_DIM + r] = 1.0
    place = jnp.asarray(place, BF16)

    full = lambda a: pl.BlockSpec(a.shape, lambda b, i: (0,) * a.ndim)
    return pl.pallas_call(
        _fox_proj_kernel, name="fox_proj",
        grid=(B, S // tm),
        in_specs=[
            pl.BlockSpec((1, tm, D_MODEL), lambda b, i: (b, i, 0)),
            pl.BlockSpec((1, 1, D_MODEL), lambda b, i: (b, 0, 0)),
            pl.BlockSpec((1, 1, D_MODEL), lambda b, i: (b, 0, 0)),
            full(wk_p), full(wqt_p), full(wvt), full(wf_hi), full(wf_lo), full(bf_rep), full(place),
        ],
        out_specs=[
            pl.BlockSpec((1, N_HEADS, tm, LANES), lambda b, i: (b, 0, i, 0)),
            pl.BlockSpec((1, N_HEADS, LANES, tm), lambda b, i: (b, 0, 0, i)),
            pl.BlockSpec((1, N_HEADS, tm // KEY_CHUNK, HEAD_DIM, KEY_CHUNK), lambda b, i: (b, 0, i, 0, 0)),
        ],
        out_shape=[
            jax.ShapeDtypeStruct((B, N_HEADS, S, LANES), BF16),
            jax.ShapeDtypeStruct((B, N_HEADS, LANES, S), BF16),
            jax.ShapeDtypeStruct((B, N_HEADS, S // KEY_CHUNK, HEAD_DIM, KEY_CHUNK), BF16),
        ],
        scratch_shapes=[pltpu.VMEM((8, LANES), F32)],
        compiler_params=_params("arbitrary", "arbitrary"),
    )(x, sc, sh, wk_p, wqt_p, wvt, wf_hi, wf_lo, bf_rep, place)


def _flash_sweep(first, n_far_pairs, scores, values, tail, s_buf, p_buf, acc_ref):
    width = acc_ref.shape[1]
    assert len(tail) % 2 == 0 and width % QUERY_SUB == 0
    s_buf[0] = scores(first)
    s_buf[1] = scores(first + 1)
    p_buf[...] = jnp.zeros_like(p_buf)
    acc_ref[...] = jnp.zeros_like(acc_ref)

    def step(j, par, stats, bias=None, prefetch=True):
        m, l, corr1, corr2 = stats
        acc_ref[...] = acc_ref[...] * corr2 + _dot(values(j - 2), p_buf[par])
        m_new, l_new, corr = [], [], []
        for c in range(width // QUERY_SUB):
            cols = slice(c * QUERY_SUB, (c + 1) * QUERY_SUB)
            s = s_buf[par, :, cols]
            extra = None if bias is None else bias(c)
            if extra is not None:
                s = s + extra
            m_c = jnp.maximum(m[:, cols], jnp.max(s, axis=0, keepdims=True))
            p = jnp.exp2(s - m_c)
            p_buf[par, :, cols] = p.astype(BF16)
            corr_c = jnp.exp2(m[:, cols] - m_c)
            l_new.append(l[:, cols] * corr_c + jnp.sum(p, axis=0, keepdims=True))
            m_new.append(m_c)
            corr.append(corr_c)
        if prefetch:
            s_buf[par] = scores(j + 2)
        cat = lambda parts: jnp.concatenate(parts, axis=1)
        return cat(m_new), cat(l_new), cat(corr), corr1

    def pair(jj, stats):
        j = first + 2 * jj
        return step(j + 1, 1, step(j, 0, stats))

    ones = jnp.ones((1, width), F32)
    stats = (jnp.full((1, width), NEG, F32), jnp.zeros((1, width), F32), ones, ones)
    stats = lax.fori_loop(0, n_far_pairs, pair, stats)
    last = first + 2 * n_far_pairs + len(tail)
    for r, bias in enumerate(tail):
        stats = step(last - len(tail) + r, r % 2, stats, bias, prefetch=r + 2 < len(tail))
    _, l, corr1, corr2 = stats
    acc = acc_ref[...] * corr2 + _dot(values(last - 2), p_buf[0])
    acc = acc * corr1 + _dot(values(last - 1), p_buf[1])
    return acc / l


def _fox_attn_kernel(qt_ref, kp_ref, vt_ref, o_ref, s_buf, p_buf, acc_ref):
    i = pl.program_id(2)
    per_tile = qt_ref.shape[3] // KEY_CHUNK

    def scores(c):
        rows = pl.ds(pl.multiple_of(c * KEY_CHUNK, KEY_CHUNK), KEY_CHUNK)
        return _dot(kp_ref[0, 0, rows, :], qt_ref[0, 0])

    def values(c):
        return vt_ref[0, 0, jnp.maximum(c, 0)]

    key = lax.broadcasted_iota(jnp.int32, (KEY_CHUNK, QUERY_SUB), 0)
    qry = lax.broadcasted_iota(jnp.int32, (KEY_CHUNK, QUERY_SUB), 1)
    causal = lambda r: (lambda c: jnp.where(key + r * KEY_CHUNK <= qry + c * QUERY_SUB, 0.0, NEG))
    assert per_tile % 2 == 0
    out = _flash_sweep(0, i * (per_tile // 2), scores, values, [causal(r) for r in range(per_tile)],
                       s_buf, p_buf, acc_ref)
    o_ref[0] = out.astype(o_ref.dtype)


def _fox_attention(qt, kp, vt):
    B, H, S, _ = kp.shape
    t = FOX_T
    return pl.pallas_call(
        _fox_attn_kernel, name="fox_attn",
        grid=(B, H, S // t),
        in_specs=[
            pl.BlockSpec((1, 1, LANES, t), lambda b, h, i: (b, h, 0, i)),
            pl.BlockSpec((1, 1, S, LANES), lambda b, h, i: (b, h, 0, 0)),
            pl.BlockSpec((1, 1, S // KEY_CHUNK, HEAD_DIM, KEY_CHUNK), lambda b, h, i: (b, h, 0, 0, 0)),
        ],
        out_specs=pl.BlockSpec((1, HEAD_DIM, t), lambda b, h, i: (b, h, i)),
        out_shape=jax.ShapeDtypeStruct((B, H * HEAD_DIM, S), BF16),
        scratch_shapes=[pltpu.VMEM((2, KEY_CHUNK, t), F32), pltpu.VMEM((2, KEY_CHUNK, t), BF16),
                        pltpu.VMEM((HEAD_DIM, t), F32)],
        compiler_params=_params("arbitrary", "arbitrary", "arbitrary"),
    )(qt, kp, vt)


def _outproj_kernel(at_ref, w_ref, x_ref, gate_ref, g_ref, b_ref, o_ref):
    tm = x_ref.shape[1]
    r = lax.broadcasted_iota(jnp.int32, (tm, tm), 0)
    c = lax.broadcasted_iota(jnp.int32, (tm, tm), 1)
    eye = (r == c).astype(BF16)
    a = _dot_nt(eye, at_ref[0]).astype(BF16)
    y = _dot(a, w_ref[...])
    z = DN_ALPHA * x_ref[0] + (1.0 + gate_ref[0]) * y
    o_ref[0] = _layer_norm(z, g_ref[...], b_ref[...])


def _outproj_ln(attn_t, w_out, x, gate, ln_g, ln_b):
    B, S, _ = x.shape
    tm = ROW_TILE
    vec = pl.BlockSpec((1, D_MODEL), lambda b, i: (0, 0))
    return pl.pallas_call(
        _outproj_kernel, name="outproj_ln",
        grid=(B, S // tm),
        in_specs=[
            pl.BlockSpec((1, D_MODEL, tm), lambda b, i: (b, 0, i)),
            pl.BlockSpec((D_MODEL, D_MODEL), lambda b, i: (0, 0)),
            pl.BlockSpec((1, tm, D_MODEL), lambda b, i: (b, i, 0)),
            pl.BlockSpec((1, 1, D_MODEL), lambda b, i: (b, 0, 0)),
            vec, vec,
        ],
        out_specs=pl.BlockSpec((1, tm, D_MODEL), lambda b, i: (b, i, 0)),
        out_shape=jax.ShapeDtypeStruct((B, S, D_MODEL), F32),
        compiler_params=_params("arbitrary", "arbitrary"),
    )(attn_t, w_out.astype(BF16), x, gate, ln_g.reshape(1, -1), ln_b.reshape(1, -1))


def _ffn_kernel(x_ref, sc_ref, sh_ref, gate_ref, wa_ref, wb_ref, wo_ref, g_ref, b_ref,
                o_ref, acc_ref):
    c = pl.program_id(2)
    x = x_ref[0]
    hb = (x * (1.0 + sc_ref[0]) + sh_ref[0]).astype(BF16)
    a = _dot(hb, wa_ref[...])
    b = _dot(hb, wb_ref[...])
    u = (a / (1.0 + jnp.exp(-a)) * b).astype(BF16)
    y = _dot(u, wo_ref[...])

    @pl.when(c == 0)
    def _():
        acc_ref[...] = y

    @pl.when(c > 0)
    def _():
        acc_ref[...] += y

    @pl.when(c == pl.num_programs(2) - 1)
    def _():
        z = DN_ALPHA * x + (1.0 + gate_ref[0]) * acc_ref[...]
        o_ref[0] = _layer_norm(z, g_ref[...], b_ref[...])


def _ffn_ln(x, sc, sh, gate, w_in, w_out, ln_g, ln_b):
    B, S, _ = x.shape
    tm = ROW_TILE
    nc = FF_HIDDEN // FFN_CHUNK
    w_in = w_in.astype(BF16)
    mod = pl.BlockSpec((1, 1, D_MODEL), lambda b, i, c: (b, 0, 0))
    vec = pl.BlockSpec((1, D_MODEL), lambda b, i, c: (0, 0))
    return pl.pallas_call(
        _ffn_kernel, name="ffn_ln",
        grid=(B, S // tm, nc),
        in_specs=[
            pl.BlockSpec((1, tm, D_MODEL), lambda b, i, c: (b, i, 0)),
            mod, mod, mod,
            pl.BlockSpec((D_MODEL, FFN_CHUNK), lambda b, i, c: (0, c)),
            pl.BlockSpec((D_MODEL, FFN_CHUNK), lambda b, i, c: (0, nc + c)),
            pl.BlockSpec((FFN_CHUNK, D_MODEL), lambda b, i, c: (c, 0)),
            vec, vec,
        ],
        out_specs=pl.BlockSpec((1, tm, D_MODEL), lambda b, i, c: (b, i, 0)),
        out_shape=jax.ShapeDtypeStruct((B, S, D_MODEL), F32),
        scratch_shapes=[pltpu.VMEM((tm, D_MODEL), F32)],
        compiler_params=_params("arbitrary", "arbitrary", "arbitrary"),
    )(x, sc, sh, gate, w_in, w_in, w_out.astype(BF16), ln_g.reshape(1, -1), ln_b.reshape(1, -1))


def _nsa_proj_kernel(x_ref, sc_ref, sh_ref, wqt_ref, wnat_ref, wvt_ref, wgt_ref, bg_ref,
                     qt_ref, kc_ref, vc_ref, ks_ref, kw_ref, vst_ref, vwt_ref, gt_ref):
    tm = x_ref.shape[1]
    t = NSA_T
    hb = (x_ref[0] * (1.0 + sc_ref[0]) + sh_ref[0]).astype(BF16)

    qt = _dot_nt(wqt_ref[...], hb).astype(BF16)
    for g in range(N_GROUPS):
        for hh in range(HEADS_PER_GROUP):
            r0 = (g * HEADS_PER_GROUP + hh) * HEAD_DIM
            for c in range(tm // t):
                qt_ref[0, g, c, :, hh * t:(hh + 1) * t] = qt[r0:r0 + HEAD_DIM, c * t:(c + 1) * t]

    nat = _dot(hb, wnat_ref[...])
    lane = lax.broadcasted_iota(jnp.int32, (tm, LANES), 1)
    ones = ((lane == HEAD_DIM) | (lane == HEAD_DIM + 1)).astype(F32)
    off_ks = 2 * KV_WIDTH
    off_kw = off_ks + N_GROUPS * LANES
    for g in range(N_GROUPS):
        kc_ref[0, g] = nat[:, g * HEAD_DIM:(g + 1) * HEAD_DIM]
        vc_ref[0, g] = nat[:, KV_WIDTH + g * HEAD_DIM:KV_WIDTH + (g + 1) * HEAD_DIM]
        ks_ref[0, g] = (nat[:, off_ks + g * LANES:off_ks + (g + 1) * LANES] + ones).astype(BF16)
        kw_ref[0, g] = nat[:, off_kw + g * LANES:off_kw + (g + 1) * LANES].astype(BF16)

    vt = _dot_nt(wvt_ref[...], hb).astype(BF16)
    for g in range(N_GROUPS):
        for c in range(tm // KEY_CHUNK):
            vst_ref[0, g, c] = vt[g * HEAD_DIM:(g + 1) * HEAD_DIM, c * KEY_CHUNK:(c + 1) * KEY_CHUNK]
        for c in range(tm // t):
            vwt_ref[0, g, c] = vt[KV_WIDTH + g * HEAD_DIM:KV_WIDTH + (g + 1) * HEAD_DIM, c * t:(c + 1) * t]

    gl = _dot_nt(wgt_ref[...], hb) + bg_ref[...]
    gt_ref[0] = 1.0 / (1.0 + jnp.exp(-gl))


def _nsa_proj(x, sc, sh, w_in, b_gate):
    B, S, _ = x.shape
    tm = ROW_TILE
    t = NSA_T
    scale = LOG2E * HEAD_DIM ** -0.5
    cuts = [D_MODEL + n * KV_WIDTH for n in range(7)]
    wq, wkc, wvc, wks, wvs, wkw, wvw, wg = jnp.split(w_in, cuts, axis=1)
    wqt = (wq * scale).T.astype(BF16)
    padk = lambda w: jnp.pad(w.reshape(D_MODEL, N_GROUPS, HEAD_DIM),
                             ((0, 0), (0, 0), (0, LANES - HEAD_DIM))).reshape(D_MODEL, N_GROUPS * LANES)
    wnat = jnp.concatenate([wkc, wvc, padk(wks), padk(wkw)], axis=1).astype(BF16)
    wvt = jnp.concatenate([wvs, wvw], axis=1).T.astype(BF16)
    n_gate = 3 * N_HEADS
    wgt = wg.T.astype(BF16)
    bg = jnp.broadcast_to(b_gate.reshape(n_gate, 1), (n_gate, tm))

    full = lambda a: pl.BlockSpec(a.shape, lambda b, i: (0,) * a.ndim)
    nat_spec = pl.BlockSpec((1, N_GROUPS, tm, HEAD_DIM), lambda b, i: (b, 0, i, 0))
    pad_spec = pl.BlockSpec((1, N_GROUPS, tm, LANES), lambda b, i: (b, 0, i, 0))
    vt_spec = lambda n: pl.BlockSpec((1, N_GROUPS, tm // n, HEAD_DIM, n), lambda b, i: (b, 0, i, 0, 0))
    return pl.pallas_call(
        _nsa_proj_kernel, name="nsa_proj",
        grid=(B, S // tm),
        in_specs=[
            pl.BlockSpec((1, tm, D_MODEL), lambda b, i: (b, i, 0)),
            pl.BlockSpec((1, 1, D_MODEL), lambda b, i: (b, 0, 0)),
            pl.BlockSpec((1, 1, D_MODEL), lambda b, i: (b, 0, 0)),
            full(wqt), full(wnat), full(wvt), full(wgt), full(bg),
        ],
        out_specs=[
            pl.BlockSpec((1, N_GROUPS, tm // t, HEAD_DIM, HEADS_PER_GROUP * t), lambda b, i: (b, 0, i, 0, 0)),
            nat_spec, nat_spec, pad_spec, pad_spec, vt_spec(KEY_CHUNK), vt_spec(t),
            pl.BlockSpec((1, n_gate, tm), lambda b, i: (b, 0, i)),
        ],
        out_shape=[
            jax.ShapeDtypeStruct((B, N_GROUPS, S // t, HEAD_DIM, HEADS_PER_GROUP * t), BF16),
            jax.ShapeDtypeStruct((B, N_GROUPS, S, HEAD_DIM), F32),
            jax.ShapeDtypeStruct((B, N_GROUPS, S, HEAD_DIM), F32),
            jax.ShapeDtypeStruct((B, N_GROUPS, S, LANES), BF16),
            jax.ShapeDtypeStruct((B, N_GROUPS, S, LANES), BF16),
            jax.ShapeDtypeStruct((B, N_GROUPS, S // KEY_CHUNK, HEAD_DIM, KEY_CHUNK), BF16),
            jax.ShapeDtypeStruct((B, N_GROUPS, S // t, HEAD_DIM, t), BF16),
            jax.ShapeDtypeStruct((B, n_gate, S), F32),
        ],
        compiler_params=_params("arbitrary", "arbitrary"),
    )(x, sc, sh, wqt, wnat, wvt, wgt, bg)


def _compress_kernel(tk_ref, tv_ref, pos_ref, w1_ref, w2k_ref, w2vt_ref, kc_ref, vct_ref):
    n = tk_ref.shape[2]
    half = CMP_STRIDE * HEAD_DIM

    def hidden(t_ref, idx):
        t16 = t_ref[0, 0]
        xa = (t16 + pos_ref[idx, 0:1, :]).astype(BF16)
        xb = (t16 + pos_ref[idx, 1:2, :]).astype(BF16)
        first = _dot(xa, w1_ref[idx, :half, :])
        second = _dot(xb, w1_ref[idx, half:, :])
        pre = first + pltpu.roll(second, n - 1, 0)
        return (pre / (1.0 + jnp.exp(-pre))).astype(BF16)

    kc = _dot(hidden(tk_ref, 0), w2k_ref[...])
    lane = lax.broadcasted_iota(jnp.int32, kc.shape, 1)
    ones = ((lane == HEAD_DIM) | (lane == HEAD_DIM + 1)).astype(F32)
    kc_ref[0, 0] = (kc + ones).astype(BF16)
    vct_ref[0, 0] = _dot_nt(w2vt_ref[...], hidden(tv_ref, 1)).astype(BF16)


def _compress(kc, vc, cmp_pos, cmp_w1, cmp_w2):
    B, G, S, _ = kc.shape
    n = S // CMP_STRIDE
    width = CMP_STRIDE * HEAD_DIM
    tk = kc.reshape(B, G, n, width)
    tv = vc.reshape(B, G, n, width)
    pos = cmp_pos.reshape(2, 2, width)
    w1 = cmp_w1.astype(BF16)
    w2k = jnp.pad(cmp_w2[0], ((0, 0), (0, LANES - HEAD_DIM))).astype(BF16)
    w2vt = cmp_w2[1].T.astype(BF16)
    full = lambda a: pl.BlockSpec(a.shape, lambda b, g: (0,) * a.ndim)
    t_spec = pl.BlockSpec((1, 1, n, width), lambda b, g: (b, g, 0, 0))
    return pl.pallas_call(
        _compress_kernel, name="nsa_compress",
        grid=(B, G),
        in_specs=[t_spec, t_spec, full(pos), full(w1), full(w2k), full(w2vt)],
        out_specs=[
            pl.BlockSpec((1, 1, n, LANES), lambda b, g: (b, g, 0, 0)),
            pl.BlockSpec((1, 1, HEAD_DIM, n), lambda b, g: (b, g, 0, 0)),
        ],
        out_shape=[
            jax.ShapeDtypeStruct((B, G, n, LANES), BF16),
            jax.ShapeDtypeStruct((B, G, HEAD_DIM, n), BF16),
        ],
        compiler_params=_params("arbitrary", "arbitrary"),
    )(tk, tv, pos, w1, w2k, w2vt)


def _bias_tiles_kernel(rb_ref, sel_ref, win_ref):
    hd = pl.program_id(0)
    far = rb_ref[FAR_BUCKET, hd]

    def dist(n, back):
        key = lax.broadcasted_iota(jnp.int32, (n, n), 0)
        qry = lax.broadcasted_iota(jnp.int32, (n, n), 1)
        return qry - key + back * n

    kc = sel_ref.shape[2]
    d_prev, d_diag = dist(kc, 1), dist(kc, 0)
    sel_ref[0, 0] = LOG2E * (_bucket_bias(d_prev, rb_ref, hd) - far)
    sel_ref[1, 0] = jnp.where(d_diag >= 0, LOG2E * (_bucket_bias(d_diag, rb_ref, hd) - far), NEG)
    t = win_ref.shape[2]
    n_back = win_ref.shape[0] - 1
    for r in range(n_back + 1):
        d = dist(t, n_back - r)
        win_ref[r, 0] = jnp.where((d >= 0) & (d < WINDOW), LOG2E * _bucket_bias(d, rb_ref, hd), NEG)


def _bias_tiles(rel_bias):
    t = NSA_T
    n_win = WINDOW // t + 1
    assert WINDOW % t == 0
    spec = lambda n, m: pl.BlockSpec((n, 1, m, m), lambda h: (0, h, 0, 0))
    return pl.pallas_call(
        _bias_tiles_kernel, name="rel_bias_tiles",
        grid=(N_HEADS,),
        in_specs=[pl.BlockSpec(memory_space=pltpu.SMEM)],
        out_specs=[spec(2, KEY_CHUNK), spec(n_win, t)],
        out_shape=[
            jax.ShapeDtypeStruct((2, N_HEADS, KEY_CHUNK, KEY_CHUNK), F32),
            jax.ShapeDtypeStruct((n_win, N_HEADS, t, t), F32),
        ],
        compiler_params=_params("arbitrary"),
    )(rel_bias)


def _far_bias_rows(rb_ref, g, width):
    t = width // HEADS_PER_GROUP
    lane = lax.broadcasted_iota(jnp.int32, (HEAD_DIM, width), 1)
    row = lax.broadcasted_iota(jnp.int32, (HEAD_DIM, width), 0)
    far = jnp.zeros((HEAD_DIM, width), F32)
    for hh in range(HEADS_PER_GROUP):
        far = jnp.where(lane >= hh * t, LOG2E * rb_ref[FAR_BUCKET, g * HEADS_PER_GROUP + hh], far)
    hi = far.astype(BF16).astype(F32)
    return jnp.where(row == 0, hi, jnp.where(row == 1, far - hi, 0.0)).astype(BF16)


def _nsa_cmp_kernel(rb_ref, qt_ref, kc_ref, vct_ref, at_ref, oc_ref, sb_ref, s_ref, p_ref):
    g = pl.program_id(1)
    i = pl.program_id(2)
    width = qt_ref.shape[4]
    t = width // HEADS_PER_GROUP
    n = kc_ref.shape[2]
    nb = at_ref.shape[0]
    t0 = i * t

    qp = jnp.concatenate([qt_ref[0, 0, 0], _far_bias_rows(rb_ref, g, width)], axis=0)
    s_ref[...] = _dot(kc_ref[0, 0], qp)

    band = t // CMP_STRIDE + 16
    assert (FAR_DIST + CMP_LEN - 1) <= 16 * CMP_STRIDE and band <= n
    r0 = pl.multiple_of(jnp.clip(t0 // CMP_STRIDE - 16, 0, n - band), 8)
    blk_r = r0 + lax.broadcasted_iota(jnp.int32, (band, t), 0)
    qry_r = t0 + lax.broadcasted_iota(jnp.int32, (band, t), 1)
    dist_r = qry_r - (blk_r * CMP_STRIDE + CMP_LEN - 1)
    for hh in range(HEADS_PER_GROUP):
        hd = g * HEADS_PER_GROUP + hh
        corr = LOG2E * (_bucket_bias(dist_r, rb_ref, hd) - rb_ref[FAR_BUCKET, hd])
        s_ref[pl.ds(r0, band), hh * t:(hh + 1) * t] += corr

    blk = lax.broadcasted_iota(jnp.int32, (n, t), 0)
    qry = t0 + lax.broadcasted_iota(jnp.int32, (n, t), 1)
    valid = qry >= blk * CMP_STRIDE + CMP_LEN - 1
    imp = jnp.zeros((n, t), F32)
    for hh in range(HEADS_PER_GROUP):
        s = jnp.where(valid, s_ref[:, hh * t:(hh + 1) * t], NEG)
        m = jnp.max(s, axis=0, keepdims=True)
        p = jnp.where(valid, jnp.exp2(s - m), 0.0)
        l = jnp.sum(p, axis=0, keepdims=True)
        p = p * (1.0 / jnp.maximum(l, TINY))
        p_ref[:, hh * t:(hh + 1) * t] = p.astype(BF16)
        imp = imp + p
    oc_ref[0, 0, 0] = _dot(vct_ref[0, 0], p_ref[...])

    hi, mid, lo = _split3(imp)
    at = at_ref[...]
    imp_sel = _dot(at, hi) + _dot(at, mid) + _dot(at, lo)

    sblk = lax.broadcasted_iota(jnp.int32, (nb, t), 0)
    cur = (t0 + lax.broadcasted_iota(jnp.int32, (nb, t), 1)) // SEL_BLOCK
    forced = (sblk == 0) | (sblk == cur) | (sblk == cur - 1)
    cand = jnp.where((sblk >= 1) & (sblk <= cur - 2), imp_sel, -1.0)
    chosen = forced.astype(F32)
    sblk_f = sblk.astype(F32)
    for _ in range(SEL_TOPK - N_FORCED):
        best = jnp.max(cand, axis=0, keepdims=True)
        hit = (cand == best) & (best >= 0.0)
        first = jnp.min(jnp.where(hit, sblk_f, float(nb)), axis=0, keepdims=True)
        pick = sblk_f == first
        chosen = jnp.where(pick, 1.0, chosen)
        cand = jnp.where(pick, -1.0, cand)
    sb_ref[0, 0, 0] = jnp.where(chosen > 0.5, 0.0, NEG).astype(BF16)


def _nsa_cmp(rel_bias, qt, kcmp, vcmp_t, n_sel):
    B, G, nq, _, width = qt.shape
    t = width // HEADS_PER_GROUP
    n = kcmp.shape[2]
    n_cmp = n - 1
    nb = -(-n_sel // LANES) * LANES
    R = SEL_BLOCK // CMP_STRIDE
    at = np.zeros((nb, n), np.float32)
    for j in range(n_sel):
        lo, hi = max(R * j - 1, 0), min(R * j + R - 1, n_cmp - 1)
        at[j, lo:hi + 1] = 1.0
    at = jnp.asarray(at, BF16)
    return pl.pallas_call(
        _nsa_cmp_kernel, name="nsa_cmp_topk",
        grid=(B, G, nq),
        in_specs=[
            pl.BlockSpec(memory_space=pltpu.SMEM),
            pl.BlockSpec((1, 1, 1, HEAD_DIM, width), lambda b, g, i: (b, g, i, 0, 0)),
            pl.BlockSpec((1, 1, n, LANES), lambda b, g, i: (b, g, 0, 0)),
            pl.BlockSpec((1, 1, HEAD_DIM, n), lambda b, g, i: (b, g, 0, 0)),
            pl.BlockSpec((nb, n), lambda b, g, i: (0, 0)),
        ],
        out_specs=[
            pl.BlockSpec((1, 1, 1, HEAD_DIM, width), lambda b, g, i: (b, g, i, 0, 0)),
            pl.BlockSpec((1, 1, 1, nb, t), lambda b, g, i: (b, g, i, 0, 0)),
        ],
        out_shape=[
            jax.ShapeDtypeStruct((B, G, nq, HEAD_DIM, width), F32),
            jax.ShapeDtypeStruct((B, G, nq, nb, t), BF16),
        ],
        scratch_shapes=[pltpu.VMEM((n, width), F32), pltpu.VMEM((n, width), BF16)],
        compiler_params=_params("arbitrary", "arbitrary", "arbitrary"),
    )(rel_bias, qt, kcmp, vcmp_t, at)


def _nsa_win_kernel(qt_ref, *refs):
    n_win = (len(refs) - 2) // 2
    k_refs, v_refs, (wb_ref, ow_ref) = refs[:n_win], refs[n_win:2 * n_win], refs[2 * n_win:]
    i = pl.program_id(2)
    width = qt_ref.shape[4]
    t = width // HEADS_PER_GROUP
    qt = qt_ref[0, 0, 0]
    qp = jnp.concatenate([qt, jnp.zeros_like(qt)], axis=0)
    scores = []
    for c in range(n_win):
        s = _dot(k_refs[c][0, 0], qp)
        missing = jnp.where(i - (n_win - 1 - c) < 0, NEG, 0.0)
        s = jnp.concatenate([s[:, hh * t:(hh + 1) * t] + (wb_ref[c, hh] + missing)
                             for hh in range(HEADS_PER_GROUP)], axis=1)
        scores.append(s)
    m = jnp.max(scores[0], axis=0, keepdims=True)
    for s in scores[1:]:
        m = jnp.maximum(m, jnp.max(s, axis=0, keepdims=True))
    l = jnp.zeros_like(m)
    acc = jnp.zeros((HEAD_DIM, width), F32)
    for c in range(n_win):
        p = jnp.exp2(scores[c] - m)
        l = l + jnp.sum(p, axis=0, keepdims=True)
        acc = acc + _dot(v_refs[c][0, 0, 0], p.astype(BF16))
    ow_ref[0, 0, 0] = acc / l


def _nsa_window(qt, kw, vwt, win_bias):
    B, G, nq, _, width = qt.shape
    t = width // HEADS_PER_GROUP
    n_win = win_bias.shape[0]
    backs = list(range(n_win - 1, -1, -1))
    k_spec = lambda back: pl.BlockSpec((1, 1, t, LANES), lambda b, g, i: (b, g, jnp.maximum(i - back, 0), 0))
    v_spec = lambda back: pl.BlockSpec((1, 1, 1, HEAD_DIM, t),
                                       lambda b, g, i: (b, g, jnp.maximum(i - back, 0), 0, 0))
    return pl.pallas_call(
        _nsa_win_kernel, name="nsa_window",
        grid=(B, G, nq),
        in_specs=[pl.BlockSpec((1, 1, 1, HEAD_DIM, width), lambda b, g, i: (b, g, i, 0, 0))]
        + [k_spec(back) for back in backs] + [v_spec(back) for back in backs]
        + [pl.BlockSpec((n_win, HEADS_PER_GROUP, t, t), lambda b, g, i: (0, g, 0, 0))],
        out_specs=pl.BlockSpec((1, 1, 1, HEAD_DIM, width), lambda b, g, i: (b, g, i, 0, 0)),
        out_shape=jax.ShapeDtypeStruct((B, G, nq, HEAD_DIM, width), F32),
        compiler_params=_params("arbitrary", "arbitrary", "arbitrary"),
    )(qt, *([kw] * n_win), *([vwt] * n_win), win_bias)


def _nsa_sel_kernel(rb_ref, qt_ref, sb_ref, ks_ref, e_ref, vst_ref, cb_ref, oc_ref, ow_ref, gt_ref,
                    o_ref, qp_ref, s_buf, p_buf, acc_ref):
    g = pl.program_id(1)
    i = pl.program_id(2)
    width = qt_ref.shape[4]
    t = width // HEADS_PER_GROUP
    per_tile = t // KEY_CHUNK
    chunks_per_slab = LANES * SEL_BLOCK // KEY_CHUNK
    blocks_per_head = t // QUERY_SUB
    assert per_tile == 2 and t % QUERY_SUB == 0

    far_rows = _far_bias_rows(rb_ref, g, width)
    for slab in range(qp_ref.shape[0]):
        qp_ref[slab, 0:HEAD_DIM, :] = qt_ref[0, 0, 0]
        qp_ref[slab, HEAD_DIM:LANES, :] = far_rows
        sb = sb_ref[0, 0, 0, slab * LANES:(slab + 1) * LANES, :]
        qp_ref[slab, LANES:2 * LANES, :] = jnp.concatenate([sb] * HEADS_PER_GROUP, axis=1)

    def scores(c):
        c = jnp.maximum(c, 0)
        rows = pl.ds(pl.multiple_of(c * KEY_CHUNK, KEY_CHUNK), KEY_CHUNK)
        kp = jnp.concatenate([ks_ref[0, 0, rows, :], e_ref[rows, :]], axis=1)
        return _dot(kp, qp_ref[c // chunks_per_slab])

    def values(c):
        return vst_ref[0, 0, jnp.maximum(c, 0)]

    missing = jnp.where(i >= 1, 0.0, NEG)

    def tail_bias(r):
        def bias(c):
            hh, part = divmod(c, blocks_per_head)
            ahead = part * QUERY_SUB // KEY_CHUNK - (r - per_tile)
            if ahead == 0:
                tile = cb_ref[1, hh]
            elif ahead == 1:
                tile = cb_ref[0, hh]
            elif ahead < 0:
                tile = jnp.full((KEY_CHUNK, QUERY_SUB), NEG, F32)
            else:
                tile = None
            if r < per_tile:
                tile = missing if tile is None else tile + missing
            return tile
        return bias

    first = per_tile * jnp.minimum(i - 1, 0)
    o_sel = _flash_sweep(first, jnp.maximum(i - 1, 0), scores, values,
                         [tail_bias(r) for r in range(2 * per_tile)], s_buf, p_buf, acc_ref)

    out = gt_ref[0, 0, 0, 0:1, :] * oc_ref[0, 0, 0] + gt_ref[0, 0, 0, 1:2, :] * o_sel \
        + gt_ref[0, 0, 0, 2:3, :] * ow_ref[0, 0, 0]
    for hh in range(HEADS_PER_GROUP):
        o_ref[0, hh * HEAD_DIM:(hh + 1) * HEAD_DIM, :] = out[:, hh * t:(hh + 1) * t].astype(o_ref.dtype)


def _nsa_select_combine(rel_bias, qt, sel_bias, ks, vst, sel_corr, oc, ow, gates):
    B, G, nq, _, width = qt.shape
    t = width // HEADS_PER_GROUP
    S = nq * t
    nb = sel_bias.shape[3]
    blocks = (np.arange(S) // SEL_BLOCK) % LANES
    onehot = jnp.asarray(blocks[:, None] == np.arange(LANES)[None, :], BF16)
    tile = pl.BlockSpec((1, 1, 1, HEAD_DIM, width), lambda b, g, i: (b, g, i, 0, 0))
    return pl.pallas_call(
        _nsa_sel_kernel, name="nsa_select",
        grid=(B, G, nq),
        in_specs=[
            pl.BlockSpec(memory_space=pltpu.SMEM),
            tile,
            pl.BlockSpec((1, 1, 1, nb, t), lambda b, g, i: (b, g, i, 0, 0)),
            pl.BlockSpec((1, 1, S, LANES), lambda b, g, i: (b, g, 0, 0)),
            pl.BlockSpec((S, LANES), lambda b, g, i: (0, 0)),
            pl.BlockSpec((1, 1, S // KEY_CHUNK, HEAD_DIM, KEY_CHUNK), lambda b, g, i: (b, g, 0, 0, 0)),
            pl.BlockSpec((2, HEADS_PER_GROUP, KEY_CHUNK, KEY_CHUNK), lambda b, g, i: (0, g, 0, 0)),
            tile, tile,
            pl.BlockSpec((1, 1, 1, 8, width), lambda b, g, i: (b, g, i, 0, 0)),
        ],
        out_specs=pl.BlockSpec((1, HEADS_PER_GROUP * HEAD_DIM, t), lambda b, g, i: (b, g, i)),
        out_shape=jax.ShapeDtypeStruct((B, D_MODEL, S), BF16),
        scratch_shapes=[pltpu.VMEM((nb // LANES, 2 * LANES, width), BF16),
                        pltpu.VMEM((2, KEY_CHUNK, width), F32), pltpu.VMEM((2, KEY_CHUNK, width), BF16),
                        pltpu.VMEM((HEAD_DIM, width), F32)],
        compiler_params=_params("arbitrary", "arbitrary", "arbitrary"),
    )(rel_bias, qt, sel_bias, ks, onehot, vst, sel_corr, oc, ow, gates)


def _nsa_attention_t(x, sc, sh, w_in, b_gate, cmp_pos, cmp_w1, cmp_w2, rel_bias):
    B, S, _ = x.shape
    t = NSA_T
    n_sel = S // SEL_BLOCK
    assert S % ROW_TILE == 0 and n_sel >= SEL_TOPK and S // CMP_STRIDE >= t // CMP_STRIDE + 16
    qt, kc, vc, ks, kw, vst, vwt, gt = _nsa_proj(x, sc, sh, w_in, b_gate)
    kcmp, vcmp_t = _compress(kc, vc, cmp_pos, cmp_w1, cmp_w2)
    sel_corr, win_bias = _bias_tiles(rel_bias)
    oc, sel_bias = _nsa_cmp(rel_bias, qt, kcmp, vcmp_t, n_sel)
    ow = _nsa_window(qt, kw, vwt, win_bias)
    gates = gt.reshape(B, 3, N_GROUPS, HEADS_PER_GROUP, S // t, t).transpose(0, 2, 4, 1, 3, 5)
    gates = jnp.pad(gates.reshape(B, N_GROUPS, S // t, 3, HEADS_PER_GROUP * t),
                    ((0, 0), (0, 0), (0, 0), (0, 5), (0, 0)))
    return _nsa_select_combine(rel_bias, qt, sel_bias, ks, vst, sel_corr, oc, ow, gates)


def kernel(x, c, ada_w, ada_b, ln_g, ln_b, fox_w_in, fox_b_f, fox_w_out, nsa_w_in, nsa_b_gate,
           nsa_cmp_pos, nsa_cmp_w1, nsa_cmp_w2, nsa_w_out, rel_bias, ffn_w_in, ffn_w_out):
    B, S, _ = x.shape
    assert S % FOX_T == 0 and S % ROW_TILE == 0 and ROW_TILE == NSA_T
    mod = _modulation(c, ada_w, ada_b)
    for layer in range(DEPTH):
        sh_a, sc_a, g_a, sh_f, sc_f, g_f = [m.reshape(B, 1, D_MODEL) for m in jnp.split(mod[layer], 6, axis=-1)]
        j = layer // 2
        if layer % 2 == 0:
            kp, qt, vt = _fox_proj(x, sc_a, sh_a, fox_w_in[j], fox_b_f[j])
            attn_t = _fox_attention(qt, kp, vt)
            w_out = fox_w_out[j]
        else:
            attn_t = _nsa_attention_t(x, sc_a, sh_a, nsa_w_in[j], nsa_b_gate[j], nsa_cmp_pos[j],
                                      nsa_cmp_w1[j], nsa_cmp_w2[j], rel_bias)
            w_out = nsa_w_out[j]
        x = _outproj_ln(attn_t, w_out, x, g_a, ln_g[layer, 0], ln_b[layer, 0])
        x = _ffn_ln(x, sc_f, sh_f, g_f, ffn_w_in[layer], ffn_w_out[layer], ln_g[layer, 1], ln_b[layer, 1])
    return x
```

```python
import functools
import math

import numpy as np
import jax
import jax.numpy as jnp
from jax import lax
from jax.experimental import pallas as pl
from jax.experimental.pallas import tpu as pltpu

F32 = jnp.float32
BF16 = jnp.bfloat16
HIGHEST = lax.Precision.HIGHEST

D_MODEL = 1024
HEAD_DIM = 64
N_HEADS = 16
N_GROUPS = 4
HEADS_PER_GROUP = 4
KV_WIDTH = N_GROUPS * HEAD_DIM
CMP_LEN = 32
CMP_STRIDE = 16
CMP_HIDDEN = 128
SEL_BLOCK = 64
SEL_TOPK = 16
WINDOW = 512
REL_BUCKETS = 32
FF_HIDDEN = 2816
DEPTH = 2
DN_ALPHA = (2 * DEPTH) ** 0.25
LN_EPS = 1e-5
NEG = -1e30
TINY = 1e-30

LANES = 128
VMEM_LIMIT = 56 * 1024 * 1024

ROW_TILE = 512
FOX_T = 1024
NSA_T = 512
FFN_CHUNK = 1408
N_FORCED = 3

BUCKET_START = (0, 1, 2, 3, 4, 5, 6, 7, 8, 9, 10, 11, 12, 13, 14, 15,
                16, 19, 21, 24, 27, 31, 35, 40, 46, 52, 59, 67, 77, 87, 99, 113)
FAR_BUCKET = REL_BUCKETS - 1
FAR_DIST = BUCKET_START[FAR_BUCKET]

LOG2E = math.log2(math.e)
QUERY_SUB = 256
KEY_CHUNK = 256
CMP_ROWS = 128
V_ROWS = HEAD_DIM + 16
NT_DIMS = (((1,), (1,)), ((), ()))


def _params(*sem):
    return pltpu.CompilerParams(dimension_semantics=sem, vmem_limit_bytes=VMEM_LIMIT)


def _dot(a, b, **kw):
    return jnp.dot(a, b, preferred_element_type=F32, **kw)


def _dot_nt(a, b):
    return lax.dot_general(a, b, NT_DIMS, preferred_element_type=F32)


def _split3(v):
    hi = v.astype(BF16)
    r = v - hi.astype(F32)
    mid = r.astype(BF16)
    lo = (r - mid.astype(F32)).astype(BF16)
    return hi, mid, lo


def _layer_norm(z, g, b):
    mu = jnp.mean(z, axis=-1, keepdims=True)
    zc = z - mu
    var = jnp.mean(zc * zc, axis=-1, keepdims=True)
    return zc * lax.rsqrt(var + LN_EPS) * g + b


def _bucket_bias(dist, rb_ref, head):
    bias = jnp.full(dist.shape, rb_ref[0, head], F32)
    for k in range(1, REL_BUCKETS):
        bias = jnp.where(dist >= BUCKET_START[k], rb_ref[k, head], bias)
    return bias


def _mod_kernel(c_ref, w_ref, b_ref, o_ref):
    c = c_ref[...]
    cs = c / (1.0 + jnp.exp(-c))
    o_ref[0] = _dot(cs, w_ref[0], precision=HIGHEST) + b_ref[0]


def _modulation(c, ada_w, ada_b):
    B = c.shape[0]
    depth, _, n = ada_w.shape
    rows = 8
    c_pad = jnp.pad(c, ((0, rows - B), (0, 0)))
    tn = 1536
    out = pl.pallas_call(
        _mod_kernel, name="adaln_mod",
        grid=(depth, n // tn),
        in_specs=[
            pl.BlockSpec((rows, D_MODEL), lambda l, j: (0, 0)),
            pl.BlockSpec((1, D_MODEL, tn), lambda l, j: (l, 0, j)),
            pl.BlockSpec((1, 1, tn), lambda l, j: (l, 0, j)),
        ],
        out_specs=pl.BlockSpec((1, rows, tn), lambda l, j: (l, 0, j)),
        out_shape=jax.ShapeDtypeStruct((depth, rows, n), F32),
        compiler_params=_params("arbitrary", "arbitrary"),
    )(c_pad, ada_w, ada_b.reshape(depth, 1, n))
    return out[:, :B]


def _fox_proj_kernel(x_ref, sc_ref, sh_ref, wk_ref, wqt_ref, wvt_ref, wfh_ref, wfl_ref,
                     bf_ref, place_ref, kp_ref, qt_ref, vt_ref, carry_ref):
    tm = x_ref.shape[1]

    @pl.when(pl.program_id(1) == 0)
    def _():
        carry_ref[...] = jnp.zeros_like(carry_ref)

    h = x_ref[0] * (1.0 + sc_ref[0]) + sh_ref[0]
    hb = h.astype(BF16)
    hl = (h - hb.astype(F32)).astype(BF16)

    f = _dot(hb, wfh_ref[...]) + _dot(hl, wfh_ref[...]) + _dot(hb, wfl_ref[...])
    z = f + bf_ref[...]
    logf = jnp.minimum(z, 0.0) - jnp.log(1.0 + jnp.exp(-jnp.abs(z)))
    row = lax.broadcasted_iota(jnp.int32, (tm, tm), 0)
    col = lax.broadcasted_iota(jnp.int32, (tm, tm), 1)
    lower = (col <= row).astype(F32)
    cum = _dot(lower, logf, precision=HIGHEST) + carry_ref[0:1, :]
    carry_ref[...] = jnp.broadcast_to(cum[tm - 1:tm, :], carry_ref.shape)

    hi, mid, lo = [p.astype(F32) for p in _split3(-LOG2E * cum)]
    lane = lax.broadcasted_iota(jnp.int32, (tm, LANES), 1)
    pieces = jnp.where(lane < N_HEADS, hi, jnp.where(lane < 2 * N_HEADS, mid, lo)).astype(BF16)
    kp = _dot(hb, wk_ref[...]) + _dot(pieces, place_ref[...])
    qt = _dot_nt(wqt_ref[...], hb)
    qrow = lax.broadcasted_iota(jnp.int32, qt.shape, 0) % LANES
    qt = jnp.where((qrow >= HEAD_DIM) & (qrow < HEAD_DIM + 3), 1.0, qt)
    vt = _dot_nt(wvt_ref[...], hb)
    for hd in range(N_HEADS):
        kp_ref[0, hd] = kp[:, hd * LANES:(hd + 1) * LANES].astype(BF16)
        qt_ref[0, hd] = qt[hd * LANES:(hd + 1) * LANES, :].astype(BF16)
        for c in range(tm // KEY_CHUNK):
            vt_ref[0, hd, c, 0:HEAD_DIM, :] = vt[hd * HEAD_DIM:(hd + 1) * HEAD_DIM,
                                                 c * KEY_CHUNK:(c + 1) * KEY_CHUNK].astype(BF16)
            vt_ref[0, hd, c, HEAD_DIM:V_ROWS, :] = _ones_row_tail(KEY_CHUNK)


def _fox_proj(x, sc, sh, w_in, b_f):
    B, S, _ = x.shape
    tm = ROW_TILE
    scale = LOG2E * HEAD_DIM ** -0.5
    wq = (w_in[:, :D_MODEL] * scale).reshape(D_MODEL, N_HEADS, HEAD_DIM)
    wk = w_in[:, D_MODEL:2 * D_MODEL].reshape(D_MODEL, N_HEADS, HEAD_DIM)
    wv = w_in[:, 2 * D_MODEL:3 * D_MODEL]
    wf = w_in[:, 3 * D_MODEL:]
    pad = ((0, 0), (0, 0), (0, LANES - HEAD_DIM))
    wk_p = jnp.pad(wk, pad).reshape(D_MODEL, N_HEADS * LANES).astype(BF16)
    wqt_p = jnp.pad(wq, pad).reshape(D_MODEL, N_HEADS * LANES).T.astype(BF16)
    wvt = wv.T.astype(BF16)
    wf_rep = jnp.pad(jnp.tile(wf, (1, 3)), ((0, 0), (0, LANES - 3 * N_HEADS)))
    wf_hi = wf_rep.astype(BF16)
    wf_lo = (wf_rep - wf_hi.astype(F32)).astype(BF16)
    bf_rep = jnp.pad(jnp.tile(b_f, 3), (0, LANES - 3 * N_HEADS)).reshape(1, LANES)
    place = np.zeros((LANES, N_HEADS * LANES), np.float32)
    for r in range(3):
        for hd in range(N_HEADS):
            place[r * N_HEADS + hd, hd * LANES + HEAD_DIM + r] = 1.0
    place = jnp.asarray(place, BF16)

    full = lambda a: pl.BlockSpec(a.shape, lambda b, i: (0,) * a.ndim)
    return pl.pallas_call(
        _fox_proj_kernel, name="fox_proj",
        grid=(B, S // tm),
        in_specs=[
            pl.BlockSpec((1, tm, D_MODEL), lambda b, i: (b, i, 0)),
            pl.BlockSpec((1, 1, D_MODEL), lambda b, i: (b, 0, 0)),
            pl.BlockSpec((1, 1, D_MODEL), lambda b, i: (b, 0, 0)),
            full(wk_p), full(wqt_p), full(wvt), full(wf_hi), full(wf_lo), full(bf_rep), full(place),
        ],
        out_specs=[
            pl.BlockSpec((1, N_HEADS, tm, LANES), lambda b, i: (b, 0, i, 0)),
            pl.BlockSpec((1, N_HEADS, LANES, tm), lambda b, i: (b, 0, 0, i)),
            pl.BlockSpec((1, N_HEADS, tm // KEY_CHUNK, V_ROWS, KEY_CHUNK), lambda b, i: (b, 0, i, 0, 0)),
        ],
        out_shape=[
            jax.ShapeDtypeStruct((B, N_HEADS, S, LANES), BF16),
            jax.ShapeDtypeStruct((B, N_HEADS, LANES, S), BF16),
            jax.ShapeDtypeStruct((B, N_HEADS, S // KEY_CHUNK, V_ROWS, KEY_CHUNK), BF16),
        ],
        scratch_shapes=[pltpu.VMEM((8, LANES), F32)],
        compiler_params=_params("arbitrary", "arbitrary"),
    )(x, sc, sh, wk_p, wqt_p, wvt, wf_hi, wf_lo, bf_rep, place)


def _ones_row_tail(width):
    row = lax.broadcasted_iota(jnp.int32, (V_ROWS - HEAD_DIM, width), 0)
    return (row == 0).astype(F32).astype(BF16)


def _flash_sweep(first, n_far_pairs, scores, values, tail, s_buf, p_buf, acc_ref, far_pairs_even=False):
    width = acc_ref.shape[1]
    assert len(tail) % 2 == 0 and width % QUERY_SUB == 0
    s_buf[0] = scores(first)
    s_buf[1] = scores(first + 1)
    p_buf[...] = jnp.zeros_like(p_buf)
    acc_ref[...] = jnp.zeros_like(acc_ref)

    def step(j, par, stats, bias=None, prefetch=True):
        m, corr1, corr2 = stats
        acc_ref[...] = acc_ref[...] * corr2 + _dot(values(j - 2), p_buf[par])
        m_new, corr = [], []
        for c in range(width // QUERY_SUB):
            cols = slice(c * QUERY_SUB, (c + 1) * QUERY_SUB)
            s = s_buf[par, :, cols]
            extra = None if bias is None else bias(c)
            if extra is not None:
                s = s + extra
            m_c = jnp.maximum(m[:, cols], jnp.max(s, axis=0, keepdims=True))
            p_buf[par, :, cols] = jnp.exp2((s - m_c).astype(BF16))
            m_new.append(m_c)
            corr.append(jnp.exp2(m[:, cols] - m_c))
        if prefetch:
            s_buf[par] = scores(j + 2)
        cat = lambda parts: jnp.concatenate(parts, axis=1)
        return cat(m_new), cat(corr), corr1

    def pair(j, stats):
        return step(j + 1, 1, step(j, 0, stats))

    def two_pairs(jj, stats):
        j = first + 2 * (odd + 2 * jj)
        return pair(j + 2, pair(j, stats))

    ones = jnp.ones((1, width), F32)
    stats = (jnp.full((1, width), NEG, F32), ones, ones)
    if far_pairs_even:
        odd = 0
    else:
        odd = n_far_pairs % 2
        stats = lax.cond(odd == 1, lambda st: pair(first, st), lambda st: st, stats)
    stats = lax.fori_loop(0, n_far_pairs // 2, two_pairs, stats)
    last = first + 2 * n_far_pairs + len(tail)
    for r, bias in enumerate(tail):
        stats = step(last - len(tail) + r, r % 2, stats, bias, prefetch=r + 2 < len(tail))
    _, corr1, corr2 = stats
    acc = acc_ref[...] * corr2 + _dot(values(last - 2), p_buf[0])
    acc = acc * corr1 + _dot(values(last - 1), p_buf[1])
    return acc[:HEAD_DIM] / acc[HEAD_DIM:HEAD_DIM + 1]


def _fox_attn_kernel(qt_ref, kp_ref, vt_ref, o_ref, s_buf, p_buf, acc_ref):
    i = pl.program_id(2)
    t = qt_ref.shape[3]
    per_tile = t // KEY_CHUNK
    s_buf, p_buf, acc_ref = s_buf.at[:, :, 0:t], p_buf.at[:, :, 0:t], acc_ref.at[:, 0:t]

    def scores(c):
        rows = pl.ds(pl.multiple_of(c * KEY_CHUNK, KEY_CHUNK), KEY_CHUNK)
        return _dot(kp_ref[0, 0, rows, :], qt_ref[0, 0])

    def values(c):
        return vt_ref[0, 0, jnp.maximum(c, 0)]

    key = lax.broadcasted_iota(jnp.int32, (KEY_CHUNK, QUERY_SUB), 0)
    qry = lax.broadcasted_iota(jnp.int32, (KEY_CHUNK, QUERY_SUB), 1)
    causal = lambda r: (lambda c: jnp.where(key + r * KEY_CHUNK <= qry + c * QUERY_SUB, 0.0, NEG))
    assert per_tile % 4 == 0
    out = _flash_sweep(0, i * (per_tile // 2), scores, values, [causal(r) for r in range(per_tile)],
                       s_buf, p_buf, acc_ref, far_pairs_even=True)
    o_ref[0] = out.astype(o_ref.dtype)


def _fox_attention(qt, kp, vt):
    B, H, S, _ = kp.shape
    t = FOX_T
    return pl.pallas_call(
        _fox_attn_kernel, name="fox_attn",
        grid=(B, H, S // t),
        in_specs=[
            pl.BlockSpec((1, 1, LANES, t), lambda b, h, i: (b, h, 0, i)),
            pl.BlockSpec((1, 1, S, LANES), lambda b, h, i: (b, h, 0, 0)),
            pl.BlockSpec((1, 1, S // KEY_CHUNK, V_ROWS, KEY_CHUNK), lambda b, h, i: (b, h, 0, 0, 0)),
        ],
        out_specs=pl.BlockSpec((1, HEAD_DIM, t), lambda b, h, i: (b, h, i)),
        out_shape=jax.ShapeDtypeStruct((B, H * HEAD_DIM, S), BF16),
        scratch_shapes=[pltpu.VMEM((2, KEY_CHUNK, t + LANES), F32), pltpu.VMEM((2, KEY_CHUNK, t + LANES), BF16),
                        pltpu.VMEM((V_ROWS, t + LANES), F32)],
        compiler_params=_params("arbitrary", "arbitrary", "arbitrary"),
    )(qt, kp, vt)


def _outproj_kernel(at_ref, w_ref, x_ref, gate_ref, g_ref, b_ref, o_ref):
    tm = x_ref.shape[1]
    r = lax.broadcasted_iota(jnp.int32, (tm, tm), 0)
    c = lax.broadcasted_iota(jnp.int32, (tm, tm), 1)
    eye = (r == c).astype(BF16)
    a = _dot_nt(eye, at_ref[0]).astype(BF16)
    y = _dot(a, w_ref[...])
    z = DN_ALPHA * x_ref[0] + (1.0 + gate_ref[0]) * y
    o_ref[0] = _layer_norm(z, g_ref[...], b_ref[...])


def _outproj_ln(attn_t, w_out, x, gate, ln_g, ln_b):
    B, S, _ = x.shape
    tm = ROW_TILE
    vec = pl.BlockSpec((1, D_MODEL), lambda b, i: (0, 0))
    return pl.pallas_call(
        _outproj_kernel, name="outproj_ln",
        grid=(B, S // tm),
        in_specs=[
            pl.BlockSpec((1, D_MODEL, tm), lambda b, i: (b, 0, i)),
            pl.BlockSpec((D_MODEL, D_MODEL), lambda b, i: (0, 0)),
            pl.BlockSpec((1, tm, D_MODEL), lambda b, i: (b, i, 0)),
            pl.BlockSpec((1, 1, D_MODEL), lambda b, i: (b, 0, 0)),
            vec, vec,
        ],
        out_specs=pl.BlockSpec((1, tm, D_MODEL), lambda b, i: (b, i, 0)),
        out_shape=jax.ShapeDtypeStruct((B, S, D_MODEL), F32),
        compiler_params=_params("arbitrary", "arbitrary"),
    )(attn_t, w_out.astype(BF16), x, gate, ln_g.reshape(1, -1), ln_b.reshape(1, -1))


def _ffn_kernel(x_ref, sc_ref, sh_ref, gate_ref, wa_ref, wb_ref, wo_ref, g_ref, b_ref,
                o_ref, acc_ref):
    c = pl.program_id(2)
    x = x_ref[0]
    hb = (x * (1.0 + sc_ref[0]) + sh_ref[0]).astype(BF16)
    a = _dot(hb, wa_ref[...])
    b = _dot(hb, wb_ref[...])
    u = (a / (1.0 + jnp.exp(-a)) * b).astype(BF16)
    y = _dot(u, wo_ref[...])

    @pl.when(c == 0)
    def _():
        acc_ref[...] = y

    @pl.when(c > 0)
    def _():
        acc_ref[...] += y

    @pl.when(c == pl.num_programs(2) - 1)
    def _():
        z = DN_ALPHA * x + (1.0 + gate_ref[0]) * acc_ref[...]
        o_ref[0] = _layer_norm(z, g_ref[...], b_ref[...])


def _ffn_ln(x, sc, sh, gate, w_in, w_out, ln_g, ln_b):
    B, S, _ = x.shape
    tm = ROW_TILE
    nc = FF_HIDDEN // FFN_CHUNK
    w_in = w_in.astype(BF16)
    mod = pl.BlockSpec((1, 1, D_MODEL), lambda b, i, c: (b, 0, 0))
    vec = pl.BlockSpec((1, D_MODEL), lambda b, i, c: (0, 0))
    return pl.pallas_call(
        _ffn_kernel, name="ffn_ln",
        grid=(B, S // tm, nc),
        in_specs=[
            pl.BlockSpec((1, tm, D_MODEL), lambda b, i, c: (b, i, 0)),
            mod, mod, mod,
            pl.BlockSpec((D_MODEL, FFN_CHUNK), lambda b, i, c: (0, c)),
            pl.BlockSpec((D_MODEL, FFN_CHUNK), lambda b, i, c: (0, nc + c)),
            pl.BlockSpec((FFN_CHUNK, D_MODEL), lambda b, i, c: (c, 0)),
            vec, vec,
        ],
        out_specs=pl.BlockSpec((1, tm, D_MODEL), lambda b, i, c: (b, i, 0)),
        out_shape=jax.ShapeDtypeStruct((B, S, D_MODEL), F32),
        scratch_shapes=[pltpu.VMEM((tm, D_MODEL), F32)],
        compiler_params=_params("arbitrary", "arbitrary", "arbitrary"),
    )(x, sc, sh, gate, w_in, w_in, w_out.astype(BF16), ln_g.reshape(1, -1), ln_b.reshape(1, -1))


def _nsa_proj_kernel(x_ref, sc_ref, sh_ref, wqt_ref, wnat_ref, wvt_ref, wgt_ref, bg_ref,
                     qt_ref, kc_ref, vc_ref, ks_ref, kw_ref, vst_ref, vwt_ref, gt_ref):
    tm = x_ref.shape[1]
    t = NSA_T
    hb = (x_ref[0] * (1.0 + sc_ref[0]) + sh_ref[0]).astype(BF16)

    qt = _dot_nt(wqt_ref[...], hb).astype(BF16)
    for g in range(N_GROUPS):
        for hh in range(HEADS_PER_GROUP):
            r0 = (g * HEADS_PER_GROUP + hh) * HEAD_DIM
            for c in range(tm // t):
                qt_ref[0, g, c, :, hh * t:(hh + 1) * t] = qt[r0:r0 + HEAD_DIM, c * t:(c + 1) * t]

    nat = _dot(hb, wnat_ref[...])
    lane = lax.broadcasted_iota(jnp.int32, (tm, LANES), 1)
    ones = ((lane == HEAD_DIM) | (lane == HEAD_DIM + 1)).astype(F32)
    off_ks = 2 * KV_WIDTH
    off_kw = off_ks + N_GROUPS * LANES
    for g in range(N_GROUPS):
        kc_ref[0, g] = nat[:, g * HEAD_DIM:(g + 1) * HEAD_DIM]
        vc_ref[0, g] = nat[:, KV_WIDTH + g * HEAD_DIM:KV_WIDTH + (g + 1) * HEAD_DIM]
        ks_ref[0, g] = (nat[:, off_ks + g * LANES:off_ks + (g + 1) * LANES] + ones).astype(BF16)
        kw_ref[0, g] = nat[:, off_kw + g * LANES:off_kw + (g + 1) * LANES].astype(BF16)

    vt = _dot_nt(wvt_ref[...], hb).astype(BF16)
    for g in range(N_GROUPS):
        for c in range(tm // KEY_CHUNK):
            vst_ref[0, g, c, 0:HEAD_DIM, :] = vt[g * HEAD_DIM:(g + 1) * HEAD_DIM, c * KEY_CHUNK:(c + 1) * KEY_CHUNK]
            vst_ref[0, g, c, HEAD_DIM:V_ROWS, :] = _ones_row_tail(KEY_CHUNK)
        for c in range(tm // t):
            vwt_ref[0, g, c] = vt[KV_WIDTH + g * HEAD_DIM:KV_WIDTH + (g + 1) * HEAD_DIM, c * t:(c + 1) * t]

    gl = _dot_nt(wgt_ref[...], hb) + bg_ref[...]
    gt_ref[0] = 1.0 / (1.0 + jnp.exp(-gl))


def _nsa_proj(x, sc, sh, w_in, b_gate):
    B, S, _ = x.shape
    tm = ROW_TILE
    t = NSA_T
    scale = LOG2E * HEAD_DIM ** -0.5
    cuts = [D_MODEL + n * KV_WIDTH for n in range(7)]
    wq, wkc, wvc, wks, wvs, wkw, wvw, wg = jnp.split(w_in, cuts, axis=1)
    wqt = (wq * scale).T.astype(BF16)
    padk = lambda w: jnp.pad(w.reshape(D_MODEL, N_GROUPS, HEAD_DIM),
                             ((0, 0), (0, 0), (0, LANES - HEAD_DIM))).reshape(D_MODEL, N_GROUPS * LANES)
    wnat = jnp.concatenate([wkc, wvc, padk(wks), padk(wkw)], axis=1).astype(BF16)
    wvt = jnp.concatenate([wvs, wvw], axis=1).T.astype(BF16)
    n_gate = 3 * N_HEADS
    wgt = wg.T.astype(BF16)
    bg = jnp.broadcast_to(b_gate.reshape(n_gate, 1), (n_gate, tm))

    full = lambda a: pl.BlockSpec(a.shape, lambda b, i: (0,) * a.ndim)
    nat_spec = pl.BlockSpec((1, N_GROUPS, tm, HEAD_DIM), lambda b, i: (b, 0, i, 0))
    pad_spec = pl.BlockSpec((1, N_GROUPS, tm, LANES), lambda b, i: (b, 0, i, 0))
    vt_spec = lambda n, rows: pl.BlockSpec((1, N_GROUPS, tm // n, rows, n), lambda b, i: (b, 0, i, 0, 0))
    return pl.pallas_call(
        _nsa_proj_kernel, name="nsa_proj",
        grid=(B, S // tm),
        in_specs=[
            pl.BlockSpec((1, tm, D_MODEL), lambda b, i: (b, i, 0)),
            pl.BlockSpec((1, 1, D_MODEL), lambda b, i: (b, 0, 0)),
            pl.BlockSpec((1, 1, D_MODEL), lambda b, i: (b, 0, 0)),
            full(wqt), full(wnat), full(wvt), full(wgt), full(bg),
        ],
        out_specs=[
            pl.BlockSpec((1, N_GROUPS, tm // t, HEAD_DIM, HEADS_PER_GROUP * t), lambda b, i: (b, 0, i, 0, 0)),
            nat_spec, nat_spec, pad_spec, pad_spec, vt_spec(KEY_CHUNK, V_ROWS), vt_spec(t, HEAD_DIM),
            pl.BlockSpec((1, n_gate, tm), lambda b, i: (b, 0, i)),
        ],
        out_shape=[
            jax.ShapeDtypeStruct((B, N_GROUPS, S // t, HEAD_DIM, HEADS_PER_GROUP * t), BF16),
            jax.ShapeDtypeStruct((B, N_GROUPS, S, HEAD_DIM), F32),
            jax.ShapeDtypeStruct((B, N_GROUPS, S, HEAD_DIM), F32),
            jax.ShapeDtypeStruct((B, N_GROUPS, S, LANES), BF16),
            jax.ShapeDtypeStruct((B, N_GROUPS, S, LANES), BF16),
            jax.ShapeDtypeStruct((B, N_GROUPS, S // KEY_CHUNK, V_ROWS, KEY_CHUNK), BF16),
            jax.ShapeDtypeStruct((B, N_GROUPS, S // t, HEAD_DIM, t), BF16),
            jax.ShapeDtypeStruct((B, n_gate, S), F32),
        ],
        compiler_params=_params("arbitrary", "arbitrary"),
    )(x, sc, sh, wqt, wnat, wvt, wgt, bg)


def _compress_kernel(tk_ref, tv_ref, pos_ref, w1_ref, w2k_ref, w2vt_ref, kc_ref, vct_ref):
    n = tk_ref.shape[2]
    half = CMP_STRIDE * HEAD_DIM

    def hidden(t_ref, idx):
        t16 = t_ref[0, 0]
        xa = (t16 + pos_ref[idx, 0:1, :]).astype(BF16)
        xb = (t16 + pos_ref[idx, 1:2, :]).astype(BF16)
        first = _dot(xa, w1_ref[idx, :half, :])
        second = _dot(xb, w1_ref[idx, half:, :])
        pre = first + pltpu.roll(second, n - 1, 0)
        return (pre / (1.0 + jnp.exp(-pre))).astype(BF16)

    kc = _dot(hidden(tk_ref, 0), w2k_ref[...])
    lane = lax.broadcasted_iota(jnp.int32, kc.shape, 1)
    ones = ((lane == HEAD_DIM) | (lane == HEAD_DIM + 1)).astype(F32)
    kc_ref[0, 0] = (kc + ones).astype(BF16)
    vct = _dot_nt(w2vt_ref[...], hidden(tv_ref, 1)).astype(BF16)
    for r in range(n // CMP_ROWS):
        vct_ref[0, 0, r] = vct[:, r * CMP_ROWS:(r + 1) * CMP_ROWS]


def _compress(kc, vc, cmp_pos, cmp_w1, cmp_w2):
    B, G, S, _ = kc.shape
    n = S // CMP_STRIDE
    width = CMP_STRIDE * HEAD_DIM
    tk = kc.reshape(B, G, n, width)
    tv = vc.reshape(B, G, n, width)
    pos = cmp_pos.reshape(2, 2, width)
    w1 = cmp_w1.astype(BF16)
    w2k = jnp.pad(cmp_w2[0], ((0, 0), (0, LANES - HEAD_DIM))).astype(BF16)
    w2vt = cmp_w2[1].T.astype(BF16)
    full = lambda a: pl.BlockSpec(a.shape, lambda b, g: (0,) * a.ndim)
    t_spec = pl.BlockSpec((1, 1, n, width), lambda b, g: (b, g, 0, 0))
    return pl.pallas_call(
        _compress_kernel, name="nsa_compress",
        grid=(B, G),
        in_specs=[t_spec, t_spec, full(pos), full(w1), full(w2k), full(w2vt)],
        out_specs=[
            pl.BlockSpec((1, 1, n, LANES), lambda b, g: (b, g, 0, 0)),
            pl.BlockSpec((1, 1, n // CMP_ROWS, HEAD_DIM, CMP_ROWS), lambda b, g: (b, g, 0, 0, 0)),
        ],
        out_shape=[
            jax.ShapeDtypeStruct((B, G, n, LANES), BF16),
            jax.ShapeDtypeStruct((B, G, n // CMP_ROWS, HEAD_DIM, CMP_ROWS), BF16),
        ],
        compiler_params=_params("arbitrary", "arbitrary"),
    )(tk, tv, pos, w1, w2k, w2vt)


def _bias_tiles_kernel(rb_ref, sel_ref, win_ref):
    hd = pl.program_id(0)
    far = rb_ref[FAR_BUCKET, hd]

    def dist(n, back):
        key = lax.broadcasted_iota(jnp.int32, (n, n), 0)
        qry = lax.broadcasted_iota(jnp.int32, (n, n), 1)
        return qry - key + back * n

    kc = sel_ref.shape[2]
    d_prev, d_diag = dist(kc, 1), dist(kc, 0)
    sel_ref[0, 0] = LOG2E * (_bucket_bias(d_prev, rb_ref, hd) - far)
    sel_ref[1, 0] = jnp.where(d_diag >= 0, LOG2E * (_bucket_bias(d_diag, rb_ref, hd) - far), NEG)
    t = win_ref.shape[2]
    n_back = win_ref.shape[0] - 1
    for r in range(n_back + 1):
        d = dist(t, n_back - r)
        win_ref[r, 0] = jnp.where((d >= 0) & (d < WINDOW), LOG2E * _bucket_bias(d, rb_ref, hd), NEG)


def _bias_tiles(rel_bias):
    t = NSA_T
    n_win = WINDOW // t + 1
    assert WINDOW % t == 0
    spec = lambda n, m: pl.BlockSpec((n, 1, m, m), lambda h: (0, h, 0, 0))
    return pl.pallas_call(
        _bias_tiles_kernel, name="rel_bias_tiles",
        grid=(N_HEADS,),
        in_specs=[pl.BlockSpec(memory_space=pltpu.SMEM)],
        out_specs=[spec(2, KEY_CHUNK), spec(n_win, t)],
        out_shape=[
            jax.ShapeDtypeStruct((2, N_HEADS, KEY_CHUNK, KEY_CHUNK), F32),
            jax.ShapeDtypeStruct((n_win, N_HEADS, t, t), F32),
        ],
        compiler_params=_params("arbitrary"),
    )(rel_bias)


def _far_bias_rows(rb_ref, g, width):
    t = width // HEADS_PER_GROUP
    lane = lax.broadcasted_iota(jnp.int32, (HEAD_DIM, width), 1)
    row = lax.broadcasted_iota(jnp.int32, (HEAD_DIM, width), 0)
    far = jnp.zeros((HEAD_DIM, width), F32)
    for hh in range(HEADS_PER_GROUP):
        far = jnp.where(lane >= hh * t, LOG2E * rb_ref[FAR_BUCKET, g * HEADS_PER_GROUP + hh], far)
    hi = far.astype(BF16).astype(F32)
    return jnp.where(row == 0, hi, jnp.where(row == 1, far - hi, 0.0)).astype(BF16)


def _nsa_cmp_kernel(rb_ref, qt_ref, kc_ref, vct_ref, at_ref, oc_ref, sb_ref, qp_ref, s_ref, imp_ref):
    g = pl.program_id(1)
    i = pl.program_id(2)
    width = qt_ref.shape[4]
    t = width // HEADS_PER_GROUP
    n = kc_ref.shape[2]
    nb = at_ref.shape[1]
    t0 = i * t
    assert t & (t - 1) == 0 and n % CMP_ROWS == 0

    n_vis = (t0 + t - CMP_LEN) // CMP_STRIDE // CMP_ROWS + 1
    n_full = jnp.maximum((t0 - (CMP_LEN - 1)) // CMP_STRIDE + 1, 0) // CMP_ROWS
    rows_of = lambda r: pl.ds(pl.multiple_of(r * CMP_ROWS, CMP_ROWS), CMP_ROWS)

    def valid(r):
        blk = r * CMP_ROWS + lax.broadcasted_iota(jnp.int32, (CMP_ROWS, width), 0)
        qry = t0 + (lax.broadcasted_iota(jnp.int32, (CMP_ROWS, width), 1) & (t - 1))
        return qry >= blk * CMP_STRIDE + CMP_LEN - 1

    qp_ref[0:HEAD_DIM, :] = qt_ref[0, 0, 0]
    qp_ref[HEAD_DIM:LANES, :] = _far_bias_rows(rb_ref, g, width)

    def score_chunk(r, carry):
        s_ref[rows_of(r), :] = _dot(kc_ref[0, 0, rows_of(r), :], qp_ref[...])
        return carry

    lax.fori_loop(0, n_vis, score_chunk, 0)

    band = t // CMP_STRIDE + 16
    assert (FAR_DIST + CMP_LEN - 1) <= 16 * CMP_STRIDE and band <= n
    r0 = pl.multiple_of(jnp.clip(t0 // CMP_STRIDE - 16, 0, n - band), 8)
    blk_r = r0 + lax.broadcasted_iota(jnp.int32, (band, t), 0)
    qry_r = t0 + lax.broadcasted_iota(jnp.int32, (band, t), 1)
    dist_r = qry_r - (blk_r * CMP_STRIDE + CMP_LEN - 1)
    for hh in range(HEADS_PER_GROUP):
        hd = g * HEADS_PER_GROUP + hh
        corr = LOG2E * (_bucket_bias(dist_r, rb_ref, hd) - rb_ref[FAR_BUCKET, hd])
        s_ref[pl.ds(r0, band), hh * t:(hh + 1) * t] += corr

    def col_max(masked):
        def body(r, m):
            s = s_ref[rows_of(r), :]
            if masked:
                s = jnp.where(valid(r), s, NEG)
            return jnp.maximum(m, jnp.max(s, axis=0, keepdims=True))
        return body

    m = lax.fori_loop(0, n_full, col_max(False), jnp.full((1, width), NEG, F32))
    m = lax.fori_loop(n_full, n_vis, col_max(True), m)

    def exp_sum(masked):
        def body(r, l):
            p = jnp.exp2(s_ref[rows_of(r), :] - m)
            if masked:
                p = jnp.where(valid(r), p, 0.0)
            s_ref[rows_of(r), :] = p
            return l + jnp.sum(p, axis=0, keepdims=True)
        return body

    l = lax.fori_loop(0, n_full, exp_sum(False), jnp.zeros((1, width), F32))
    l = lax.fori_loop(n_full, n_vis, exp_sum(True), l)
    inv = 1.0 / jnp.maximum(l, TINY)

    oc_ref[0, 0, 0] = jnp.zeros((HEAD_DIM, width), F32)
    imp_ref[...] = jnp.zeros_like(imp_ref)

    def finish(r, carry):
        p = s_ref[rows_of(r), :] * inv
        oc_ref[0, 0, 0] += _dot(vct_ref[0, 0, r], p.astype(BF16))
        imp = p[:, 0:t]
        for hh in range(1, HEADS_PER_GROUP):
            imp = imp + p[:, hh * t:(hh + 1) * t]
        hi, mid, lo = _split3(imp)
        a = at_ref[r]
        imp_ref[...] += _dot(a, hi) + _dot(a, mid) + _dot(a, lo)
        return carry

    lax.fori_loop(0, n_vis, finish, 0)

    sblk = lax.broadcasted_iota(jnp.int32, (nb, t), 0)
    cur = (t0 + lax.broadcasted_iota(jnp.int32, (nb, t), 1)) // SEL_BLOCK
    forced = (sblk == 0) | (sblk == cur) | (sblk == cur - 1)
    is_cand = (sblk >= 1) & (sblk <= cur - 2)
    cand = jnp.where(is_cand, imp_ref[...], -1.0)
    sblk_f = sblk.astype(F32)
    for _ in range(SEL_TOPK - N_FORCED):
        best = jnp.max(cand, axis=0, keepdims=True)
        first = jnp.min(jnp.where(cand == best, sblk_f, float(nb)), axis=0, keepdims=True)
        cand = jnp.where(sblk_f == first, -1.0, cand)
    chosen = forced | (is_cand & (cand < 0.0))
    sb_ref[0, 0, 0] = jnp.where(chosen, 0.0, NEG).astype(BF16)


def _nsa_cmp(rel_bias, qt, kcmp, vcmp_t, n_sel):
    B, G, nq, _, width = qt.shape
    t = width // HEADS_PER_GROUP
    n = kcmp.shape[2]
    n_cmp = n - 1
    nb = -(-n_sel // LANES) * LANES
    R = SEL_BLOCK // CMP_STRIDE
    at = np.zeros((nb, n), np.float32)
    for j in range(n_sel):
        lo, hi = max(R * j - 1, 0), min(R * j + R - 1, n_cmp - 1)
        at[j, lo:hi + 1] = 1.0
    at = jnp.asarray(at.reshape(nb, n // CMP_ROWS, CMP_ROWS).transpose(1, 0, 2), BF16)
    return pl.pallas_call(
        _nsa_cmp_kernel, name="nsa_cmp_topk",
        grid=(B, G, nq),
        in_specs=[
            pl.BlockSpec(memory_space=pltpu.SMEM),
            pl.BlockSpec((1, 1, 1, HEAD_DIM, width), lambda b, g, i: (b, g, i, 0, 0)),
            pl.BlockSpec((1, 1, n, LANES), lambda b, g, i: (b, g, 0, 0)),
            pl.BlockSpec((1, 1, n // CMP_ROWS, HEAD_DIM, CMP_ROWS), lambda b, g, i: (b, g, 0, 0, 0)),
            pl.BlockSpec((n // CMP_ROWS, nb, CMP_ROWS), lambda b, g, i: (0, 0, 0)),
        ],
        out_specs=[
            pl.BlockSpec((1, 1, 1, HEAD_DIM, width), lambda b, g, i: (b, g, i, 0, 0)),
            pl.BlockSpec((1, 1, 1, nb, t), lambda b, g, i: (b, g, i, 0, 0)),
        ],
        out_shape=[
            jax.ShapeDtypeStruct((B, G, nq, HEAD_DIM, width), F32),
            jax.ShapeDtypeStruct((B, G, nq, nb, t), BF16),
        ],
        scratch_shapes=[pltpu.VMEM((LANES, width), BF16), pltpu.VMEM((n, width), F32),
                        pltpu.VMEM((nb, t), F32)],
        compiler_params=_params("arbitrary", "arbitrary", "arbitrary"),
    )(rel_bias, qt, kcmp, vcmp_t, at)


def _nsa_win_kernel(qt_ref, *refs):
    n_win = (len(refs) - 2) // 2
    k_refs, v_refs, (wb_ref, ow_ref) = refs[:n_win], refs[n_win:2 * n_win], refs[2 * n_win:]
    i = pl.program_id(2)
    width = qt_ref.shape[4]
    t = width // HEADS_PER_GROUP
    qt = qt_ref[0, 0, 0]
    qp = jnp.concatenate([qt, jnp.zeros_like(qt)], axis=0)
    scores = []
    for c in range(n_win):
        s = _dot(k_refs[c][0, 0], qp)
        missing = jnp.where(i - (n_win - 1 - c) < 0, NEG, 0.0)
        s = jnp.concatenate([s[:, hh * t:(hh + 1) * t] + (wb_ref[c, hh] + missing)
                             for hh in range(HEADS_PER_GROUP)], axis=1)
        scores.append(s)
    m = jnp.max(scores[0], axis=0, keepdims=True)
    for s in scores[1:]:
        m = jnp.maximum(m, jnp.max(s, axis=0, keepdims=True))
    l = jnp.zeros_like(m)
    acc = jnp.zeros((HEAD_DIM, width), F32)
    for c in range(n_win):
        p = jnp.exp2(scores[c] - m)
        l = l + jnp.sum(p, axis=0, keepdims=True)
        acc = acc + _dot(v_refs[c][0, 0, 0], p.astype(BF16))
    ow_ref[0, 0, 0] = acc / l


def _nsa_window(qt, kw, vwt, win_bias):
    B, G, nq, _, width = qt.shape
    t = width // HEADS_PER_GROUP
    n_win = win_bias.shape[0]
    backs = list(range(n_win - 1, -1, -1))
    k_spec = lambda back: pl.BlockSpec((1, 1, t, LANES), lambda b, g, i: (b, g, jnp.maximum(i - back, 0), 0))
    v_spec = lambda back: pl.BlockSpec((1, 1, 1, HEAD_DIM, t),
                                       lambda b, g, i: (b, g, jnp.maximum(i - back, 0), 0, 0))
    return pl.pallas_call(
        _nsa_win_kernel, name="nsa_window",
        grid=(B, G, nq),
        in_specs=[pl.BlockSpec((1, 1, 1, HEAD_DIM, width), lambda b, g, i: (b, g, i, 0, 0))]
        + [k_spec(back) for back in backs] + [v_spec(back) for back in backs]
        + [pl.BlockSpec((n_win, HEADS_PER_GROUP, t, t), lambda b, g, i: (0, g, 0, 0))],
        out_specs=pl.BlockSpec((1, 1, 1, HEAD_DIM, width), lambda b, g, i: (b, g, i, 0, 0)),
        out_shape=jax.ShapeDtypeStruct((B, G, nq, HEAD_DIM, width), F32),
        compiler_params=_params("arbitrary", "arbitrary", "arbitrary"),
    )(qt, *([kw] * n_win), *([vwt] * n_win), win_bias)


def _nsa_sel_kernel(rb_ref, qt_ref, sb_ref, ks_ref, e_ref, vst_ref, cb_ref, oc_ref, ow_ref, gt_ref,
                    o_ref, qp_ref, s_buf, p_buf, acc_ref):
    g = pl.program_id(1)
    i = pl.program_id(2)
    width = qt_ref.shape[4]
    t = width // HEADS_PER_GROUP
    per_tile = t // KEY_CHUNK
    chunks_per_slab = LANES * SEL_BLOCK // KEY_CHUNK
    blocks_per_head = t // QUERY_SUB
    assert per_tile == 2 and t % QUERY_SUB == 0

    far_rows = _far_bias_rows(rb_ref, g, width)
    for slab in range(qp_ref.shape[0]):
        qp_ref[slab, 0:HEAD_DIM, :] = qt_ref[0, 0, 0]
        qp_ref[slab, HEAD_DIM:LANES, :] = far_rows
        sb = sb_ref[0, 0, 0, slab * LANES:(slab + 1) * LANES, :]
        qp_ref[slab, LANES:2 * LANES, :] = jnp.concatenate([sb] * HEADS_PER_GROUP, axis=1)

    def scores(c):
        c = jnp.maximum(c, 0)
        rows = pl.ds(pl.multiple_of(c * KEY_CHUNK, KEY_CHUNK), KEY_CHUNK)
        kp = jnp.concatenate([ks_ref[0, 0, rows, :], e_ref[rows, :]], axis=1)
        return _dot(kp, qp_ref[c // chunks_per_slab])

    def values(c):
        return vst_ref[0, 0, jnp.maximum(c, 0)]

    missing = jnp.where(i >= 1, 0.0, NEG)

    def tail_bias(r):
        def bias(c):
            hh, part = divmod(c, blocks_per_head)
            ahead = part * QUERY_SUB // KEY_CHUNK - (r - per_tile)
            if ahead == 0:
                tile = cb_ref[1, hh]
            elif ahead == 1:
                tile = cb_ref[0, hh]
            elif ahead < 0:
                tile = jnp.full((KEY_CHUNK, QUERY_SUB), NEG, F32)
            else:
                tile = None
            if r < per_tile:
                tile = missing if tile is None else tile + missing
            return tile
        return bias

    first = per_tile * jnp.minimum(i - 1, 0)
    o_sel = _flash_sweep(first, jnp.maximum(i - 1, 0), scores, values,
                         [tail_bias(r) for r in range(2 * per_tile)], s_buf, p_buf, acc_ref)

    out = gt_ref[0, 0, 0, 0:1, :] * oc_ref[0, 0, 0] + gt_ref[0, 0, 0, 1:2, :] * o_sel \
        + gt_ref[0, 0, 0, 2:3, :] * ow_ref[0, 0, 0]
    for hh in range(HEADS_PER_GROUP):
        o_ref[0, hh * HEAD_DIM:(hh + 1) * HEAD_DIM, :] = out[:, hh * t:(hh + 1) * t].astype(o_ref.dtype)


def _nsa_select_combine(rel_bias, qt, sel_bias, ks, vst, sel_corr, oc, ow, gates):
    B, G, nq, _, width = qt.shape
    t = width // HEADS_PER_GROUP
    S = nq * t
    nb = sel_bias.shape[3]
    blocks = (np.arange(S) // SEL_BLOCK) % LANES
    onehot = jnp.asarray(blocks[:, None] == np.arange(LANES)[None, :], BF16)
    tile = pl.BlockSpec((1, 1, 1, HEAD_DIM, width), lambda b, g, i: (b, g, i, 0, 0))
    return pl.pallas_call(
        _nsa_sel_kernel, name="nsa_select",
        grid=(B, G, nq),
        in_specs=[
            pl.BlockSpec(memory_space=pltpu.SMEM),
            tile,
            pl.BlockSpec((1, 1, 1, nb, t), lambda b, g, i: (b, g, i, 0, 0)),
            pl.BlockSpec((1, 1, S, LANES), lambda b, g, i: (b, g, 0, 0)),
            pl.BlockSpec((S, LANES), lambda b, g, i: (0, 0)),
            pl.BlockSpec((1, 1, S // KEY_CHUNK, V_ROWS, KEY_CHUNK), lambda b, g, i: (b, g, 0, 0, 0)),
            pl.BlockSpec((2, HEADS_PER_GROUP, KEY_CHUNK, KEY_CHUNK), lambda b, g, i: (0, g, 0, 0)),
            tile, tile,
            pl.BlockSpec((1, 1, 1, 8, width), lambda b, g, i: (b, g, i, 0, 0)),
        ],
        out_specs=pl.BlockSpec((1, HEADS_PER_GROUP * HEAD_DIM, t), lambda b, g, i: (b, g, i)),
        out_shape=jax.ShapeDtypeStruct((B, D_MODEL, S), BF16),
        scratch_shapes=[pltpu.VMEM((nb // LANES, 2 * LANES, width), BF16),
                        pltpu.VMEM((2, KEY_CHUNK, width), F32), pltpu.VMEM((2, KEY_CHUNK, width), BF16),
                        pltpu.VMEM((V_ROWS, width), F32)],
        compiler_params=_params("arbitrary", "arbitrary", "arbitrary"),
    )(rel_bias, qt, sel_bias, ks, onehot, vst, sel_corr, oc, ow, gates)


def _nsa_attention_t(x, sc, sh, w_in, b_gate, cmp_pos, cmp_w1, cmp_w2, rel_bias):
    B, S, _ = x.shape
    t = NSA_T
    n_sel = S // SEL_BLOCK
    assert S % ROW_TILE == 0 and n_sel >= SEL_TOPK and S // CMP_STRIDE >= t // CMP_STRIDE + 16
    qt, kc, vc, ks, kw, vst, vwt, gt = _nsa_proj(x, sc, sh, w_in, b_gate)
    kcmp, vcmp_t = _compress(kc, vc, cmp_pos, cmp_w1, cmp_w2)
    sel_corr, win_bias = _bias_tiles(rel_bias)
    oc, sel_bias = _nsa_cmp(rel_bias, qt, kcmp, vcmp_t, n_sel)
    ow = _nsa_window(qt, kw, vwt, win_bias)
    gates = gt.reshape(B, 3, N_GROUPS, HEADS_PER_GROUP, S // t, t).transpose(0, 2, 4, 1, 3, 5)
    gates = jnp.pad(gates.reshape(B, N_GROUPS, S // t, 3, HEADS_PER_GROUP * t),
                    ((0, 0), (0, 0), (0, 0), (0, 5), (0, 0)))
    return _nsa_select_combine(rel_bias, qt, sel_bias, ks, vst, sel_corr, oc, ow, gates)


def kernel(x, c, ada_w, ada_b, ln_g, ln_b, fox_w_in, fox_b_f, fox_w_out, nsa_w_in, nsa_b_gate,
           nsa_cmp_pos, nsa_cmp_w1, nsa_cmp_w2, nsa_w_out, rel_bias, ffn_w_in, ffn_w_out):
    B, S, _ = x.shape
    assert S % FOX_T == 0 and S % ROW_TILE == 0 and ROW_TILE == NSA_T
    mod = _modulation(c, ada_w, ada_b)
    for layer in range(DEPTH):
        sh_a, sc_a, g_a, sh_f, sc_f, g_f = [m.reshape(B, 1, D_MODEL) for m in jnp.split(mod[layer], 6, axis=-1)]
        j = layer // 2
        if layer % 2 == 0:
            kp, qt, vt = _fox_proj(x, sc_a, sh_a, fox_w_in[j], fox_b_f[j])
            attn_t = _fox_attention(qt, kp, vt)
            w_out = fox_w_out[j]
        else:
            attn_t = _nsa_attention_t(x, sc_a, sh_a, nsa_w_in[j], nsa_b_gate[j], nsa_cmp_pos[j],
                                      nsa_cmp_w1[j], nsa_cmp_w2[j], rel_bias)
            w_out = nsa_w_out[j]
        x = _outproj_ln(attn_t, w_out, x, g_a, ln_g[layer, 0], ln_b[layer, 0])
        x = _ffn_ln(x, sc_f, sh_f, g_f, ffn_w_in[layer], ffn_w_out[layer], ln_g[layer, 1], ln_b[layer, 1])
    return x
```

```python
import functools
import math

import numpy as np
import jax
import jax.numpy as jnp
from jax import lax
from jax.experimental import pallas as pl
from jax.experimental.pallas import tpu as pltpu

F32 = jnp.float32
BF16 = jnp.bfloat16
HIGHEST = lax.Precision.HIGHEST

D_MODEL = 1024
HEAD_DIM = 64
N_HEADS = 16
N_GROUPS = 4
HEADS_PER_GROUP = 4
KV_WIDTH = N_GROUPS * HEAD_DIM
CMP_LEN = 32
CMP_STRIDE = 16
CMP_HIDDEN = 128
SEL_BLOCK = 64
SEL_TOPK = 16
WINDOW = 512
REL_BUCKETS = 32
FF_HIDDEN = 2816
DEPTH = 2
DN_ALPHA = (2 * DEPTH) ** 0.25
LN_EPS = 1e-5
NEG = -1e30
TINY = 1e-30

LANES = 128
VMEM_LIMIT = 56 * 1024 * 1024

ROW_TILE = 512
FOX_T = 1024
NSA_T = 512
FFN_CHUNK = 1408
N_FORCED = 3

BUCKET_START = (0, 1, 2, 3, 4, 5, 6, 7, 8, 9, 10, 11, 12, 13, 14, 15,
                16, 19, 21, 24, 27, 31, 35, 40, 46, 52, 59, 67, 77, 87, 99, 113)
FAR_BUCKET = REL_BUCKETS - 1
FAR_DIST = BUCKET_START[FAR_BUCKET]

LOG2E = math.log2(math.e)
QUERY_SUB = 256
KEY_CHUNK = 256
CMP_ROWS = 128
V_ROWS = HEAD_DIM + 16
NT_DIMS = (((1,), (1,)), ((), ()))


def _params(*sem):
    return pltpu.CompilerParams(dimension_semantics=sem, vmem_limit_bytes=VMEM_LIMIT)


def _dot(a, b, **kw):
    return jnp.dot(a, b, preferred_element_type=F32, **kw)


def _dot_nt(a, b):
    return lax.dot_general(a, b, NT_DIMS, preferred_element_type=F32)


def _split3(v):
    hi = v.astype(BF16)
    r = v - hi.astype(F32)
    mid = r.astype(BF16)
    lo = (r - mid.astype(F32)).astype(BF16)
    return hi, mid, lo


def _layer_norm(z, g, b):
    mu = jnp.mean(z, axis=-1, keepdims=True)
    zc = z - mu
    var = jnp.mean(zc * zc, axis=-1, keepdims=True)
    return zc * lax.rsqrt(var + LN_EPS) * g + b


def _bucket_bias(dist, rb_ref, head):
    bias = jnp.full(dist.shape, rb_ref[0, head], F32)
    for k in range(1, REL_BUCKETS):
        bias = jnp.where(dist >= BUCKET_START[k], rb_ref[k, head], bias)
    return bias


def _mod_kernel(c_ref, w_ref, b_ref, o_ref):
    c = c_ref[...]
    cs = c / (1.0 + jnp.exp(-c))
    o_ref[0] = _dot(cs, w_ref[0], precision=HIGHEST) + b_ref[0]


def _modulation(c, ada_w, ada_b):
    B = c.shape[0]
    depth, _, n = ada_w.shape
    rows = 8
    c_pad = jnp.pad(c, ((0, rows - B), (0, 0)))
    tn = 1536
    out = pl.pallas_call(
        _mod_kernel, name="adaln_mod",
        grid=(depth, n // tn),
        in_specs=[
            pl.BlockSpec((rows, D_MODEL), lambda l, j: (0, 0)),
            pl.BlockSpec((1, D_MODEL, tn), lambda l, j: (l, 0, j)),
            pl.BlockSpec((1, 1, tn), lambda l, j: (l, 0, j)),
        ],
        out_specs=pl.BlockSpec((1, rows, tn), lambda l, j: (l, 0, j)),
        out_shape=jax.ShapeDtypeStruct((depth, rows, n), F32),
        compiler_params=_params("arbitrary", "arbitrary"),
    )(c_pad, ada_w, ada_b.reshape(depth, 1, n))
    return out[:, :B]


def _fox_proj_kernel(x_ref, sc_ref, sh_ref, wk_ref, wqt_ref, wvt_ref, wfh_ref, wfl_ref,
                     bf_ref, place_ref, kp_ref, qt_ref, vt_ref, carry_ref):
    tm = x_ref.shape[1]

    @pl.when(pl.program_id(1) == 0)
    def _():
        carry_ref[...] = jnp.zeros_like(carry_ref)

    h = x_ref[0] * (1.0 + sc_ref[0]) + sh_ref[0]
    hb = h.astype(BF16)
    hl = (h - hb.astype(F32)).astype(BF16)

    f = _dot(hb, wfh_ref[...]) + _dot(hl, wfh_ref[...]) + _dot(hb, wfl_ref[...])
    z = f + bf_ref[...]
    logf = jnp.minimum(z, 0.0) - jnp.log(1.0 + jnp.exp(-jnp.abs(z)))
    row = lax.broadcasted_iota(jnp.int32, (tm, tm), 0)
    col = lax.broadcasted_iota(jnp.int32, (tm, tm), 1)
    lower = (col <= row).astype(F32)
    cum = _dot(lower, logf, precision=HIGHEST) + carry_ref[0:1, :]
    carry_ref[...] = jnp.broadcast_to(cum[tm - 1:tm, :], carry_ref.shape)

    hi, mid, lo = [p.astype(F32) for p in _split3(-LOG2E * cum)]
    lane = lax.broadcasted_iota(jnp.int32, (tm, LANES), 1)
    pieces = jnp.where(lane < N_HEADS, hi, jnp.where(lane < 2 * N_HEADS, mid, lo)).astype(BF16)
    kp = _dot(hb, wk_ref[...]) + _dot(pieces, place_ref[...])
    qt = _dot_nt(wqt_ref[...], hb)
    qrow = lax.broadcasted_iota(jnp.int32, qt.shape, 0) % LANES
    qt = jnp.where((qrow >= HEAD_DIM) & (qrow < HEAD_DIM + 3), 1.0, qt)
    vt = _dot_nt(wvt_ref[...], hb)
    for hd in range(N_HEADS):
        kp_ref[0, hd] = kp[:, hd * LANES:(hd + 1) * LANES].astype(BF16)
        qt_ref[0, hd] = qt[hd * LANES:(hd + 1) * LANES, :].astype(BF16)
        for c in range(tm // KEY_CHUNK):
            vt_ref[0, hd, c, 0:HEAD_DIM, :] = vt[hd * HEAD_DIM:(hd + 1) * HEAD_DIM,
                                                 c * KEY_CHUNK:(c + 1) * KEY_CHUNK].astype(BF16)
            vt_ref[0, hd, c, HEAD_DIM:V_ROWS, :] = _ones_row_tail(KEY_CHUNK)


def _fox_proj(x, sc, sh, w_in, b_f):
    B, S, _ = x.shape
    tm = ROW_TILE
    scale = LOG2E * HEAD_DIM ** -0.5
    wq = (w_in[:, :D_MODEL] * scale).reshape(D_MODEL, N_HEADS, HEAD_DIM)
    wk = w_in[:, D_MODEL:2 * D_MODEL].reshape(D_MODEL, N_HEADS, HEAD_DIM)
    wv = w_in[:, 2 * D_MODEL:3 * D_MODEL]
    wf = w_in[:, 3 * D_MODEL:]
    pad = ((0, 0), (0, 0), (0, LANES - HEAD_DIM))
    wk_p = jnp.pad(wk, pad).reshape(D_MODEL, N_HEADS * LANES).astype(BF16)
    wqt_p = jnp.pad(wq, pad).reshape(D_MODEL, N_HEADS * LANES).T.astype(BF16)
    wvt = wv.T.astype(BF16)
    wf_rep = jnp.pad(jnp.tile(wf, (1, 3)), ((0, 0), (0, LANES - 3 * N_HEADS)))
    wf_hi = wf_rep.astype(BF16)
    wf_lo = (wf_rep - wf_hi.astype(F32)).astype(BF16)
    bf_rep = jnp.pad(jnp.tile(b_f, 3), (0, LANES - 3 * N_HEADS)).reshape(1, LANES)
    place = np.zeros((LANES, N_HEADS * LANES), np.float32)
    for r in range(3):
        for hd in range(N_HEADS):
            place[r * N_HEADS + hd, hd * LANES + HEAD_DIM + r] = 1.0
    place = jnp.asarray(place, BF16)

    full = lambda a: pl.BlockSpec(a.shape, lambda b, i: (0,) * a.ndim)
    return pl.pallas_call(
        _fox_proj_kernel, name="fox_proj",
        grid=(B, S // tm),
        in_specs=[
            pl.BlockSpec((1, tm, D_MODEL), lambda b, i: (b, i, 0)),
            pl.BlockSpec((1, 1, D_MODEL), lambda b, i: (b, 0, 0)),
            pl.BlockSpec((1, 1, D_MODEL), lambda b, i: (b, 0, 0)),
            full(wk_p), full(wqt_p), full(wvt), full(wf_hi), full(wf_lo), full(bf_rep), full(place),
        ],
        out_specs=[
            pl.BlockSpec((1, N_HEADS, tm, LANES), lambda b, i: (b, 0, i, 0)),
            pl.BlockSpec((1, N_HEADS, LANES, tm), lambda b, i: (b, 0, 0, i)),
            pl.BlockSpec((1, N_HEADS, tm // KEY_CHUNK, V_ROWS, KEY_CHUNK), lambda b, i: (b, 0, i, 0, 0)),
        ],
        out_shape=[
            jax.ShapeDtypeStruct((B, N_HEADS, S, LANES), BF16),
            jax.ShapeDtypeStruct((B, N_HEADS, LANES, S), BF16),
            jax.ShapeDtypeStruct((B, N_HEADS, S // KEY_CHUNK, V_ROWS, KEY_CHUNK), BF16),
        ],
        scratch_shapes=[pltpu.VMEM((8, LANES), F32)],
        compiler_params=_params("arbitrary", "arbitrary"),
    )(x, sc, sh, wk_p, wqt_p, wvt, wf_hi, wf_lo, bf_rep, place)


def _ones_row_tail(width):
    row = lax.broadcasted_iota(jnp.int32, (V_ROWS - HEAD_DIM, width), 0)
    return (row == 0).astype(F32).astype(BF16)


def _flash_sweep(first, n_far_pairs, scores, values, tail, s_buf, p_buf, acc_ref, far_pairs_even=False):
    width = acc_ref.shape[1]
    n_blocks = width // QUERY_SUB
    assert len(tail) % 2 == 0 and width % QUERY_SUB == 0
    assert tail[0][0] is None and tail[1][0] is None
    cat = lambda parts: jnp.concatenate(parts, axis=1)

    def fetch(c, par, bias, may_be_missing):
        s = scores(c)
        missing = jnp.where(c < 0, NEG, 0.0) if may_be_missing else None
        col_max = []
        for blk in range(n_blocks):
            cols = slice(blk * QUERY_SUB, (blk + 1) * QUERY_SUB)
            part = s[:, cols]
            extra = None if bias is None else bias(blk)
            if extra is not None:
                part = part + extra
            if missing is not None:
                part = part + missing
            s_buf[par, :, cols] = part
            col_max.append(jnp.max(part, axis=0, keepdims=True))
        return cat(col_max)

    def step(j, par, stats, nxt=None):
        m, corr1, corr2, cmax = stats
        acc_ref[...] = acc_ref[...] * corr2 + _dot(values(j - 2), p_buf[par])
        m_new = jnp.maximum(m, cmax[par])
        for blk in range(n_blocks):
            cols = slice(blk * QUERY_SUB, (blk + 1) * QUERY_SUB)
            p_buf[par, :, cols] = jnp.exp2((s_buf[par, :, cols] - m_new[:, cols]).astype(BF16))
        if nxt is not None:
            cmax = tuple(fetch(j + 2, par, *nxt) if q == par else cmax[q] for q in range(2))
        return m_new, jnp.exp2(m - m_new), corr1, cmax

    far = (None, False)

    def pair(j, stats):
        return step(j + 1, 1, step(j, 0, stats, far), far)

    def two_pairs(jj, stats):
        j = first + 2 * (odd + 2 * jj)
        return pair(j + 2, pair(j, stats))

    p_buf[...] = jnp.zeros_like(p_buf)
    acc_ref[...] = jnp.zeros_like(acc_ref)
    ones = jnp.ones((1, width), F32)
    cmax = (fetch(first, 0, None, True), fetch(first + 1, 1, None, True))
    stats = (jnp.full((1, width), NEG, F32), ones, ones, cmax)
    if far_pairs_even:
        odd = 0
    else:
        odd = n_far_pairs % 2
        stats = lax.cond(odd == 1, lambda st: pair(first, st), lambda st: st, stats)
    stats = lax.fori_loop(0, n_far_pairs // 2, two_pairs, stats)
    last = first + 2 * n_far_pairs + len(tail)
    for r in range(len(tail)):
        stats = step(last - len(tail) + r, r % 2, stats, tail[r + 2] if r + 2 < len(tail) else None)
    _, corr1, corr2, _ = stats
    acc = acc_ref[...] * corr2 + _dot(values(last - 2), p_buf[0])
    acc = acc * corr1 + _dot(values(last - 1), p_buf[1])
    return acc[:HEAD_DIM] / acc[HEAD_DIM:HEAD_DIM + 1]


def _fox_attn_kernel(qt_ref, kp_ref, vt_ref, o_ref, s_buf, p_buf, acc_ref):
    i = pl.program_id(2)
    t = qt_ref.shape[3]
    per_tile = t // KEY_CHUNK

    def scores(c):
        rows = pl.ds(pl.multiple_of(jnp.maximum(c, 0) * KEY_CHUNK, KEY_CHUNK), KEY_CHUNK)
        return _dot(kp_ref[0, 0, rows, :], qt_ref[0, 0])

    def values(c):
        return vt_ref[0, 0, jnp.maximum(c, 0)]

    key = lax.broadcasted_iota(jnp.int32, (KEY_CHUNK, QUERY_SUB), 0)
    qry = lax.broadcasted_iota(jnp.int32, (KEY_CHUNK, QUERY_SUB), 1)
    causal = lambda r: (lambda c: jnp.where(key + r * KEY_CHUNK <= qry + c * QUERY_SUB, 0.0, NEG))
    assert per_tile % 4 == 0
    tail = [(None, True)] * per_tile + [(causal(r), False) for r in range(per_tile)]
    out = _flash_sweep(per_tile * jnp.minimum(i - 1, 0), (per_tile // 2) * jnp.maximum(i - 1, 0),
                       scores, values, tail, s_buf, p_buf, acc_ref, far_pairs_even=True)
    o_ref[0] = out.astype(o_ref.dtype)


def _fox_attention(qt, kp, vt):
    B, H, S, _ = kp.shape
    t = FOX_T
    return pl.pallas_call(
        _fox_attn_kernel, name="fox_attn",
        grid=(B, H, S // t),
        in_specs=[
            pl.BlockSpec((1, 1, LANES, t), lambda b, h, i: (b, h, 0, i)),
            pl.BlockSpec((1, 1, S, LANES), lambda b, h, i: (b, h, 0, 0)),
            pl.BlockSpec((1, 1, S // KEY_CHUNK, V_ROWS, KEY_CHUNK), lambda b, h, i: (b, h, 0, 0, 0)),
        ],
        out_specs=pl.BlockSpec((1, HEAD_DIM, t), lambda b, h, i: (b, h, i)),
        out_shape=jax.ShapeDtypeStruct((B, H * HEAD_DIM, S), BF16),
        scratch_shapes=[pltpu.VMEM((2, KEY_CHUNK, t), F32), pltpu.VMEM((2, KEY_CHUNK, t), BF16),
                        pltpu.VMEM((V_ROWS, t), F32)],
        compiler_params=_params("arbitrary", "arbitrary", "arbitrary"),
    )(qt, kp, vt)


def _outproj_kernel(at_ref, w_ref, x_ref, gate_ref, g_ref, b_ref, o_ref):
    tm = x_ref.shape[1]
    r = lax.broadcasted_iota(jnp.int32, (tm, tm), 0)
    c = lax.broadcasted_iota(jnp.int32, (tm, tm), 1)
    eye = (r == c).astype(BF16)
    a = _dot_nt(eye, at_ref[0]).astype(BF16)
    y = _dot(a, w_ref[...])
    z = DN_ALPHA * x_ref[0] + (1.0 + gate_ref[0]) * y
    o_ref[0] = _layer_norm(z, g_ref[...], b_ref[...])


def _outproj_ln(attn_t, w_out, x, gate, ln_g, ln_b):
    B, S, _ = x.shape
    tm = ROW_TILE
    vec = pl.BlockSpec((1, D_MODEL), lambda b, i: (0, 0))
    return pl.pallas_call(
        _outproj_kernel, name="outproj_ln",
        grid=(B, S // tm),
        in_specs=[
            pl.BlockSpec((1, D_MODEL, tm), lambda b, i: (b, 0, i)),
            pl.BlockSpec((D_MODEL, D_MODEL), lambda b, i: (0, 0)),
            pl.BlockSpec((1, tm, D_MODEL), lambda b, i: (b, i, 0)),
            pl.BlockSpec((1, 1, D_MODEL), lambda b, i: (b, 0, 0)),
            vec, vec,
        ],
        out_specs=pl.BlockSpec((1, tm, D_MODEL), lambda b, i: (b, i, 0)),
        out_shape=jax.ShapeDtypeStruct((B, S, D_MODEL), F32),
        compiler_params=_params("arbitrary", "arbitrary"),
    )(attn_t, w_out.astype(BF16), x, gate, ln_g.reshape(1, -1), ln_b.reshape(1, -1))


def _ffn_kernel(x_ref, sc_ref, sh_ref, gate_ref, wa_ref, wb_ref, wo_ref, g_ref, b_ref,
                o_ref, acc_ref):
    c = pl.program_id(2)
    x = x_ref[0]
    hb = (x * (1.0 + sc_ref[0]) + sh_ref[0]).astype(BF16)
    a = _dot(hb, wa_ref[...])
    b = _dot(hb, wb_ref[...])
    u = (a / (1.0 + jnp.exp(-a)) * b).astype(BF16)
    y = _dot(u, wo_ref[...])

    @pl.when(c == 0)
    def _():
        acc_ref[...] = y

    @pl.when(c > 0)
    def _():
        acc_ref[...] += y

    @pl.when(c == pl.num_programs(2) - 1)
    def _():
        z = DN_ALPHA * x + (1.0 + gate_ref[0]) * acc_ref[...]
        o_ref[0] = _layer_norm(z, g_ref[...], b_ref[...])


def _ffn_ln(x, sc, sh, gate, w_in, w_out, ln_g, ln_b):
    B, S, _ = x.shape
    tm = ROW_TILE
    nc = FF_HIDDEN // FFN_CHUNK
    w_in = w_in.astype(BF16)
    mod = pl.BlockSpec((1, 1, D_MODEL), lambda b, i, c: (b, 0, 0))
    vec = pl.BlockSpec((1, D_MODEL), lambda b, i, c: (0, 0))
    return pl.pallas_call(
        _ffn_kernel, name="ffn_ln",
        grid=(B, S // tm, nc),
        in_specs=[
            pl.BlockSpec((1, tm, D_MODEL), lambda b, i, c: (b, i, 0)),
            mod, mod, mod,
            pl.BlockSpec((D_MODEL, FFN_CHUNK), lambda b, i, c: (0, c)),
            pl.BlockSpec((D_MODEL, FFN_CHUNK), lambda b, i, c: (0, nc + c)),
            pl.BlockSpec((FFN_CHUNK, D_MODEL), lambda b, i, c: (c, 0)),
            vec, vec,
        ],
        out_specs=pl.BlockSpec((1, tm, D_MODEL), lambda b, i, c: (b, i, 0)),
        out_shape=jax.ShapeDtypeStruct((B, S, D_MODEL), F32),
        scratch_shapes=[pltpu.VMEM((tm, D_MODEL), F32)],
        compiler_params=_params("arbitrary", "arbitrary", "arbitrary"),
    )(x, sc, sh, gate, w_in, w_in, w_out.astype(BF16), ln_g.reshape(1, -1), ln_b.reshape(1, -1))


def _nsa_proj_kernel(x_ref, sc_ref, sh_ref, wqt_ref, wnat_ref, wvt_ref, wgt_ref, bg_ref,
                     qt_ref, kc_ref, vc_ref, ks_ref, kw_ref, vst_ref, vwt_ref, gt_ref):
    tm = x_ref.shape[1]
    t = NSA_T
    hb = (x_ref[0] * (1.0 + sc_ref[0]) + sh_ref[0]).astype(BF16)

    qt = _dot_nt(wqt_ref[...], hb).astype(BF16)
    for g in range(N_GROUPS):
        for hh in range(HEADS_PER_GROUP):
            r0 = (g * HEADS_PER_GROUP + hh) * HEAD_DIM
            for c in range(tm // t):
                qt_ref[0, g, c, :, hh * t:(hh + 1) * t] = qt[r0:r0 + HEAD_DIM, c * t:(c + 1) * t]

    nat = _dot(hb, wnat_ref[...])
    lane = lax.broadcasted_iota(jnp.int32, (tm, LANES), 1)
    ones = ((lane == HEAD_DIM) | (lane == HEAD_DIM + 1)).astype(F32)
    off_ks = 2 * KV_WIDTH
    off_kw = off_ks + N_GROUPS * LANES
    for g in range(N_GROUPS):
        kc_ref[0, g] = nat[:, g * HEAD_DIM:(g + 1) * HEAD_DIM]
        vc_ref[0, g] = nat[:, KV_WIDTH + g * HEAD_DIM:KV_WIDTH + (g + 1) * HEAD_DIM]
        ks_ref[0, g] = (nat[:, off_ks + g * LANES:off_ks + (g + 1) * LANES] + ones).astype(BF16)
        kw_ref[0, g] = nat[:, off_kw + g * LANES:off_kw + (g + 1) * LANES].astype(BF16)

    vt = _dot_nt(wvt_ref[...], hb).astype(BF16)
    for g in range(N_GROUPS):
        for c in range(tm // KEY_CHUNK):
            vst_ref[0, g, c, 0:HEAD_DIM, :] = vt[g * HEAD_DIM:(g + 1) * HEAD_DIM, c * KEY_CHUNK:(c + 1) * KEY_CHUNK]
            vst_ref[0, g, c, HEAD_DIM:V_ROWS, :] = _ones_row_tail(KEY_CHUNK)
        for c in range(tm // t):
            vwt_ref[0, g, c] = vt[KV_WIDTH + g * HEAD_DIM:KV_WIDTH + (g + 1) * HEAD_DIM, c * t:(c + 1) * t]

    gl = _dot_nt(wgt_ref[...], hb) + bg_ref[...]
    gt_ref[0] = 1.0 / (1.0 + jnp.exp(-gl))


def _nsa_proj(x, sc, sh, w_in, b_gate):
    B, S, _ = x.shape
    tm = ROW_TILE
    t = NSA_T
    scale = LOG2E * HEAD_DIM ** -0.5
    cuts = [D_MODEL + n * KV_WIDTH for n in range(7)]
    wq, wkc, wvc, wks, wvs, wkw, wvw, wg = jnp.split(w_in, cuts, axis=1)
    wqt = (wq * scale).T.astype(BF16)
    padk = lambda w: jnp.pad(w.reshape(D_MODEL, N_GROUPS, HEAD_DIM),
                             ((0, 0), (0, 0), (0, LANES - HEAD_DIM))).reshape(D_MODEL, N_GROUPS * LANES)
    wnat = jnp.concatenate([wkc, wvc, padk(wks), padk(wkw)], axis=1).astype(BF16)
    wvt = jnp.concatenate([wvs, wvw], axis=1).T.astype(BF16)
    n_gate = 3 * N_HEADS
    wgt = wg.T.astype(BF16)
    bg = jnp.broadcast_to(b_gate.reshape(n_gate, 1), (n_gate, tm))

    full = lambda a: pl.BlockSpec(a.shape, lambda b, i: (0,) * a.ndim)
    nat_spec = pl.BlockSpec((1, N_GROUPS, tm, HEAD_DIM), lambda b, i: (b, 0, i, 0))
    pad_spec = pl.BlockSpec((1, N_GROUPS, tm, LANES), lambda b, i: (b, 0, i, 0))
    vt_spec = lambda n, rows: pl.BlockSpec((1, N_GROUPS, tm // n, rows, n), lambda b, i: (b, 0, i, 0, 0))
    return pl.pallas_call(
        _nsa_proj_kernel, name="nsa_proj",
        grid=(B, S // tm),
        in_specs=[
            pl.BlockSpec((1, tm, D_MODEL), lambda b, i: (b, i, 0)),
            pl.BlockSpec((1, 1, D_MODEL), lambda b, i: (b, 0, 0)),
            pl.BlockSpec((1, 1, D_MODEL), lambda b, i: (b, 0, 0)),
            full(wqt), full(wnat), full(wvt), full(wgt), full(bg),
        ],
        out_specs=[
            pl.BlockSpec((1, N_GROUPS, tm // t, HEAD_DIM, HEADS_PER_GROUP * t), lambda b, i: (b, 0, i, 0, 0)),
            nat_spec, nat_spec, pad_spec, pad_spec, vt_spec(KEY_CHUNK, V_ROWS), vt_spec(t, HEAD_DIM),
            pl.BlockSpec((1, n_gate, tm), lambda b, i: (b, 0, i)),
        ],
        out_shape=[
            jax.ShapeDtypeStruct((B, N_GROUPS, S // t, HEAD_DIM, HEADS_PER_GROUP * t), BF16),
            jax.ShapeDtypeStruct((B, N_GROUPS, S, HEAD_DIM), F32),
            jax.ShapeDtypeStruct((B, N_GROUPS, S, HEAD_DIM), F32),
            jax.ShapeDtypeStruct((B, N_GROUPS, S, LANES), BF16),
            jax.ShapeDtypeStruct((B, N_GROUPS, S, LANES), BF16),
            jax.ShapeDtypeStruct((B, N_GROUPS, S // KEY_CHUNK, V_ROWS, KEY_CHUNK), BF16),
            jax.ShapeDtypeStruct((B, N_GROUPS, S // t, HEAD_DIM, t), BF16),
            jax.ShapeDtypeStruct((B, n_gate, S), F32),
        ],
        compiler_params=_params("arbitrary", "arbitrary"),
    )(x, sc, sh, wqt, wnat, wvt, wgt, bg)


def _compress_kernel(tk_ref, tv_ref, pos_ref, w1_ref, w2k_ref, w2vt_ref, kc_ref, vct_ref):
    n = tk_ref.shape[2]
    half = CMP_STRIDE * HEAD_DIM

    def hidden(t_ref, idx):
        t16 = t_ref[0, 0]
        xa = (t16 + pos_ref[idx, 0:1, :]).astype(BF16)
        xb = (t16 + pos_ref[idx, 1:2, :]).astype(BF16)
        first = _dot(xa, w1_ref[idx, :half, :])
        second = _dot(xb, w1_ref[idx, half:, :])
        pre = first + pltpu.roll(second, n - 1, 0)
        return (pre / (1.0 + jnp.exp(-pre))).astype(BF16)

    kc = _dot(hidden(tk_ref, 0), w2k_ref[...])
    lane = lax.broadcasted_iota(jnp.int32, kc.shape, 1)
    ones = ((lane == HEAD_DIM) | (lane == HEAD_DIM + 1)).astype(F32)
    kc_ref[0, 0] = (kc + ones).astype(BF16)
    vct = _dot_nt(w2vt_ref[...], hidden(tv_ref, 1)).astype(BF16)
    for r in range(n // CMP_ROWS):
        vct_ref[0, 0, r] = vct[:, r * CMP_ROWS:(r + 1) * CMP_ROWS]


def _compress(kc, vc, cmp_pos, cmp_w1, cmp_w2):
    B, G, S, _ = kc.shape
    n = S // CMP_STRIDE
    width = CMP_STRIDE * HEAD_DIM
    tk = kc.reshape(B, G, n, width)
    tv = vc.reshape(B, G, n, width)
    pos = cmp_pos.reshape(2, 2, width)
    w1 = cmp_w1.astype(BF16)
    w2k = jnp.pad(cmp_w2[0], ((0, 0), (0, LANES - HEAD_DIM))).astype(BF16)
    w2vt = cmp_w2[1].T.astype(BF16)
    full = lambda a: pl.BlockSpec(a.shape, lambda b, g: (0,) * a.ndim)
    t_spec = pl.BlockSpec((1, 1, n, width), lambda b, g: (b, g, 0, 0))
    return pl.pallas_call(
        _compress_kernel, name="nsa_compress",
        grid=(B, G),
        in_specs=[t_spec, t_spec, full(pos), full(w1), full(w2k), full(w2vt)],
        out_specs=[
            pl.BlockSpec((1, 1, n, LANES), lambda b, g: (b, g, 0, 0)),
            pl.BlockSpec((1, 1, n // CMP_ROWS, HEAD_DIM, CMP_ROWS), lambda b, g: (b, g, 0, 0, 0)),
        ],
        out_shape=[
            jax.ShapeDtypeStruct((B, G, n, LANES), BF16),
            jax.ShapeDtypeStruct((B, G, n // CMP_ROWS, HEAD_DIM, CMP_ROWS), BF16),
        ],
        compiler_params=_params("arbitrary", "arbitrary"),
    )(tk, tv, pos, w1, w2k, w2vt)


def _bias_tiles_kernel(rb_ref, sel_ref, win_ref):
    hd = pl.program_id(0)
    far = rb_ref[FAR_BUCKET, hd]

    def dist(n, back):
        key = lax.broadcasted_iota(jnp.int32, (n, n), 0)
        qry = lax.broadcasted_iota(jnp.int32, (n, n), 1)
        return qry - key + back * n

    kc = sel_ref.shape[2]
    d_prev, d_diag = dist(kc, 1), dist(kc, 0)
    sel_ref[0, 0] = LOG2E * (_bucket_bias(d_prev, rb_ref, hd) - far)
    sel_ref[1, 0] = jnp.where(d_diag >= 0, LOG2E * (_bucket_bias(d_diag, rb_ref, hd) - far), NEG)
    t = win_ref.shape[2]
    n_back = win_ref.shape[0] - 1
    for r in range(n_back + 1):
        d = dist(t, n_back - r)
        win_ref[r, 0] = jnp.where((d >= 0) & (d < WINDOW), LOG2E * _bucket_bias(d, rb_ref, hd), NEG)


def _bias_tiles(rel_bias):
    t = NSA_T
    n_win = WINDOW // t + 1
    assert WINDOW % t == 0
    spec = lambda n, m: pl.BlockSpec((n, 1, m, m), lambda h: (0, h, 0, 0))
    return pl.pallas_call(
        _bias_tiles_kernel, name="rel_bias_tiles",
        grid=(N_HEADS,),
        in_specs=[pl.BlockSpec(memory_space=pltpu.SMEM)],
        out_specs=[spec(2, KEY_CHUNK), spec(n_win, t)],
        out_shape=[
            jax.ShapeDtypeStruct((2, N_HEADS, KEY_CHUNK, KEY_CHUNK), F32),
            jax.ShapeDtypeStruct((n_win, N_HEADS, t, t), F32),
        ],
        compiler_params=_params("arbitrary"),
    )(rel_bias)


def _far_bias_rows(rb_ref, g, width):
    t = width // HEADS_PER_GROUP
    lane = lax.broadcasted_iota(jnp.int32, (HEAD_DIM, width), 1)
    row = lax.broadcasted_iota(jnp.int32, (HEAD_DIM, width), 0)
    far = jnp.zeros((HEAD_DIM, width), F32)
    for hh in range(HEADS_PER_GROUP):
        far = jnp.where(lane >= hh * t, LOG2E * rb_ref[FAR_BUCKET, g * HEADS_PER_GROUP + hh], far)
    hi = far.astype(BF16).astype(F32)
    return jnp.where(row == 0, hi, jnp.where(row == 1, far - hi, 0.0)).astype(BF16)


def _nsa_cmp_kernel(rb_ref, qt_ref, kc_ref, vct_ref, at_ref, oc_ref, sb_ref, qp_ref, s_ref, imp_ref):
    g = pl.program_id(1)
    i = pl.program_id(2)
    width = qt_ref.shape[4]
    t = width // HEADS_PER_GROUP
    n = kc_ref.shape[2]
    nb = at_ref.shape[1]
    t0 = i * t
    assert t & (t - 1) == 0 and n % CMP_ROWS == 0

    n_vis = (t0 + t - CMP_LEN) // CMP_STRIDE // CMP_ROWS + 1
    n_full = jnp.maximum((t0 - (CMP_LEN - 1)) // CMP_STRIDE + 1, 0) // CMP_ROWS
    rows_of = lambda r: pl.ds(pl.multiple_of(r * CMP_ROWS, CMP_ROWS), CMP_ROWS)

    def valid(r):
        blk = r * CMP_ROWS + lax.broadcasted_iota(jnp.int32, (CMP_ROWS, width), 0)
        qry = t0 + (lax.broadcasted_iota(jnp.int32, (CMP_ROWS, width), 1) & (t - 1))
        return qry >= blk * CMP_STRIDE + CMP_LEN - 1

    qp_ref[0:HEAD_DIM, :] = qt_ref[0, 0, 0]
    qp_ref[HEAD_DIM:LANES, :] = _far_bias_rows(rb_ref, g, width)

    def score_chunk(r, carry):
        s_ref[rows_of(r), :] = _dot(kc_ref[0, 0, rows_of(r), :], qp_ref[...])
        return carry

    lax.fori_loop(0, n_vis, score_chunk, 0)

    band = t // CMP_STRIDE + 16
    assert (FAR_DIST + CMP_LEN - 1) <= 16 * CMP_STRIDE and band <= n
    r0 = pl.multiple_of(jnp.clip(t0 // CMP_STRIDE - 16, 0, n - band), 8)
    blk_r = r0 + lax.broadcasted_iota(jnp.int32, (band, t), 0)
    qry_r = t0 + lax.broadcasted_iota(jnp.int32, (band, t), 1)
    dist_r = qry_r - (blk_r * CMP_STRIDE + CMP_LEN - 1)
    for hh in range(HEADS_PER_GROUP):
        hd = g * HEADS_PER_GROUP + hh
        corr = LOG2E * (_bucket_bias(dist_r, rb_ref, hd) - rb_ref[FAR_BUCKET, hd])
        s_ref[pl.ds(r0, band), hh * t:(hh + 1) * t] += corr

    def col_max(masked):
        def body(r, m):
            s = s_ref[rows_of(r), :]
            if masked:
                s = jnp.where(valid(r), s, NEG)
            return jnp.maximum(m, jnp.max(s, axis=0, keepdims=True))
        return body

    m = lax.fori_loop(0, n_full, col_max(False), jnp.full((1, width), NEG, F32))
    m = lax.fori_loop(n_full, n_vis, col_max(True), m)

    def exp_sum(masked):
        def body(r, l):
            p = jnp.exp2(s_ref[rows_of(r), :] - m)
            if masked:
                p = jnp.where(valid(r), p, 0.0)
            s_ref[rows_of(r), :] = p
            return l + jnp.sum(p, axis=0, keepdims=True)
        return body

    l = lax.fori_loop(0, n_full, exp_sum(False), jnp.zeros((1, width), F32))
    l = lax.fori_loop(n_full, n_vis, exp_sum(True), l)
    inv = 1.0 / jnp.maximum(l, TINY)

    oc_ref[0, 0, 0] = jnp.zeros((HEAD_DIM, width), F32)
    imp_ref[...] = jnp.zeros_like(imp_ref)

    def finish(r, carry):
        p = s_ref[rows_of(r), :] * inv
        oc_ref[0, 0, 0] += _dot(vct_ref[0, 0, r], p.astype(BF16))
        imp = p[:, 0:t]
        for hh in range(1, HEADS_PER_GROUP):
            imp = imp + p[:, hh * t:(hh + 1) * t]
        hi, mid, lo = _split3(imp)
        a = at_ref[r]
        imp_ref[...] += _dot(a, hi) + _dot(a, mid) + _dot(a, lo)
        return carry

    lax.fori_loop(0, n_vis, finish, 0)

    sblk = lax.broadcasted_iota(jnp.int32, (nb, t), 0)
    cur = (t0 + lax.broadcasted_iota(jnp.int32, (nb, t), 1)) // SEL_BLOCK
    forced = (sblk == 0) | (sblk == cur) | (sblk == cur - 1)
    is_cand = (sblk >= 1) & (sblk <= cur - 2)
    cand = jnp.where(is_cand, imp_ref[...], -1.0)
    sblk_f = sblk.astype(F32)
    for _ in range(SEL_TOPK - N_FORCED):
        best = jnp.max(cand, axis=0, keepdims=True)
        first = jnp.min(jnp.where(cand == best, sblk_f, float(nb)), axis=0, keepdims=True)
        cand = jnp.where(sblk_f == first, -1.0, cand)
    chosen = forced | (is_cand & (cand < 0.0))
    sb_ref[0, 0, 0] = jnp.where(chosen, 0.0, NEG).astype(BF16)


def _nsa_cmp(rel_bias, qt, kcmp, vcmp_t, n_sel):
    B, G, nq, _, width = qt.shape
    t = width // HEADS_PER_GROUP
    n = kcmp.shape[2]
    n_cmp = n - 1
    nb = -(-n_sel // LANES) * LANES
    R = SEL_BLOCK // CMP_STRIDE
    at = np.zeros((nb, n), np.float32)
    for j in range(n_sel):
        lo, hi = max(R * j - 1, 0), min(R * j + R - 1, n_cmp - 1)
        at[j, lo:hi + 1] = 1.0
    at = jnp.asarray(at.reshape(nb, n // CMP_ROWS, CMP_ROWS).transpose(1, 0, 2), BF16)
    return pl.pallas_call(
        _nsa_cmp_kernel, name="nsa_cmp_topk",
        grid=(B, G, nq),
        in_specs=[
            pl.BlockSpec(memory_space=pltpu.SMEM),
            pl.BlockSpec((1, 1, 1, HEAD_DIM, width), lambda b, g, i: (b, g, i, 0, 0)),
            pl.BlockSpec((1, 1, n, LANES), lambda b, g, i: (b, g, 0, 0)),
            pl.BlockSpec((1, 1, n // CMP_ROWS, HEAD_DIM, CMP_ROWS), lambda b, g, i: (b, g, 0, 0, 0)),
            pl.BlockSpec((n // CMP_ROWS, nb, CMP_ROWS), lambda b, g, i: (0, 0, 0)),
        ],
        out_specs=[
            pl.BlockSpec((1, 1, 1, HEAD_DIM, width), lambda b, g, i: (b, g, i, 0, 0)),
            pl.BlockSpec((1, 1, 1, nb, t), lambda b, g, i: (b, g, i, 0, 0)),
        ],
        out_shape=[
            jax.ShapeDtypeStruct((B, G, nq, HEAD_DIM, width), F32),
            jax.ShapeDtypeStruct((B, G, nq, nb, t), BF16),
        ],
        scratch_shapes=[pltpu.VMEM((LANES, width), BF16), pltpu.VMEM((n, width), F32),
                        pltpu.VMEM((nb, t), F32)],
        compiler_params=_params("arbitrary", "arbitrary", "arbitrary"),
    )(rel_bias, qt, kcmp, vcmp_t, at)


def _nsa_win_kernel(qt_ref, *refs):
    n_win = (len(refs) - 2) // 2
    k_refs, v_refs, (wb_ref, ow_ref) = refs[:n_win], refs[n_win:2 * n_win], refs[2 * n_win:]
    i = pl.program_id(2)
    width = qt_ref.shape[4]
    t = width // HEADS_PER_GROUP
    qt = qt_ref[0, 0, 0]
    qp = jnp.concatenate([qt, jnp.zeros_like(qt)], axis=0)
    scores = []
    for c in range(n_win):
        s = _dot(k_refs[c][0, 0], qp)
        missing = jnp.where(i - (n_win - 1 - c) < 0, NEG, 0.0)
        s = jnp.concatenate([s[:, hh * t:(hh + 1) * t] + (wb_ref[c, hh] + missing)
                             for hh in range(HEADS_PER_GROUP)], axis=1)
        scores.append(s)
    m = jnp.max(scores[0], axis=0, keepdims=True)
    for s in scores[1:]:
        m = jnp.maximum(m, jnp.max(s, axis=0, keepdims=True))
    l = jnp.zeros_like(m)
    acc = jnp.zeros((HEAD_DIM, width), F32)
    for c in range(n_win):
        p = jnp.exp2(scores[c] - m)
        l = l + jnp.sum(p, axis=0, keepdims=True)
        acc = acc + _dot(v_refs[c][0, 0, 0], p.astype(BF16))
    ow_ref[0, 0, 0] = acc / l


def _nsa_window(qt, kw, vwt, win_bias):
    B, G, nq, _, width = qt.shape
    t = width // HEADS_PER_GROUP
    n_win = win_bias.shape[0]
    backs = list(range(n_win - 1, -1, -1))
    k_spec = lambda back: pl.BlockSpec((1, 1, t, LANES), lambda b, g, i: (b, g, jnp.maximum(i - back, 0), 0))
    v_spec = lambda back: pl.BlockSpec((1, 1, 1, HEAD_DIM, t),
                                       lambda b, g, i: (b, g, jnp.maximum(i - back, 0), 0, 0))
    return pl.pallas_call(
        _nsa_win_kernel, name="nsa_window",
        grid=(B, G, nq),
        in_specs=[pl.BlockSpec((1, 1, 1, HEAD_DIM, width), lambda b, g, i: (b, g, i, 0, 0))]
        + [k_spec(back) for back in backs] + [v_spec(back) for back in backs]
        + [pl.BlockSpec((n_win, HEADS_PER_GROUP, t, t), lambda b, g, i: (0, g, 0, 0))],
        out_specs=pl.BlockSpec((1, 1, 1, HEAD_DIM, width), lambda b, g, i: (b, g, i, 0, 0)),
        out_shape=jax.ShapeDtypeStruct((B, G, nq, HEAD_DIM, width), F32),
        compiler_params=_params("arbitrary", "arbitrary", "arbitrary"),
    )(qt, *([kw] * n_win), *([vwt] * n_win), win_bias)


def _nsa_sel_kernel(rb_ref, qt_ref, sb_ref, ks_ref, e_ref, vst_ref, cb_ref, oc_ref, ow_ref, gt_ref,
                    o_ref, qp_ref, s_buf, p_buf, acc_ref):
    g = pl.program_id(1)
    i = pl.program_id(2)
    width = qt_ref.shape[4]
    t = width // HEADS_PER_GROUP
    per_tile = t // KEY_CHUNK
    chunks_per_slab = LANES * SEL_BLOCK // KEY_CHUNK
    blocks_per_head = t // QUERY_SUB
    assert per_tile == 2 and t % QUERY_SUB == 0

    far_rows = _far_bias_rows(rb_ref, g, width)
    for slab in range(qp_ref.shape[0]):
        qp_ref[slab, 0:HEAD_DIM, :] = qt_ref[0, 0, 0]
        qp_ref[slab, HEAD_DIM:LANES, :] = far_rows
        sb = sb_ref[0, 0, 0, slab * LANES:(slab + 1) * LANES, :]
        qp_ref[slab, LANES:2 * LANES, :] = jnp.concatenate([sb] * HEADS_PER_GROUP, axis=1)

    def scores(c):
        c = jnp.maximum(c, 0)
        rows = pl.ds(pl.multiple_of(c * KEY_CHUNK, KEY_CHUNK), KEY_CHUNK)
        kp = jnp.concatenate([ks_ref[0, 0, rows, :], e_ref[rows, :]], axis=1)
        return _dot(kp, qp_ref[c // chunks_per_slab])

    def values(c):
        return vst_ref[0, 0, jnp.maximum(c, 0)]

    def near_bias(r):
        def bias(c):
            hh, part = divmod(c, blocks_per_head)
            ahead = part * QUERY_SUB // KEY_CHUNK - (r - per_tile)
            if ahead == 0:
                return cb_ref[1, hh]
            if ahead == 1:
                return cb_ref[0, hh]
            if ahead < 0:
                return jnp.full((KEY_CHUNK, QUERY_SUB), NEG, F32)
            return None
        return bias

    tail = [(None, True), (None, True)] + [(near_bias(r), r < per_tile) for r in range(2 * per_tile)]
    o_sel = _flash_sweep(per_tile * jnp.minimum(i - 2, 0), jnp.maximum(i - 2, 0), scores, values, tail,
                         s_buf, p_buf, acc_ref)

    out = gt_ref[0, 0, 0, 0:1, :] * oc_ref[0, 0, 0] + gt_ref[0, 0, 0, 1:2, :] * o_sel \
        + gt_ref[0, 0, 0, 2:3, :] * ow_ref[0, 0, 0]
    for hh in range(HEADS_PER_GROUP):
        o_ref[0, hh * HEAD_DIM:(hh + 1) * HEAD_DIM, :] = out[:, hh * t:(hh + 1) * t].astype(o_ref.dtype)


def _nsa_select_combine(rel_bias, qt, sel_bias, ks, vst, sel_corr, oc, ow, gates):
    B, G, nq, _, width = qt.shape
    t = width // HEADS_PER_GROUP
    S = nq * t
    nb = sel_bias.shape[3]
    blocks = (np.arange(S) // SEL_BLOCK) % LANES
    onehot = jnp.asarray(blocks[:, None] == np.arange(LANES)[None, :], BF16)
    tile = pl.BlockSpec((1, 1, 1, HEAD_DIM, width), lambda b, g, i: (b, g, i, 0, 0))
    return pl.pallas_call(
        _nsa_sel_kernel, name="nsa_select",
        grid=(B, G, nq),
        in_specs=[
            pl.BlockSpec(memory_space=pltpu.SMEM),
            tile,
            pl.BlockSpec((1, 1, 1, nb, t), lambda b, g, i: (b, g, i, 0, 0)),
            pl.BlockSpec((1, 1, S, LANES), lambda b, g, i: (b, g, 0, 0)),
            pl.BlockSpec((S, LANES), lambda b, g, i: (0, 0)),
            pl.BlockSpec((1, 1, S // KEY_CHUNK, V_ROWS, KEY_CHUNK), lambda b, g, i: (b, g, 0, 0, 0)),
            pl.BlockSpec((2, HEADS_PER_GROUP, KEY_CHUNK, KEY_CHUNK), lambda b, g, i: (0, g, 0, 0)),
            tile, tile,
            pl.BlockSpec((1, 1, 1, 8, width), lambda b, g, i: (b, g, i, 0, 0)),
        ],
        out_specs=pl.BlockSpec((1, HEADS_PER_GROUP * HEAD_DIM, t), lambda b, g, i: (b, g, i)),
        out_shape=jax.ShapeDtypeStruct((B, D_MODEL, S), BF16),
        scratch_shapes=[pltpu.VMEM((nb // LANES, 2 * LANES, width), BF16),
                        pltpu.VMEM((2, KEY_CHUNK, width), F32), pltpu.VMEM((2, KEY_CHUNK, width), BF16),
                        pltpu.VMEM((V_ROWS, width), F32)],
        compiler_params=_params("arbitrary", "arbitrary", "arbitrary"),
    )(rel_bias, qt, sel_bias, ks, onehot, vst, sel_corr, oc, ow, gates)


def _nsa_attention_t(x, sc, sh, w_in, b_gate, cmp_pos, cmp_w1, cmp_w2, rel_bias):
    B, S, _ = x.shape
    t = NSA_T
    n_sel = S // SEL_BLOCK
    assert S % ROW_TILE == 0 and n_sel >= SEL_TOPK and S // CMP_STRIDE >= t // CMP_STRIDE + 16
    qt, kc, vc, ks, kw, vst, vwt, gt = _nsa_proj(x, sc, sh, w_in, b_gate)
    kcmp, vcmp_t = _compress(kc, vc, cmp_pos, cmp_w1, cmp_w2)
    sel_corr, win_bias = _bias_tiles(rel_bias)
    oc, sel_bias = _nsa_cmp(rel_bias, qt, kcmp, vcmp_t, n_sel)
    ow = _nsa_window(qt, kw, vwt, win_bias)
    gates = gt.reshape(B, 3, N_GROUPS, HEADS_PER_GROUP, S // t, t).transpose(0, 2, 4, 1, 3, 5)
    gates = jnp.pad(gates.reshape(B, N_GROUPS, S // t, 3, HEADS_PER_GROUP * t),
                    ((0, 0), (0, 0), (0, 0), (0, 5), (0, 0)))
    return _nsa_select_combine(rel_bias, qt, sel_bias, ks, vst, sel_corr, oc, ow, gates)


def kernel(x, c, ada_w, ada_b, ln_g, ln_b, fox_w_in, fox_b_f, fox_w_out, nsa_w_in, nsa_b_gate,
           nsa_cmp_pos, nsa_cmp_w1, nsa_cmp_w2, nsa_w_out, rel_bias, ffn_w_in, ffn_w_out):
    B, S, _ = x.shape
    assert S % FOX_T == 0 and S % ROW_TILE == 0 and ROW_TILE == NSA_T
    mod = _modulation(c, ada_w, ada_b)
    for layer in range(DEPTH):
        sh_a, sc_a, g_a, sh_f, sc_f, g_f = [m.reshape(B, 1, D_MODEL) for m in jnp.split(mod[layer], 6, axis=-1)]
        j = layer // 2
        if layer % 2 == 0:
            kp, qt, vt = _fox_proj(x, sc_a, sh_a, fox_w_in[j], fox_b_f[j])
            attn_t = _fox_attention(qt, kp, vt)
            w_out = fox_w_out[j]
        else:
            attn_t = _nsa_attention_t(x, sc_a, sh_a, nsa_w_in[j], nsa_b_gate[j], nsa_cmp_pos[j],
                                      nsa_cmp_w1[j], nsa_cmp_w2[j], rel_bias)
            w_out = nsa_w_out[j]
        x = _outproj_ln(attn_t, w_out, x, g_a, ln_g[layer, 0], ln_b[layer, 0])
        x = _ffn_ln(x, sc_f, sh_f, g_f, ffn_w_in[layer], ffn_w_out[layer], ln_g[layer, 1], ln_b[layer, 1])
    return x
```

```python
import functools
import math

import numpy as np
import jax
import jax.numpy as jnp
from jax import lax
from jax.experimental import pallas as pl
from jax.experimental.pallas import tpu as pltpu

F32 = jnp.float32
BF16 = jnp.bfloat16
HIGHEST = lax.Precision.HIGHEST

D_MODEL = 1024
HEAD_DIM = 64
N_HEADS = 16
N_GROUPS = 4
HEADS_PER_GROUP = 4
KV_WIDTH = N_GROUPS * HEAD_DIM
CMP_LEN = 32
CMP_STRIDE = 16
CMP_HIDDEN = 128
SEL_BLOCK = 64
SEL_TOPK = 16
WINDOW = 512
REL_BUCKETS = 32
FF_HIDDEN = 2816
DEPTH = 2
DN_ALPHA = (2 * DEPTH) ** 0.25
LN_EPS = 1e-5
NEG = -1e30
TINY = 1e-30

LANES = 128
VMEM_LIMIT = 56 * 1024 * 1024

ROW_TILE = 512
FOX_T = 1024
NSA_T = 512
FFN_CHUNK = 1408
N_FORCED = 3

BUCKET_START = (0, 1, 2, 3, 4, 5, 6, 7, 8, 9, 10, 11, 12, 13, 14, 15,
                16, 19, 21, 24, 27, 31, 35, 40, 46, 52, 59, 67, 77, 87, 99, 113)
FAR_BUCKET = REL_BUCKETS - 1
FAR_DIST = BUCKET_START[FAR_BUCKET]

LOG2E = math.log2(math.e)
QUERY_SUB = 256
KEY_CHUNK = 256
PAIRS_PER_TRIP = 4
CMP_ROWS = 128
V_ROWS = HEAD_DIM + 16
NT_DIMS = (((1,), (1,)), ((), ()))


def _params(*sem):
    return pltpu.CompilerParams(dimension_semantics=sem, vmem_limit_bytes=VMEM_LIMIT)


def _dot(a, b, **kw):
    return jnp.dot(a, b, preferred_element_type=F32, **kw)


def _dot_nt(a, b):
    return lax.dot_general(a, b, NT_DIMS, preferred_element_type=F32)


def _split3(v):
    hi = v.astype(BF16)
    r = v - hi.astype(F32)
    mid = r.astype(BF16)
    lo = (r - mid.astype(F32)).astype(BF16)
    return hi, mid, lo


def _layer_norm(z, g, b):
    mu = jnp.mean(z, axis=-1, keepdims=True)
    zc = z - mu
    var = jnp.mean(zc * zc, axis=-1, keepdims=True)
    return zc * lax.rsqrt(var + LN_EPS) * g + b


def _bucket_bias(dist, rb_ref, head):
    bias = jnp.full(dist.shape, rb_ref[0, head], F32)
    for k in range(1, REL_BUCKETS):
        bias = jnp.where(dist >= BUCKET_START[k], rb_ref[k, head], bias)
    return bias


def _mod_kernel(c_ref, w_ref, b_ref, o_ref):
    c = c_ref[...]
    cs = c / (1.0 + jnp.exp(-c))
    o_ref[0] = _dot(cs, w_ref[0], precision=HIGHEST) + b_ref[0]


def _modulation(c, ada_w, ada_b):
    B = c.shape[0]
    depth, _, n = ada_w.shape
    rows = 8
    c_pad = jnp.pad(c, ((0, rows - B), (0, 0)))
    tn = 1536
    out = pl.pallas_call(
        _mod_kernel, name="adaln_mod",
        grid=(depth, n // tn),
        in_specs=[
            pl.BlockSpec((rows, D_MODEL), lambda l, j: (0, 0)),
            pl.BlockSpec((1, D_MODEL, tn), lambda l, j: (l, 0, j)),
            pl.BlockSpec((1, 1, tn), lambda l, j: (l, 0, j)),
        ],
        out_specs=pl.BlockSpec((1, rows, tn), lambda l, j: (l, 0, j)),
        out_shape=jax.ShapeDtypeStruct((depth, rows, n), F32),
        compiler_params=_params("arbitrary", "arbitrary"),
    )(c_pad, ada_w, ada_b.reshape(depth, 1, n))
    return out[:, :B]


def _fox_proj_kernel(x_ref, sc_ref, sh_ref, wk_ref, wqt_ref, wvt_ref, wfh_ref, wfl_ref,
                     bf_ref, place_ref, kp_ref, qt_ref, vt_ref, carry_ref):
    tm = x_ref.shape[1]

    @pl.when(pl.program_id(1) == 0)
    def _():
        carry_ref[...] = jnp.zeros_like(carry_ref)

    h = x_ref[0] * (1.0 + sc_ref[0]) + sh_ref[0]
    hb = h.astype(BF16)
    hl = (h - hb.astype(F32)).astype(BF16)

    f = _dot(hb, wfh_ref[...]) + _dot(hl, wfh_ref[...]) + _dot(hb, wfl_ref[...])
    z = f + bf_ref[...]
    logf = jnp.minimum(z, 0.0) - jnp.log(1.0 + jnp.exp(-jnp.abs(z)))
    row = lax.broadcasted_iota(jnp.int32, (tm, tm), 0)
    col = lax.broadcasted_iota(jnp.int32, (tm, tm), 1)
    lower = (col <= row).astype(F32)
    cum = _dot(lower, logf, precision=HIGHEST) + carry_ref[0:1, :]
    carry_ref[...] = jnp.broadcast_to(cum[tm - 1:tm, :], carry_ref.shape)

    hi, mid, lo = [p.astype(F32) for p in _split3(-LOG2E * cum)]
    lane = lax.broadcasted_iota(jnp.int32, (tm, LANES), 1)
    pieces = jnp.where(lane < N_HEADS, hi, jnp.where(lane < 2 * N_HEADS, mid, lo)).astype(BF16)
    kp = _dot(hb, wk_ref[...]) + _dot(pieces, place_ref[...])
    qt = _dot_nt(wqt_ref[...], hb).astype(BF16)
    qrow = lax.broadcasted_iota(jnp.int32, (LANES - HEAD_DIM, tm), 0)
    q_ones = (qrow < 3).astype(F32).astype(BF16)
    vt = _dot_nt(wvt_ref[...], hb)
    for hd in range(N_HEADS):
        kp_ref[0, hd] = kp[:, hd * LANES:(hd + 1) * LANES].astype(BF16)
        qt_ref[0, hd, 0:HEAD_DIM, :] = qt[hd * HEAD_DIM:(hd + 1) * HEAD_DIM, :]
        qt_ref[0, hd, HEAD_DIM:LANES, :] = q_ones
        for c in range(tm // KEY_CHUNK):
            vt_ref[0, hd, c, 0:HEAD_DIM, :] = vt[hd * HEAD_DIM:(hd + 1) * HEAD_DIM,
                                                 c * KEY_CHUNK:(c + 1) * KEY_CHUNK].astype(BF16)
            vt_ref[0, hd, c, HEAD_DIM:V_ROWS, :] = _ones_row_tail(KEY_CHUNK)


def _fox_proj(x, sc, sh, w_in, b_f):
    B, S, _ = x.shape
    tm = ROW_TILE
    scale = LOG2E * HEAD_DIM ** -0.5
    wq = (w_in[:, :D_MODEL] * scale).reshape(D_MODEL, N_HEADS, HEAD_DIM)
    wk = w_in[:, D_MODEL:2 * D_MODEL].reshape(D_MODEL, N_HEADS, HEAD_DIM)
    wv = w_in[:, 2 * D_MODEL:3 * D_MODEL]
    wf = w_in[:, 3 * D_MODEL:]
    pad = ((0, 0), (0, 0), (0, LANES - HEAD_DIM))
    wk_p = jnp.pad(wk, pad).reshape(D_MODEL, N_HEADS * LANES).astype(BF16)
    wqt_p = wq.reshape(D_MODEL, N_HEADS * HEAD_DIM).T.astype(BF16)
    wvt = wv.T.astype(BF16)
    wf_rep = jnp.pad(jnp.tile(wf, (1, 3)), ((0, 0), (0, LANES - 3 * N_HEADS)))
    wf_hi = wf_rep.astype(BF16)
    wf_lo = (wf_rep - wf_hi.astype(F32)).astype(BF16)
    bf_rep = jnp.pad(jnp.tile(b_f, 3), (0, LANES - 3 * N_HEADS)).reshape(1, LANES)
    place = np.zeros((LANES, N_HEADS * LANES), np.float32)
    for r in range(3):
        for hd in range(N_HEADS):
            place[r * N_HEADS + hd, hd * LANES + HEAD_DIM + r] = 1.0
    place = jnp.asarray(place, BF16)

    full = lambda a: pl.BlockSpec(a.shape, lambda b, i: (0,) * a.ndim)
    return pl.pallas_call(
        _fox_proj_kernel, name="fox_proj",
        grid=(B, S // tm),
        in_specs=[
            pl.BlockSpec((1, tm, D_MODEL), lambda b, i: (b, i, 0)),
            pl.BlockSpec((1, 1, D_MODEL), lambda b, i: (b, 0, 0)),
            pl.BlockSpec((1, 1, D_MODEL), lambda b, i: (b, 0, 0)),
            full(wk_p), full(wqt_p), full(wvt), full(wf_hi), full(wf_lo), full(bf_rep), full(place),
        ],
        out_specs=[
            pl.BlockSpec((1, N_HEADS, tm, LANES), lambda b, i: (b, 0, i, 0)),
            pl.BlockSpec((1, N_HEADS, LANES, tm), lambda b, i: (b, 0, 0, i)),
            pl.BlockSpec((1, N_HEADS, tm // KEY_CHUNK, V_ROWS, KEY_CHUNK), lambda b, i: (b, 0, i, 0, 0)),
        ],
        out_shape=[
            jax.ShapeDtypeStruct((B, N_HEADS, S, LANES), BF16),
            jax.ShapeDtypeStruct((B, N_HEADS, LANES, S), BF16),
            jax.ShapeDtypeStruct((B, N_HEADS, S // KEY_CHUNK, V_ROWS, KEY_CHUNK), BF16),
        ],
        scratch_shapes=[pltpu.VMEM((8, LANES), F32)],
        compiler_params=_params("arbitrary", "arbitrary"),
    )(x, sc, sh, wk_p, wqt_p, wvt, wf_hi, wf_lo, bf_rep, place)


def _ones_row_tail(width):
    row = lax.broadcasted_iota(jnp.int32, (V_ROWS - HEAD_DIM, width), 0)
    return (row == 0).astype(F32).astype(BF16)


def _flash_sweep(first, n_far_pairs, scores, values, tail, s_buf, p_buf, acc_ref, far_pairs_even=False):
    width = acc_ref.shape[1]
    n_blocks = width // QUERY_SUB
    assert len(tail) % 2 == 0 and width % QUERY_SUB == 0
    assert tail[0][0] is None and tail[1][0] is None
    cat = lambda parts: jnp.concatenate(parts, axis=1)

    def fetch(c, par, bias, may_be_missing):
        block_scores = scores(c)
        missing = jnp.where(c < 0, NEG, 0.0) if may_be_missing else None
        col_max = []
        for blk in range(n_blocks):
            cols = slice(blk * QUERY_SUB, (blk + 1) * QUERY_SUB)
            part = block_scores(cols)
            extra = None if bias is None else bias(blk)
            if extra is not None:
                part = part + extra
            if missing is not None:
                part = part + missing
            s_buf[par, :, cols] = part
            col_max.append(jnp.max(part, axis=0, keepdims=True))
        return cat(col_max)

    def step(j, par, stats, nxt=None):
        m, corr1, corr2, cmax = stats
        acc_ref[...] = acc_ref[...] * corr2 + _dot(values(j - 2), p_buf[par])
        m_new = jnp.maximum(m, cmax[par])
        for blk in range(n_blocks):
            cols = slice(blk * QUERY_SUB, (blk + 1) * QUERY_SUB)
            p_buf[par, :, cols] = jnp.exp2((s_buf[par, :, cols] - m_new[:, cols]).astype(BF16))
        if nxt is not None:
            cmax = tuple(fetch(j + 2, par, *nxt) if q == par else cmax[q] for q in range(2))
        return m_new, jnp.exp2(m - m_new), corr1, cmax

    far = (None, False)

    def pairs(j, n, stats):
        for k in range(n):
            stats = step(j + 2 * k + 1, 1, step(j + 2 * k, 0, stats, far), far)
        return stats

    p_buf[...] = jnp.zeros_like(p_buf)
    acc_ref[...] = jnp.zeros_like(acc_ref)
    ones = jnp.ones((1, width), F32)
    cmax = (fetch(first, 0, None, True), fetch(first + 1, 1, None, True))
    stats = (jnp.full((1, width), NEG, F32), ones, ones, cmax)
    start = first
    for n in (1, 2):
        if n == 1 and far_pairs_even:
            continue
        group = (n_far_pairs // n) % 2
        stats = lax.cond(group == 1, lambda st, j=start, n=n: pairs(j, n, st), lambda st: st, stats)
        start = start + 2 * n * group
    stats = lax.fori_loop(0, n_far_pairs // PAIRS_PER_TRIP,
                          lambda jj, st: pairs(start + 2 * PAIRS_PER_TRIP * jj, PAIRS_PER_TRIP, st), stats)
    last = first + 2 * n_far_pairs + len(tail)
    for r in range(len(tail)):
        stats = step(last - len(tail) + r, r % 2, stats, tail[r + 2] if r + 2 < len(tail) else None)
    _, corr1, corr2, _ = stats
    acc = acc_ref[...] * corr2 + _dot(values(last - 2), p_buf[0])
    acc = acc * corr1 + _dot(values(last - 1), p_buf[1])
    return acc[:HEAD_DIM] / acc[HEAD_DIM:HEAD_DIM + 1]


def _fox_attn_kernel(qt_ref, kp_ref, vt_ref, o_ref, s_buf, p_buf, acc_ref):
    i = pl.program_id(2)
    t = qt_ref.shape[3]
    per_tile = t // KEY_CHUNK

    def scores(c):
        rows = pl.ds(pl.multiple_of(jnp.maximum(c, 0) * KEY_CHUNK, KEY_CHUNK), KEY_CHUNK)
        k = kp_ref[0, 0, rows, :]
        return lambda cols: _dot(k, qt_ref[0, 0, :, cols])

    def values(c):
        return vt_ref[0, 0, jnp.maximum(c, 0)]

    key = lax.broadcasted_iota(jnp.int32, (KEY_CHUNK, QUERY_SUB), 0)
    qry = lax.broadcasted_iota(jnp.int32, (KEY_CHUNK, QUERY_SUB), 1)
    causal = lambda r: (lambda c: jnp.where(key + r * KEY_CHUNK <= qry + c * QUERY_SUB, 0.0, NEG))
    assert per_tile % 4 == 0
    tail = [(None, True)] * per_tile + [(causal(r), False) for r in range(per_tile)]
    out = _flash_sweep(per_tile * jnp.minimum(i - 1, 0), (per_tile // 2) * jnp.maximum(i - 1, 0),
                       scores, values, tail, s_buf, p_buf, acc_ref, far_pairs_even=True)
    o_ref[0] = out.astype(o_ref.dtype)


def _fox_attention(qt, kp, vt):
    B, H, S, _ = kp.shape
    t = FOX_T
    return pl.pallas_call(
        _fox_attn_kernel, name="fox_attn",
        grid=(B, H, S // t),
        in_specs=[
            pl.BlockSpec((1, 1, LANES, t), lambda b, h, i: (b, h, 0, i)),
            pl.BlockSpec((1, 1, S, LANES), lambda b, h, i: (b, h, 0, 0)),
            pl.BlockSpec((1, 1, S // KEY_CHUNK, V_ROWS, KEY_CHUNK), lambda b, h, i: (b, h, 0, 0, 0)),
        ],
        out_specs=pl.BlockSpec((1, HEAD_DIM, t), lambda b, h, i: (b, h, i)),
        out_shape=jax.ShapeDtypeStruct((B, H * HEAD_DIM, S), BF16),
        scratch_shapes=[pltpu.VMEM((2, KEY_CHUNK, t), F32), pltpu.VMEM((2, KEY_CHUNK, t), BF16),
                        pltpu.VMEM((V_ROWS, t), F32)],
        compiler_params=_params("arbitrary", "arbitrary", "arbitrary"),
    )(qt, kp, vt)


def _outproj_kernel(at_ref, w_ref, x_ref, gate_ref, g_ref, b_ref, o_ref):
    tm = x_ref.shape[1]
    r = lax.broadcasted_iota(jnp.int32, (tm, tm), 0)
    c = lax.broadcasted_iota(jnp.int32, (tm, tm), 1)
    eye = (r == c).astype(BF16)
    a = _dot_nt(eye, at_ref[0]).astype(BF16)
    y = _dot(a, w_ref[...])
    z = DN_ALPHA * x_ref[0] + (1.0 + gate_ref[0]) * y
    o_ref[0] = _layer_norm(z, g_ref[...], b_ref[...])


def _outproj_ln(attn_t, w_out, x, gate, ln_g, ln_b):
    B, S, _ = x.shape
    tm = ROW_TILE
    vec = pl.BlockSpec((1, D_MODEL), lambda b, i: (0, 0))
    return pl.pallas_call(
        _outproj_kernel, name="outproj_ln",
        grid=(B, S // tm),
        in_specs=[
            pl.BlockSpec((1, D_MODEL, tm), lambda b, i: (b, 0, i)),
            pl.BlockSpec((D_MODEL, D_MODEL), lambda b, i: (0, 0)),
            pl.BlockSpec((1, tm, D_MODEL), lambda b, i: (b, i, 0)),
            pl.BlockSpec((1, 1, D_MODEL), lambda b, i: (b, 0, 0)),
            vec, vec,
        ],
        out_specs=pl.BlockSpec((1, tm, D_MODEL), lambda b, i: (b, i, 0)),
        out_shape=jax.ShapeDtypeStruct((B, S, D_MODEL), F32),
        compiler_params=_params("arbitrary", "arbitrary"),
    )(attn_t, w_out.astype(BF16), x, gate, ln_g.reshape(1, -1), ln_b.reshape(1, -1))


def _ffn_kernel(x_ref, sc_ref, sh_ref, gate_ref, wa_ref, wb_ref, wo_ref, g_ref, b_ref,
                o_ref, acc_ref):
    c = pl.program_id(2)
    x = x_ref[0]
    hb = (x * (1.0 + sc_ref[0]) + sh_ref[0]).astype(BF16)
    a = _dot(hb, wa_ref[...])
    b = _dot(hb, wb_ref[...])
    u = (a / (1.0 + jnp.exp(-a)) * b).astype(BF16)
    y = _dot(u, wo_ref[...])

    @pl.when(c == 0)
    def _():
        acc_ref[...] = y

    @pl.when(c > 0)
    def _():
        acc_ref[...] += y

    @pl.when(c == pl.num_programs(2) - 1)
    def _():
        z = DN_ALPHA * x + (1.0 + gate_ref[0]) * acc_ref[...]
        o_ref[0] = _layer_norm(z, g_ref[...], b_ref[...])


def _ffn_ln(x, sc, sh, gate, w_in, w_out, ln_g, ln_b):
    B, S, _ = x.shape
    tm = ROW_TILE
    nc = FF_HIDDEN // FFN_CHUNK
    w_in = w_in.astype(BF16)
    mod = pl.BlockSpec((1, 1, D_MODEL), lambda b, i, c: (b, 0, 0))
    vec = pl.BlockSpec((1, D_MODEL), lambda b, i, c: (0, 0))
    return pl.pallas_call(
        _ffn_kernel, name="ffn_ln",
        grid=(B, S // tm, nc),
        in_specs=[
            pl.BlockSpec((1, tm, D_MODEL), lambda b, i, c: (b, i, 0)),
            mod, mod, mod,
            pl.BlockSpec((D_MODEL, FFN_CHUNK), lambda b, i, c: (0, c)),
            pl.BlockSpec((D_MODEL, FFN_CHUNK), lambda b, i, c: (0, nc + c)),
            pl.BlockSpec((FFN_CHUNK, D_MODEL), lambda b, i, c: (c, 0)),
            vec, vec,
        ],
        out_specs=pl.BlockSpec((1, tm, D_MODEL), lambda b, i, c: (b, i, 0)),
        out_shape=jax.ShapeDtypeStruct((B, S, D_MODEL), F32),
        scratch_shapes=[pltpu.VMEM((tm, D_MODEL), F32)],
        compiler_params=_params("arbitrary", "arbitrary", "arbitrary"),
    )(x, sc, sh, gate, w_in, w_in, w_out.astype(BF16), ln_g.reshape(1, -1), ln_b.reshape(1, -1))


def _nsa_proj_kernel(x_ref, sc_ref, sh_ref, wqt_ref, wnat_ref, wvt_ref, wgt_ref, bg_ref,
                     qt_ref, kc_ref, vc_ref, ks_ref, kw_ref, vst_ref, vwt_ref, gt_ref):
    tm = x_ref.shape[1]
    t = NSA_T
    hb = (x_ref[0] * (1.0 + sc_ref[0]) + sh_ref[0]).astype(BF16)

    qt = _dot_nt(wqt_ref[...], hb).astype(BF16)
    for g in range(N_GROUPS):
        for hh in range(HEADS_PER_GROUP):
            r0 = (g * HEADS_PER_GROUP + hh) * HEAD_DIM
            for c in range(tm // t):
                qt_ref[0, g, c, :, hh * t:(hh + 1) * t] = qt[r0:r0 + HEAD_DIM, c * t:(c + 1) * t]

    nat = _dot(hb, wnat_ref[...])
    lane = lax.broadcasted_iota(jnp.int32, (tm, LANES), 1)
    ones = ((lane == HEAD_DIM) | (lane == HEAD_DIM + 1)).astype(F32)
    off_ks = 2 * KV_WIDTH
    off_kw = off_ks + N_GROUPS * LANES
    for g in range(N_GROUPS):
        kc_ref[0, g] = nat[:, g * HEAD_DIM:(g + 1) * HEAD_DIM]
        vc_ref[0, g] = nat[:, KV_WIDTH + g * HEAD_DIM:KV_WIDTH + (g + 1) * HEAD_DIM]
        ks_ref[0, g] = (nat[:, off_ks + g * LANES:off_ks + (g + 1) * LANES] + ones).astype(BF16)
        kw_ref[0, g] = nat[:, off_kw + g * LANES:off_kw + (g + 1) * LANES].astype(BF16)

    vt = _dot_nt(wvt_ref[...], hb).astype(BF16)
    for g in range(N_GROUPS):
        for c in range(tm // KEY_CHUNK):
            vst_ref[0, g, c, 0:HEAD_DIM, :] = vt[g * HEAD_DIM:(g + 1) * HEAD_DIM, c * KEY_CHUNK:(c + 1) * KEY_CHUNK]
            vst_ref[0, g, c, HEAD_DIM:V_ROWS, :] = _ones_row_tail(KEY_CHUNK)
        for c in range(tm // t):
            vwt_ref[0, g, c, 0:HEAD_DIM, :] = vt[KV_WIDTH + g * HEAD_DIM:KV_WIDTH + (g + 1) * HEAD_DIM,
                                                 c * t:(c + 1) * t]
            vwt_ref[0, g, c, HEAD_DIM:V_ROWS, :] = _ones_row_tail(t)

    gl = _dot_nt(wgt_ref[...], hb) + bg_ref[...]
    gates = 1.0 / (1.0 + jnp.exp(-gl))
    assert tm == t
    gt_ref[...] = jnp.zeros_like(gt_ref)
    for br in range(3):
        for g in range(N_GROUPS):
            for hh in range(HEADS_PER_GROUP):
                row = br * N_HEADS + g * HEADS_PER_GROUP + hh
                gt_ref[0, g, 0, br:br + 1, hh * t:(hh + 1) * t] = gates[row:row + 1, :]


def _nsa_proj(x, sc, sh, w_in, b_gate):
    B, S, _ = x.shape
    tm = ROW_TILE
    t = NSA_T
    scale = LOG2E * HEAD_DIM ** -0.5
    cuts = [D_MODEL + n * KV_WIDTH for n in range(7)]
    wq, wkc, wvc, wks, wvs, wkw, wvw, wg = jnp.split(w_in, cuts, axis=1)
    wqt = (wq * scale).T.astype(BF16)
    padk = lambda w: jnp.pad(w.reshape(D_MODEL, N_GROUPS, HEAD_DIM),
                             ((0, 0), (0, 0), (0, LANES - HEAD_DIM))).reshape(D_MODEL, N_GROUPS * LANES)
    wnat = jnp.concatenate([wkc, wvc, padk(wks), padk(wkw)], axis=1).astype(BF16)
    wvt = jnp.concatenate([wvs, wvw], axis=1).T.astype(BF16)
    n_gate = 3 * N_HEADS
    wgt = wg.T.astype(BF16)
    bg = jnp.broadcast_to(b_gate.reshape(n_gate, 1), (n_gate, tm))

    full = lambda a: pl.BlockSpec(a.shape, lambda b, i: (0,) * a.ndim)
    nat_spec = pl.BlockSpec((1, N_GROUPS, tm, HEAD_DIM), lambda b, i: (b, 0, i, 0))
    pad_spec = pl.BlockSpec((1, N_GROUPS, tm, LANES), lambda b, i: (b, 0, i, 0))
    vt_spec = lambda n, rows: pl.BlockSpec((1, N_GROUPS, tm // n, rows, n), lambda b, i: (b, 0, i, 0, 0))
    return pl.pallas_call(
        _nsa_proj_kernel, name="nsa_proj",
        grid=(B, S // tm),
        in_specs=[
            pl.BlockSpec((1, tm, D_MODEL), lambda b, i: (b, i, 0)),
            pl.BlockSpec((1, 1, D_MODEL), lambda b, i: (b, 0, 0)),
            pl.BlockSpec((1, 1, D_MODEL), lambda b, i: (b, 0, 0)),
            full(wqt), full(wnat), full(wvt), full(wgt), full(bg),
        ],
        out_specs=[
            pl.BlockSpec((1, N_GROUPS, tm // t, HEAD_DIM, HEADS_PER_GROUP * t), lambda b, i: (b, 0, i, 0, 0)),
            nat_spec, nat_spec, pad_spec, pad_spec, vt_spec(KEY_CHUNK, V_ROWS), vt_spec(t, V_ROWS),
            pl.BlockSpec((1, N_GROUPS, 1, 8, HEADS_PER_GROUP * t), lambda b, i: (b, 0, i, 0, 0)),
        ],
        out_shape=[
            jax.ShapeDtypeStruct((B, N_GROUPS, S // t, HEAD_DIM, HEADS_PER_GROUP * t), BF16),
            jax.ShapeDtypeStruct((B, N_GROUPS, S, HEAD_DIM), F32),
            jax.ShapeDtypeStruct((B, N_GROUPS, S, HEAD_DIM), F32),
            jax.ShapeDtypeStruct((B, N_GROUPS, S, LANES), BF16),
            jax.ShapeDtypeStruct((B, N_GROUPS, S, LANES), BF16),
            jax.ShapeDtypeStruct((B, N_GROUPS, S // KEY_CHUNK, V_ROWS, KEY_CHUNK), BF16),
            jax.ShapeDtypeStruct((B, N_GROUPS, S // t, V_ROWS, t), BF16),
            jax.ShapeDtypeStruct((B, N_GROUPS, S // t, 8, HEADS_PER_GROUP * t), F32),
        ],
        compiler_params=_params("arbitrary", "arbitrary"),
    )(x, sc, sh, wqt, wnat, wvt, wgt, bg)


def _compress_kernel(tk_ref, tv_ref, pos_ref, w1_ref, w2k_ref, w2vt_ref, kc_ref, vct_ref):
    n = tk_ref.shape[2]
    half = CMP_STRIDE * HEAD_DIM

    def hidden(t_ref, idx):
        t16 = t_ref[0, 0]
        xa = (t16 + pos_ref[idx, 0:1, :]).astype(BF16)
        xb = (t16 + pos_ref[idx, 1:2, :]).astype(BF16)
        first = _dot(xa, w1_ref[idx, :half, :])
        second = _dot(xb, w1_ref[idx, half:, :])
        pre = first + pltpu.roll(second, n - 1, 0)
        return (pre / (1.0 + jnp.exp(-pre))).astype(BF16)

    kc = _dot(hidden(tk_ref, 0), w2k_ref[...])
    lane = lax.broadcasted_iota(jnp.int32, kc.shape, 1)
    ones = ((lane == HEAD_DIM) | (lane == HEAD_DIM + 1)).astype(F32)
    kc_ref[0, 0] = (kc + ones).astype(BF16)
    vct = _dot_nt(w2vt_ref[...], hidden(tv_ref, 1)).astype(BF16)
    for r in range(n // CMP_ROWS):
        vct_ref[0, 0, r] = vct[:, r * CMP_ROWS:(r + 1) * CMP_ROWS]


def _compress(kc, vc, cmp_pos, cmp_w1, cmp_w2):
    B, G, S, _ = kc.shape
    n = S // CMP_STRIDE
    width = CMP_STRIDE * HEAD_DIM
    tk = kc.reshape(B, G, n, width)
    tv = vc.reshape(B, G, n, width)
    pos = cmp_pos.reshape(2, 2, width)
    w1 = cmp_w1.astype(BF16)
    w2k = jnp.pad(cmp_w2[0], ((0, 0), (0, LANES - HEAD_DIM))).astype(BF16)
    w2vt = cmp_w2[1].T.astype(BF16)
    full = lambda a: pl.BlockSpec(a.shape, lambda b, g: (0,) * a.ndim)
    t_spec = pl.BlockSpec((1, 1, n, width), lambda b, g: (b, g, 0, 0))
    return pl.pallas_call(
        _compress_kernel, name="nsa_compress",
        grid=(B, G),
        in_specs=[t_spec, t_spec, full(pos), full(w1), full(w2k), full(w2vt)],
        out_specs=[
            pl.BlockSpec((1, 1, n, LANES), lambda b, g: (b, g, 0, 0)),
            pl.BlockSpec((1, 1, n // CMP_ROWS, HEAD_DIM, CMP_ROWS), lambda b, g: (b, g, 0, 0, 0)),
        ],
        out_shape=[
            jax.ShapeDtypeStruct((B, G, n, LANES), BF16),
            jax.ShapeDtypeStruct((B, G, n // CMP_ROWS, HEAD_DIM, CMP_ROWS), BF16),
        ],
        compiler_params=_params("arbitrary", "arbitrary"),
    )(tk, tv, pos, w1, w2k, w2vt)


def _bias_tiles_kernel(rb_ref, sel_ref, win_ref):
    hd = pl.program_id(0)
    far = rb_ref[FAR_BUCKET, hd]

    def dist(n, back):
        key = lax.broadcasted_iota(jnp.int32, (n, n), 0)
        qry = lax.broadcasted_iota(jnp.int32, (n, n), 1)
        return qry - key + back * n

    kc = sel_ref.shape[2]
    d_prev, d_diag = dist(kc, 1), dist(kc, 0)
    sel_ref[0, 0] = LOG2E * (_bucket_bias(d_prev, rb_ref, hd) - far)
    sel_ref[1, 0] = jnp.where(d_diag >= 0, LOG2E * (_bucket_bias(d_diag, rb_ref, hd) - far), NEG)
    t = win_ref.shape[2]
    n_back = win_ref.shape[0] - 1
    for r in range(n_back + 1):
        d = dist(t, n_back - r)
        win_ref[r, 0] = jnp.where((d >= 0) & (d < WINDOW), LOG2E * _bucket_bias(d, rb_ref, hd), NEG)


def _bias_tiles(rel_bias):
    t = NSA_T
    n_win = WINDOW // t + 1
    assert WINDOW % t == 0
    spec = lambda n, m: pl.BlockSpec((n, 1, m, m), lambda h: (0, h, 0, 0))
    return pl.pallas_call(
        _bias_tiles_kernel, name="rel_bias_tiles",
        grid=(N_HEADS,),
        in_specs=[pl.BlockSpec(memory_space=pltpu.SMEM)],
        out_specs=[spec(2, KEY_CHUNK), spec(n_win, t)],
        out_shape=[
            jax.ShapeDtypeStruct((2, N_HEADS, KEY_CHUNK, KEY_CHUNK), F32),
            jax.ShapeDtypeStruct((n_win, N_HEADS, t, t), F32),
        ],
        compiler_params=_params("arbitrary"),
    )(rel_bias)


def _far_bias_rows(rb_ref, g, width):
    t = width // HEADS_PER_GROUP
    lane = lax.broadcasted_iota(jnp.int32, (HEAD_DIM, width), 1)
    row = lax.broadcasted_iota(jnp.int32, (HEAD_DIM, width), 0)
    far = jnp.zeros((HEAD_DIM, width), F32)
    for hh in range(HEADS_PER_GROUP):
        far = jnp.where(lane >= hh * t, LOG2E * rb_ref[FAR_BUCKET, g * HEADS_PER_GROUP + hh], far)
    hi = far.astype(BF16).astype(F32)
    return jnp.where(row == 0, hi, jnp.where(row == 1, far - hi, 0.0)).astype(BF16)


def _nsa_cmp_kernel(rb_ref, qt_ref, kc_ref, vct_ref, at_ref, oc_ref, sb_ref, qp_ref, s_ref, imp_ref):
    g = pl.program_id(1)
    i = pl.program_id(2)
    width = qt_ref.shape[4]
    t = width // HEADS_PER_GROUP
    n = kc_ref.shape[2]
    nb = at_ref.shape[1]
    t0 = i * t
    assert t & (t - 1) == 0 and n % CMP_ROWS == 0

    n_vis = (t0 + t - CMP_LEN) // CMP_STRIDE // CMP_ROWS + 1
    n_full = jnp.maximum((t0 - (CMP_LEN - 1)) // CMP_STRIDE + 1, 0) // CMP_ROWS
    rows_of = lambda r: pl.ds(pl.multiple_of(r * CMP_ROWS, CMP_ROWS), CMP_ROWS)

    def valid(r):
        blk = r * CMP_ROWS + lax.broadcasted_iota(jnp.int32, (CMP_ROWS, width), 0)
        qry = t0 + (lax.broadcasted_iota(jnp.int32, (CMP_ROWS, width), 1) & (t - 1))
        return qry >= blk * CMP_STRIDE + CMP_LEN - 1

    qp_ref[0:HEAD_DIM, :] = qt_ref[0, 0, 0]
    qp_ref[HEAD_DIM:LANES, :] = _far_bias_rows(rb_ref, g, width)

    def score_chunk(r, carry):
        s_ref[rows_of(r), :] = _dot(kc_ref[0, 0, rows_of(r), :], qp_ref[...])
        return carry

    lax.fori_loop(0, n_vis, score_chunk, 0)

    band = t // CMP_STRIDE + 16
    assert (FAR_DIST + CMP_LEN - 1) <= 16 * CMP_STRIDE and band <= n
    r0 = pl.multiple_of(jnp.clip(t0 // CMP_STRIDE - 16, 0, n - band), 8)
    blk_r = r0 + lax.broadcasted_iota(jnp.int32, (band, t), 0)
    qry_r = t0 + lax.broadcasted_iota(jnp.int32, (band, t), 1)
    dist_r = qry_r - (blk_r * CMP_STRIDE + CMP_LEN - 1)
    for hh in range(HEADS_PER_GROUP):
        hd = g * HEADS_PER_GROUP + hh
        corr = LOG2E * (_bucket_bias(dist_r, rb_ref, hd) - rb_ref[FAR_BUCKET, hd])
        s_ref[pl.ds(r0, band), hh * t:(hh + 1) * t] += corr

    def col_max(masked):
        def body(r, m):
            s = s_ref[rows_of(r), :]
            if masked:
                s = jnp.where(valid(r), s, NEG)
            return jnp.maximum(m, jnp.max(s, axis=0, keepdims=True))
        return body

    m = lax.fori_loop(0, n_full, col_max(False), jnp.full((1, width), NEG, F32))
    m = lax.fori_loop(n_full, n_vis, col_max(True), m)

    def exp_sum(masked):
        def body(r, l):
            p = jnp.exp2(s_ref[rows_of(r), :] - m)
            if masked:
                p = jnp.where(valid(r), p, 0.0)
            s_ref[rows_of(r), :] = p
            return l + jnp.sum(p, axis=0, keepdims=True)
        return body

    l = lax.fori_loop(0, n_full, exp_sum(False), jnp.zeros((1, width), F32))
    l = lax.fori_loop(n_full, n_vis, exp_sum(True), l)
    inv = 1.0 / jnp.maximum(l, TINY)

    oc_ref[0, 0, 0] = jnp.zeros((HEAD_DIM, width), F32)
    imp_ref[...] = jnp.zeros_like(imp_ref)

    def finish(r, carry):
        p = s_ref[rows_of(r), :] * inv
        oc_ref[0, 0, 0] += _dot(vct_ref[0, 0, r], p.astype(BF16))
        imp = p[:, 0:t]
        for hh in range(1, HEADS_PER_GROUP):
            imp = imp + p[:, hh * t:(hh + 1) * t]
        hi, mid, lo = _split3(imp)
        a = at_ref[r]
        imp_ref[...] += _dot(a, hi) + _dot(a, mid) + _dot(a, lo)
        return carry

    lax.fori_loop(0, n_vis, finish, 0)

    sblk = lax.broadcasted_iota(jnp.int32, (nb, t), 0)
    cur = (t0 + lax.broadcasted_iota(jnp.int32, (nb, t), 1)) // SEL_BLOCK
    forced = (sblk == 0) | (sblk == cur) | (sblk == cur - 1)
    is_cand = (sblk >= 1) & (sblk <= cur - 2)
    cand = jnp.where(is_cand, imp_ref[...], -1.0)
    sblk_f = sblk.astype(F32)
    for _ in range(SEL_TOPK - N_FORCED):
        best = jnp.max(cand, axis=0, keepdims=True)
        first = jnp.min(jnp.where(cand == best, sblk_f, float(nb)), axis=0, keepdims=True)
        cand = jnp.where(sblk_f == first, -1.0, cand)
    chosen = forced | (is_cand & (cand < 0.0))
    sb_ref[0, 0, 0] = jnp.where(chosen, 0.0, NEG).astype(BF16)


def _nsa_cmp(rel_bias, qt, kcmp, vcmp_t, n_sel):
    B, G, nq, _, width = qt.shape
    t = width // HEADS_PER_GROUP
    n = kcmp.shape[2]
    n_cmp = n - 1
    nb = -(-n_sel // LANES) * LANES
    R = SEL_BLOCK // CMP_STRIDE
    at = np.zeros((nb, n), np.float32)
    for j in range(n_sel):
        lo, hi = max(R * j - 1, 0), min(R * j + R - 1, n_cmp - 1)
        at[j, lo:hi + 1] = 1.0
    at = jnp.asarray(at.reshape(nb, n // CMP_ROWS, CMP_ROWS).transpose(1, 0, 2), BF16)
    return pl.pallas_call(
        _nsa_cmp_kernel, name="nsa_cmp_topk",
        grid=(B, G, nq),
        in_specs=[
            pl.BlockSpec(memory_space=pltpu.SMEM),
            pl.BlockSpec((1, 1, 1, HEAD_DIM, width), lambda b, g, i: (b, g, i, 0, 0)),
            pl.BlockSpec((1, 1, n, LANES), lambda b, g, i: (b, g, 0, 0)),
            pl.BlockSpec((1, 1, n // CMP_ROWS, HEAD_DIM, CMP_ROWS), lambda b, g, i: (b, g, 0, 0, 0)),
            pl.BlockSpec((n // CMP_ROWS, nb, CMP_ROWS), lambda b, g, i: (0, 0, 0)),
        ],
        out_specs=[
            pl.BlockSpec((1, 1, 1, HEAD_DIM, width), lambda b, g, i: (b, g, i, 0, 0)),
            pl.BlockSpec((1, 1, 1, nb, t), lambda b, g, i: (b, g, i, 0, 0)),
        ],
        out_shape=[
            jax.ShapeDtypeStruct((B, G, nq, HEAD_DIM, width), F32),
            jax.ShapeDtypeStruct((B, G, nq, nb, t), BF16),
        ],
        scratch_shapes=[pltpu.VMEM((LANES, width), BF16), pltpu.VMEM((n, width), F32),
                        pltpu.VMEM((nb, t), F32)],
        compiler_params=_params("arbitrary", "arbitrary", "arbitrary"),
    )(rel_bias, qt, kcmp, vcmp_t, at)


def _nsa_win_kernel(qt_ref, *refs):
    n_win = (len(refs) - 5) // 2
    k_refs, v_refs = refs[:n_win], refs[n_win:2 * n_win]
    wb_ref, ow_ref, qp_ref, s_ref, p_ref = refs[2 * n_win:]
    i = pl.program_id(2)
    width = qt_ref.shape[4]
    t = width // HEADS_PER_GROUP
    qt = qt_ref[0, 0, 0]
    qp_ref[0:HEAD_DIM, :] = qt
    qp_ref[HEAD_DIM:LANES, :] = jnp.zeros_like(qt)
    m = None
    for c in range(n_win):
        k = k_refs[c][0, 0]
        missing = jnp.where(i - (n_win - 1 - c) < 0, NEG, 0.0)
        col_max = []
        for blk in range(width // QUERY_SUB):
            cols = slice(blk * QUERY_SUB, (blk + 1) * QUERY_SUB)
            hh, off = divmod(blk * QUERY_SUB, t)
            part = _dot(k, qp_ref[:, cols]) + wb_ref[c, hh, :, off:off + QUERY_SUB] + missing
            s_ref[c, :, cols] = part
            col_max.append(jnp.max(part, axis=0, keepdims=True))
        col_max = jnp.concatenate(col_max, axis=1)
        m = col_max if m is None else jnp.maximum(m, col_max)
    acc = jnp.zeros((V_ROWS, width), F32)
    for c in range(n_win):
        for blk in range(width // QUERY_SUB):
            cols = slice(blk * QUERY_SUB, (blk + 1) * QUERY_SUB)
            p_ref[c, :, cols] = jnp.exp2((s_ref[c, :, cols] - m[:, cols]).astype(BF16))
        acc = acc + _dot(v_refs[c][0, 0, 0], p_ref[c])
    ow_ref[0, 0, 0] = acc[:HEAD_DIM] / acc[HEAD_DIM:HEAD_DIM + 1]


def _nsa_window(qt, kw, vwt, win_bias):
    B, G, nq, _, width = qt.shape
    t = width // HEADS_PER_GROUP
    n_win = win_bias.shape[0]
    backs = list(range(n_win - 1, -1, -1))
    k_spec = lambda back: pl.BlockSpec((1, 1, t, LANES), lambda b, g, i: (b, g, jnp.maximum(i - back, 0), 0))
    v_spec = lambda back: pl.BlockSpec((1, 1, 1, V_ROWS, t),
                                       lambda b, g, i: (b, g, jnp.maximum(i - back, 0), 0, 0))
    return pl.pallas_call(
        _nsa_win_kernel, name="nsa_window",
        grid=(B, G, nq),
        in_specs=[pl.BlockSpec((1, 1, 1, HEAD_DIM, width), lambda b, g, i: (b, g, i, 0, 0))]
        + [k_spec(back) for back in backs] + [v_spec(back) for back in backs]
        + [pl.BlockSpec((n_win, HEADS_PER_GROUP, t, t), lambda b, g, i: (0, g, 0, 0))],
        out_specs=pl.BlockSpec((1, 1, 1, HEAD_DIM, width), lambda b, g, i: (b, g, i, 0, 0)),
        out_shape=jax.ShapeDtypeStruct((B, G, nq, HEAD_DIM, width), F32),
        scratch_shapes=[pltpu.VMEM((LANES, width), BF16),
                        pltpu.VMEM((n_win, t, width), F32), pltpu.VMEM((n_win, t, width), BF16)],
        compiler_params=_params("arbitrary", "arbitrary", "arbitrary"),
    )(qt, *([kw] * n_win), *([vwt] * n_win), win_bias)


def _nsa_sel_kernel(rb_ref, qt_ref, sb_ref, ks_ref, e_ref, vst_ref, cb_ref, oc_ref, ow_ref, gt_ref,
                    o_ref, qp_ref, s_buf, p_buf, acc_ref):
    g = pl.program_id(1)
    i = pl.program_id(2)
    width = qt_ref.shape[4]
    t = width // HEADS_PER_GROUP
    per_tile = t // KEY_CHUNK
    chunks_per_slab = LANES * SEL_BLOCK // KEY_CHUNK
    blocks_per_head = t // QUERY_SUB
    assert per_tile == 2 and t % QUERY_SUB == 0

    far_rows = _far_bias_rows(rb_ref, g, width)
    for slab in range(qp_ref.shape[0]):
        qp_ref[slab, 0:HEAD_DIM, :] = qt_ref[0, 0, 0]
        qp_ref[slab, HEAD_DIM:LANES, :] = far_rows
        sb = sb_ref[0, 0, 0, slab * LANES:(slab + 1) * LANES, :]
        qp_ref[slab, LANES:2 * LANES, :] = jnp.concatenate([sb] * HEADS_PER_GROUP, axis=1)

    def scores(c):
        c = jnp.maximum(c, 0)
        rows = pl.ds(pl.multiple_of(c * KEY_CHUNK, KEY_CHUNK), KEY_CHUNK)
        kp = jnp.concatenate([ks_ref[0, 0, rows, :], e_ref[rows, :]], axis=1)
        return lambda cols: _dot(kp, qp_ref[c // chunks_per_slab, :, cols])

    def values(c):
        return vst_ref[0, 0, jnp.maximum(c, 0)]

    def near_bias(r):
        def bias(c):
            hh, part = divmod(c, blocks_per_head)
            ahead = part * QUERY_SUB // KEY_CHUNK - (r - per_tile)
            if ahead == 0:
                return cb_ref[1, hh]
            if ahead == 1:
                return cb_ref[0, hh]
            if ahead < 0:
                return jnp.full((KEY_CHUNK, QUERY_SUB), NEG, F32)
            return None
        return bias

    tail = [(None, True), (None, True)] + [(near_bias(r), r < per_tile) for r in range(2 * per_tile)]
    o_sel = _flash_sweep(per_tile * jnp.minimum(i - 2, 0), jnp.maximum(i - 2, 0), scores, values, tail,
                         s_buf, p_buf, acc_ref)

    out = gt_ref[0, 0, 0, 0:1, :] * oc_ref[0, 0, 0] + gt_ref[0, 0, 0, 1:2, :] * o_sel \
        + gt_ref[0, 0, 0, 2:3, :] * ow_ref[0, 0, 0]
    for hh in range(HEADS_PER_GROUP):
        o_ref[0, hh * HEAD_DIM:(hh + 1) * HEAD_DIM, :] = out[:, hh * t:(hh + 1) * t].astype(o_ref.dtype)


def _nsa_select_combine(rel_bias, qt, sel_bias, ks, vst, sel_corr, oc, ow, gates):
    B, G, nq, _, width = qt.shape
    t = width // HEADS_PER_GROUP
    S = nq * t
    nb = sel_bias.shape[3]
    blocks = (np.arange(S) // SEL_BLOCK) % LANES
    onehot = jnp.asarray(blocks[:, None] == np.arange(LANES)[None, :], BF16)
    tile = pl.BlockSpec((1, 1, 1, HEAD_DIM, width), lambda b, g, i: (b, g, i, 0, 0))
    return pl.pallas_call(
        _nsa_sel_kernel, name="nsa_select",
        grid=(B, G, nq),
        in_specs=[
            pl.BlockSpec(memory_space=pltpu.SMEM),
            tile,
            pl.BlockSpec((1, 1, 1, nb, t), lambda b, g, i: (b, g, i, 0, 0)),
            pl.BlockSpec((1, 1, S, LANES), lambda b, g, i: (b, g, 0, 0)),
            pl.BlockSpec((S, LANES), lambda b, g, i: (0, 0)),
            pl.BlockSpec((1, 1, S // KEY_CHUNK, V_ROWS, KEY_CHUNK), lambda b, g, i: (b, g, 0, 0, 0)),
            pl.BlockSpec((2, HEADS_PER_GROUP, KEY_CHUNK, KEY_CHUNK), lambda b, g, i: (0, g, 0, 0)),
            tile, tile,
            pl.BlockSpec((1, 1, 1, 8, width), lambda b, g, i: (b, g, i, 0, 0)),
        ],
        out_specs=pl.BlockSpec((1, HEADS_PER_GROUP * HEAD_DIM, t), lambda b, g, i: (b, g, i)),
        out_shape=jax.ShapeDtypeStruct((B, D_MODEL, S), BF16),
        scratch_shapes=[pltpu.VMEM((nb // LANES, 2 * LANES, width), BF16),
                        pltpu.VMEM((2, KEY_CHUNK, width), F32), pltpu.VMEM((2, KEY_CHUNK, width), BF16),
                        pltpu.VMEM((V_ROWS, width), F32)],
        compiler_params=_params("arbitrary", "arbitrary", "arbitrary"),
    )(rel_bias, qt, sel_bias, ks, onehot, vst, sel_corr, oc, ow, gates)


def _nsa_attention_t(x, sc, sh, w_in, b_gate, cmp_pos, cmp_w1, cmp_w2, rel_bias):
    B, S, _ = x.shape
    t = NSA_T
    n_sel = S // SEL_BLOCK
    assert S % ROW_TILE == 0 and n_sel >= SEL_TOPK and S // CMP_STRIDE >= t // CMP_STRIDE + 16
    qt, kc, vc, ks, kw, vst, vwt, gt = _nsa_proj(x, sc, sh, w_in, b_gate)
    kcmp, vcmp_t = _compress(kc, vc, cmp_pos, cmp_w1, cmp_w2)
    sel_corr, win_bias = _bias_tiles(rel_bias)
    oc, sel_bias = _nsa_cmp(rel_bias, qt, kcmp, vcmp_t, n_sel)
    ow = _nsa_window(qt, kw, vwt, win_bias)
    return _nsa_select_combine(rel_bias, qt, sel_bias, ks, vst, sel_corr, oc, ow, gt)


def kernel(x, c, ada_w, ada_b, ln_g, ln_b, fox_w_in, fox_b_f, fox_w_out, nsa_w_in, nsa_b_gate,
           nsa_cmp_pos, nsa_cmp_w1, nsa_cmp_w2, nsa_w_out, rel_bias, ffn_w_in, ffn_w_out):
    B, S, _ = x.shape
    assert S % FOX_T == 0 and S % ROW_TILE == 0 and ROW_TILE == NSA_T
    mod = _modulation(c, ada_w, ada_b)
    for layer in range(DEPTH):
        sh_a, sc_a, g_a, sh_f, sc_f, g_f = [m.reshape(B, 1, D_MODEL) for m in jnp.split(mod[layer], 6, axis=-1)]
        j = layer // 2
        if layer % 2 == 0:
            kp, qt, vt = _fox_proj(x, sc_a, sh_a, fox_w_in[j], fox_b_f[j])
            attn_t = _fox_attention(qt, kp, vt)
            w_out = fox_w_out[j]
        else:
            attn_t = _nsa_attention_t(x, sc_a, sh_a, nsa_w_in[j], nsa_b_gate[j], nsa_cmp_pos[j],
                                      nsa_cmp_w1[j], nsa_cmp_w2[j], rel_bias)
            w_out = nsa_w_out[j]
        x = _outproj_ln(attn_t, w_out, x, g_a, ln_g[layer, 0], ln_b[layer, 0])
        x = _ffn_ln(x, sc_f, sh_f, g_f, ffn_w_in[layer], ffn_w_out[layer], ln_g[layer, 1], ln_b[layer, 1])
    return x
```

```python
import functools
import math

import numpy as np
import jax
import jax.numpy as jnp
from jax import lax
from jax.experimental import pallas as pl
from jax.experimental.pallas import tpu as pltpu

F32 = jnp.float32
BF16 = jnp.bfloat16
HIGHEST = lax.Precision.HIGHEST

D_MODEL = 1024
HEAD_DIM = 64
N_HEADS = 16
N_GROUPS = 4
HEADS_PER_GROUP = 4
KV_WIDTH = N_GROUPS * HEAD_DIM
CMP_LEN = 32
CMP_STRIDE = 16
CMP_HIDDEN = 128
SEL_BLOCK = 64
SEL_TOPK = 16
WINDOW = 512
REL_BUCKETS = 32
FF_HIDDEN = 2816
DEPTH = 2
DN_ALPHA = (2 * DEPTH) ** 0.25
LN_EPS = 1e-5
NEG = -1e30
TINY = 1e-30

LANES = 128
VMEM_LIMIT = 56 * 1024 * 1024

ROW_TILE = 512
FOX_T = 1024
NSA_T = 512
N_FORCED = 3

BUCKET_START = (0, 1, 2, 3, 4, 5, 6, 7, 8, 9, 10, 11, 12, 13, 14, 15,
                16, 19, 21, 24, 27, 31, 35, 40, 46, 52, 59, 67, 77, 87, 99, 113)
FAR_BUCKET = REL_BUCKETS - 1
FAR_DIST = BUCKET_START[FAR_BUCKET]

LOG2E = math.log2(math.e)
QUERY_SUB = 256
KEY_CHUNK = 256
PAIRS_PER_TRIP = 4
CMP_ROWS = 128
V_ROWS = HEAD_DIM + 16
NT_DIMS = (((1,), (1,)), ((), ()))


def _params(*sem):
    return pltpu.CompilerParams(dimension_semantics=sem, vmem_limit_bytes=VMEM_LIMIT)


def _dot(a, b, **kw):
    return jnp.dot(a, b, preferred_element_type=F32, **kw)


def _dot_nt(a, b):
    return lax.dot_general(a, b, NT_DIMS, preferred_element_type=F32)


def _split3(v):
    hi = v.astype(BF16)
    r = v - hi.astype(F32)
    mid = r.astype(BF16)
    lo = (r - mid.astype(F32)).astype(BF16)
    return hi, mid, lo


def _layer_norm(z, g, b):
    mu = jnp.mean(z, axis=-1, keepdims=True)
    zc = z - mu
    var = jnp.mean(zc * zc, axis=-1, keepdims=True)
    return zc * lax.rsqrt(var + LN_EPS) * g + b


def _bucket_bias(dist, rb_ref, head):
    bias = jnp.full(dist.shape, rb_ref[0, head], F32)
    for k in range(1, REL_BUCKETS):
        bias = jnp.where(dist >= BUCKET_START[k], rb_ref[k, head], bias)
    return bias


def _mod_kernel(c_ref, w_ref, b_ref, o_ref):
    c = c_ref[...]
    cs = c / (1.0 + jnp.exp(-c))
    o_ref[0] = _dot(cs, w_ref[0], precision=HIGHEST) + b_ref[0]


def _modulation(c, ada_w, ada_b):
    B = c.shape[0]
    depth, _, n = ada_w.shape
    rows = 8
    c_pad = jnp.pad(c, ((0, rows - B), (0, 0)))
    tn = 1536
    out = pl.pallas_call(
        _mod_kernel, name="adaln_mod",
        grid=(depth, n // tn),
        in_specs=[
            pl.BlockSpec((rows, D_MODEL), lambda l, j: (0, 0)),
            pl.BlockSpec((1, D_MODEL, tn), lambda l, j: (l, 0, j)),
            pl.BlockSpec((1, 1, tn), lambda l, j: (l, 0, j)),
        ],
        out_specs=pl.BlockSpec((1, rows, tn), lambda l, j: (l, 0, j)),
        out_shape=jax.ShapeDtypeStruct((depth, rows, n), F32),
        compiler_params=_params("arbitrary", "arbitrary"),
    )(c_pad, ada_w, ada_b.reshape(depth, 1, n))
    return out[:, :B]


def _fox_proj_kernel(x_ref, sc_ref, sh_ref, wk_ref, wqt_ref, wvt_ref, wfh_ref, wfl_ref,
                     bf_ref, place_ref, kp_ref, qt_ref, vt_ref, carry_ref):
    tm = x_ref.shape[1]

    @pl.when(pl.program_id(1) == 0)
    def _():
        carry_ref[...] = jnp.zeros_like(carry_ref)

    h = x_ref[0] * (1.0 + sc_ref[0]) + sh_ref[0]
    hb = h.astype(BF16)
    hl = (h - hb.astype(F32)).astype(BF16)

    f = _dot(hb, wfh_ref[...]) + _dot(hl, wfh_ref[...]) + _dot(hb, wfl_ref[...])
    z = f + bf_ref[...]
    logf = jnp.minimum(z, 0.0) - jnp.log(1.0 + jnp.exp(-jnp.abs(z)))
    row = lax.broadcasted_iota(jnp.int32, (tm, tm), 0)
    col = lax.broadcasted_iota(jnp.int32, (tm, tm), 1)
    lower = (col <= row).astype(F32)
    cum = _dot(lower, logf, precision=HIGHEST) + carry_ref[0:1, :]
    carry_ref[...] = jnp.broadcast_to(cum[tm - 1:tm, :], carry_ref.shape)

    hi, mid, lo = [p.astype(F32) for p in _split3(-LOG2E * cum)]
    lane = lax.broadcasted_iota(jnp.int32, (tm, LANES), 1)
    pieces = jnp.where(lane < N_HEADS, hi, jnp.where(lane < 2 * N_HEADS, mid, lo)).astype(BF16)
    kp = _dot(hb, wk_ref[...]) + _dot(pieces, place_ref[...])
    qt = _dot_nt(wqt_ref[...], hb).astype(BF16)
    qrow = lax.broadcasted_iota(jnp.int32, (LANES - HEAD_DIM, tm), 0)
    q_ones = (qrow < 3).astype(F32).astype(BF16)
    vt = _dot_nt(wvt_ref[...], hb)
    for hd in range(N_HEADS):
        kp_ref[0, hd] = kp[:, hd * LANES:(hd + 1) * LANES].astype(BF16)
        qt_ref[0, hd, 0:HEAD_DIM, :] = qt[hd * HEAD_DIM:(hd + 1) * HEAD_DIM, :]
        qt_ref[0, hd, HEAD_DIM:LANES, :] = q_ones
        for c in range(tm // KEY_CHUNK):
            vt_ref[0, hd, c, 0:HEAD_DIM, :] = vt[hd * HEAD_DIM:(hd + 1) * HEAD_DIM,
                                                 c * KEY_CHUNK:(c + 1) * KEY_CHUNK].astype(BF16)
            vt_ref[0, hd, c, HEAD_DIM:V_ROWS, :] = _ones_row_tail(KEY_CHUNK)


def _fox_proj(x, sc, sh, w_in, b_f):
    B, S, _ = x.shape
    tm = ROW_TILE
    scale = LOG2E * HEAD_DIM ** -0.5
    wq = (w_in[:, :D_MODEL] * scale).reshape(D_MODEL, N_HEADS, HEAD_DIM)
    wk = w_in[:, D_MODEL:2 * D_MODEL].reshape(D_MODEL, N_HEADS, HEAD_DIM)
    wv = w_in[:, 2 * D_MODEL:3 * D_MODEL]
    wf = w_in[:, 3 * D_MODEL:]
    pad = ((0, 0), (0, 0), (0, LANES - HEAD_DIM))
    wk_p = jnp.pad(wk, pad).reshape(D_MODEL, N_HEADS * LANES).astype(BF16)
    wqt_p = wq.reshape(D_MODEL, N_HEADS * HEAD_DIM).T.astype(BF16)
    wvt = wv.T.astype(BF16)
    wf_rep = jnp.pad(jnp.tile(wf, (1, 3)), ((0, 0), (0, LANES - 3 * N_HEADS)))
    wf_hi = wf_rep.astype(BF16)
    wf_lo = (wf_rep - wf_hi.astype(F32)).astype(BF16)
    bf_rep = jnp.pad(jnp.tile(b_f, 3), (0, LANES - 3 * N_HEADS)).reshape(1, LANES)
    place = np.zeros((LANES, N_HEADS * LANES), np.float32)
    for r in range(3):
        for hd in range(N_HEADS):
            place[r * N_HEADS + hd, hd * LANES + HEAD_DIM + r] = 1.0
    place = jnp.asarray(place, BF16)

    full = lambda a: pl.BlockSpec(a.shape, lambda b, i: (0,) * a.ndim)
    return pl.pallas_call(
        _fox_proj_kernel, name="fox_proj",
        grid=(B, S // tm),
        in_specs=[
            pl.BlockSpec((1, tm, D_MODEL), lambda b, i: (b, i, 0)),
            pl.BlockSpec((1, 1, D_MODEL), lambda b, i: (b, 0, 0)),
            pl.BlockSpec((1, 1, D_MODEL), lambda b, i: (b, 0, 0)),
            full(wk_p), full(wqt_p), full(wvt), full(wf_hi), full(wf_lo), full(bf_rep), full(place),
        ],
        out_specs=[
            pl.BlockSpec((1, N_HEADS, tm, LANES), lambda b, i: (b, 0, i, 0)),
            pl.BlockSpec((1, N_HEADS, LANES, tm), lambda b, i: (b, 0, 0, i)),
            pl.BlockSpec((1, N_HEADS, tm // KEY_CHUNK, V_ROWS, KEY_CHUNK), lambda b, i: (b, 0, i, 0, 0)),
        ],
        out_shape=[
            jax.ShapeDtypeStruct((B, N_HEADS, S, LANES), BF16),
            jax.ShapeDtypeStruct((B, N_HEADS, LANES, S), BF16),
            jax.ShapeDtypeStruct((B, N_HEADS, S // KEY_CHUNK, V_ROWS, KEY_CHUNK), BF16),
        ],
        scratch_shapes=[pltpu.VMEM((8, LANES), F32)],
        compiler_params=_params("arbitrary", "arbitrary"),
    )(x, sc, sh, wk_p, wqt_p, wvt, wf_hi, wf_lo, bf_rep, place)


def _ones_row_tail(width):
    row = lax.broadcasted_iota(jnp.int32, (V_ROWS - HEAD_DIM, width), 0)
    return (row == 0).astype(F32).astype(BF16)


def _flash_sweep(first, n_far_pairs, scores, values, tail, s_buf, p_buf, acc_ref, far_pairs_even=False):
    width = acc_ref.shape[1]
    n_blocks = width // QUERY_SUB
    assert len(tail) % 2 == 0 and width % QUERY_SUB == 0
    assert tail[0][0] is None and tail[1][0] is None
    cat = lambda parts: jnp.concatenate(parts, axis=1)

    def fetch(c, par, bias, may_be_missing):
        block_scores = scores(c)
        missing = jnp.where(c < 0, NEG, 0.0) if may_be_missing else None
        col_max = []
        for blk in range(n_blocks):
            cols = slice(blk * QUERY_SUB, (blk + 1) * QUERY_SUB)
            part = block_scores(cols)
            extra = None if bias is None else bias(blk)
            if extra is not None:
                part = part + extra
            if missing is not None:
                part = part + missing
            s_buf[par, :, cols] = part
            col_max.append(jnp.max(part, axis=0, keepdims=True))
        return cat(col_max)

    def step(j, par, stats, nxt=None):
        m, corr1, corr2, cmax = stats
        acc_ref[...] = acc_ref[...] * corr2 + _dot(values(j - 2), p_buf[par])
        m_new = jnp.maximum(m, cmax[par])
        for blk in range(n_blocks):
            cols = slice(blk * QUERY_SUB, (blk + 1) * QUERY_SUB)
            p_buf[par, :, cols] = jnp.exp2((s_buf[par, :, cols] - m_new[:, cols]).astype(BF16))
        if nxt is not None:
            cmax = tuple(fetch(j + 2, par, *nxt) if q == par else cmax[q] for q in range(2))
        return m_new, jnp.exp2(m - m_new), corr1, cmax

    far = (None, False)

    def pairs(j, n, stats):
        for k in range(n):
            stats = step(j + 2 * k + 1, 1, step(j + 2 * k, 0, stats, far), far)
        return stats

    p_buf[...] = jnp.zeros_like(p_buf)
    acc_ref[...] = jnp.zeros_like(acc_ref)
    ones = jnp.ones((1, width), F32)
    cmax = (fetch(first, 0, None, True), fetch(first + 1, 1, None, True))
    stats = (jnp.full((1, width), NEG, F32), ones, ones, cmax)
    start = first
    for n in (1, 2):
        if n == 1 and far_pairs_even:
            continue
        group = (n_far_pairs // n) % 2
        stats = lax.cond(group == 1, lambda st, j=start, n=n: pairs(j, n, st), lambda st: st, stats)
        start = start + 2 * n * group
    stats = lax.fori_loop(0, n_far_pairs // PAIRS_PER_TRIP,
                          lambda jj, st: pairs(start + 2 * PAIRS_PER_TRIP * jj, PAIRS_PER_TRIP, st), stats)
    last = first + 2 * n_far_pairs + len(tail)
    for r in range(len(tail)):
        stats = step(last - len(tail) + r, r % 2, stats, tail[r + 2] if r + 2 < len(tail) else None)
    _, corr1, corr2, _ = stats
    acc = acc_ref[...] * corr2 + _dot(values(last - 2), p_buf[0])
    acc = acc * corr1 + _dot(values(last - 1), p_buf[1])
    return acc[:HEAD_DIM] / acc[HEAD_DIM:HEAD_DIM + 1]


def _fox_attn_kernel(qt_ref, kp_ref, vt_ref, o_ref, s_buf, p_buf, acc_ref):
    i = pl.program_id(2)
    t = qt_ref.shape[3]
    per_tile = t // KEY_CHUNK

    def scores(c):
        rows = pl.ds(pl.multiple_of(jnp.maximum(c, 0) * KEY_CHUNK, KEY_CHUNK), KEY_CHUNK)
        k = kp_ref[0, 0, rows, :]
        return lambda cols: _dot(k, qt_ref[0, 0, :, cols])

    def values(c):
        return vt_ref[0, 0, jnp.maximum(c, 0)]

    key = lax.broadcasted_iota(jnp.int32, (KEY_CHUNK, QUERY_SUB), 0)
    qry = lax.broadcasted_iota(jnp.int32, (KEY_CHUNK, QUERY_SUB), 1)
    causal = lambda r: (lambda c: jnp.where(key + r * KEY_CHUNK <= qry + c * QUERY_SUB, 0.0, NEG))
    assert per_tile % 4 == 0
    tail = [(None, True)] * per_tile + [(causal(r), False) for r in range(per_tile)]
    out = _flash_sweep(per_tile * jnp.minimum(i - 1, 0), (per_tile // 2) * jnp.maximum(i - 1, 0),
                       scores, values, tail, s_buf, p_buf, acc_ref, far_pairs_even=True)
    o_ref[0] = out.astype(o_ref.dtype)


def _fox_attention(qt, kp, vt):
    B, H, S, _ = kp.shape
    t = FOX_T
    return pl.pallas_call(
        _fox_attn_kernel, name="fox_attn",
        grid=(B, H, S // t),
        in_specs=[
            pl.BlockSpec((1, 1, LANES, t), lambda b, h, i: (b, h, 0, i)),
            pl.BlockSpec((1, 1, S, LANES), lambda b, h, i: (b, h, 0, 0)),
            pl.BlockSpec((1, 1, S // KEY_CHUNK, V_ROWS, KEY_CHUNK), lambda b, h, i: (b, h, 0, 0, 0)),
        ],
        out_specs=pl.BlockSpec((1, HEAD_DIM, t), lambda b, h, i: (b, h, i)),
        out_shape=jax.ShapeDtypeStruct((B, H * HEAD_DIM, S), BF16),
        scratch_shapes=[pltpu.VMEM((2, KEY_CHUNK, t), F32), pltpu.VMEM((2, KEY_CHUNK, t), BF16),
                        pltpu.VMEM((V_ROWS, t), F32)],
        compiler_params=_params("arbitrary", "arbitrary", "arbitrary"),
    )(qt, kp, vt)


def _outproj_kernel(at_ref, w_ref, x_ref, gate_ref, g_ref, b_ref, o_ref):
    y = lax.dot_general(at_ref[0], w_ref[...], (((0,), (0,)), ((), ())), preferred_element_type=F32)
    z = DN_ALPHA * x_ref[0] + (1.0 + gate_ref[0]) * y
    o_ref[0] = _layer_norm(z, g_ref[...], b_ref[...])


def _outproj_ln(attn_t, w_out, x, gate, ln_g, ln_b):
    B, S, _ = x.shape
    tm = ROW_TILE
    vec = pl.BlockSpec((1, D_MODEL), lambda b, i: (0, 0))
    return pl.pallas_call(
        _outproj_kernel, name="outproj_ln",
        grid=(B, S // tm),
        in_specs=[
            pl.BlockSpec((1, D_MODEL, tm), lambda b, i: (b, 0, i)),
            pl.BlockSpec((D_MODEL, D_MODEL), lambda b, i: (0, 0)),
            pl.BlockSpec((1, tm, D_MODEL), lambda b, i: (b, i, 0)),
            pl.BlockSpec((1, 1, D_MODEL), lambda b, i: (b, 0, 0)),
            vec, vec,
        ],
        out_specs=pl.BlockSpec((1, tm, D_MODEL), lambda b, i: (b, i, 0)),
        out_shape=jax.ShapeDtypeStruct((B, S, D_MODEL), F32),
        compiler_params=_params("arbitrary", "arbitrary"),
    )(attn_t, w_out.astype(BF16), x, gate, ln_g.reshape(1, -1), ln_b.reshape(1, -1))


def _ffn_kernel(x_ref, sc_ref, sh_ref, gate_ref, wa_ref, wb_ref, wo_ref, g_ref, b_ref, o_ref):
    x = x_ref[0]
    hb = (x * (1.0 + sc_ref[0]) + sh_ref[0]).astype(BF16)
    a = _dot(hb, wa_ref[...])
    b = _dot(hb, wb_ref[...])
    u = (a / (1.0 + jnp.exp(-a)) * b).astype(BF16)
    y = _dot(u, wo_ref[...])
    z = DN_ALPHA * x + (1.0 + gate_ref[0]) * y
    o_ref[0] = _layer_norm(z, g_ref[...], b_ref[...])


def _ffn_ln(x, sc, sh, gate, w_in, w_out, ln_g, ln_b):
    B, S, _ = x.shape
    tm = ROW_TILE
    w_in = w_in.astype(BF16)
    mod = pl.BlockSpec((1, 1, D_MODEL), lambda b, i: (b, 0, 0))
    vec = pl.BlockSpec((1, D_MODEL), lambda b, i: (0, 0))
    resident = lambda shape, index: pl.BlockSpec(shape, index, pipeline_mode=pl.Buffered(1))
    return pl.pallas_call(
        _ffn_kernel, name="ffn_ln",
        grid=(B, S // tm),
        in_specs=[
            pl.BlockSpec((1, tm, D_MODEL), lambda b, i: (b, i, 0)),
            mod, mod, mod,
            resident((D_MODEL, FF_HIDDEN), lambda b, i: (0, 0)),
            resident((D_MODEL, FF_HIDDEN), lambda b, i: (0, 1)),
            resident((FF_HIDDEN, D_MODEL), lambda b, i: (0, 0)),
            vec, vec,
        ],
        out_specs=pl.BlockSpec((1, tm, D_MODEL), lambda b, i: (b, i, 0)),
        out_shape=jax.ShapeDtypeStruct((B, S, D_MODEL), F32),
        compiler_params=_params("arbitrary", "arbitrary"),
    )(x, sc, sh, gate, w_in, w_in, w_out.astype(BF16), ln_g.reshape(1, -1), ln_b.reshape(1, -1))


def _nsa_proj_kernel(x_ref, sc_ref, sh_ref, wqt_ref, wnat_ref, wvt_ref, wgt_ref, bg_ref,
                     qt_ref, kc_ref, vc_ref, ks_ref, kw_ref, vst_ref, vwt_ref, gt_ref):
    tm = x_ref.shape[1]
    t = NSA_T
    hb = (x_ref[0] * (1.0 + sc_ref[0]) + sh_ref[0]).astype(BF16)

    qt = _dot_nt(wqt_ref[...], hb).astype(BF16)
    for g in range(N_GROUPS):
        for hh in range(HEADS_PER_GROUP):
            r0 = (g * HEADS_PER_GROUP + hh) * HEAD_DIM
            for c in range(tm // t):
                qt_ref[0, g, c, :, hh * t:(hh + 1) * t] = qt[r0:r0 + HEAD_DIM, c * t:(c + 1) * t]

    nat = _dot(hb, wnat_ref[...])
    lane = lax.broadcasted_iota(jnp.int32, (tm, LANES), 1)
    ones = ((lane == HEAD_DIM) | (lane == HEAD_DIM + 1)).astype(F32)
    off_ks = 2 * KV_WIDTH
    off_kw = off_ks + N_GROUPS * LANES
    for g in range(N_GROUPS):
        kc_ref[0, g] = nat[:, g * HEAD_DIM:(g + 1) * HEAD_DIM].astype(BF16)
        vc_ref[0, g] = nat[:, KV_WIDTH + g * HEAD_DIM:KV_WIDTH + (g + 1) * HEAD_DIM].astype(BF16)
        ks_ref[0, g] = (nat[:, off_ks + g * LANES:off_ks + (g + 1) * LANES] + ones).astype(BF16)
        kw_ref[0, g] = nat[:, off_kw + g * LANES:off_kw + (g + 1) * LANES].astype(BF16)

    vt = _dot_nt(wvt_ref[...], hb).astype(BF16)
    for g in range(N_GROUPS):
        for c in range(tm // KEY_CHUNK):
            vst_ref[0, g, c, 0:HEAD_DIM, :] = vt[g * HEAD_DIM:(g + 1) * HEAD_DIM, c * KEY_CHUNK:(c + 1) * KEY_CHUNK]
            vst_ref[0, g, c, HEAD_DIM:V_ROWS, :] = _ones_row_tail(KEY_CHUNK)
        for c in range(tm // t):
            vwt_ref[0, g, c, 0:HEAD_DIM, :] = vt[KV_WIDTH + g * HEAD_DIM:KV_WIDTH + (g + 1) * HEAD_DIM,
                                                 c * t:(c + 1) * t]
            vwt_ref[0, g, c, HEAD_DIM:V_ROWS, :] = _ones_row_tail(t)

    gl = _dot_nt(wgt_ref[...], hb) + bg_ref[...]
    gates = 1.0 / (1.0 + jnp.exp(-gl))
    assert tm == t
    gt_ref[...] = jnp.zeros_like(gt_ref)
    for br in range(3):
        for g in range(N_GROUPS):
            for hh in range(HEADS_PER_GROUP):
                row = br * N_HEADS + g * HEADS_PER_GROUP + hh
                gt_ref[0, g, 0, br:br + 1, hh * t:(hh + 1) * t] = gates[row:row + 1, :]


def _nsa_proj(x, sc, sh, w_in, b_gate):
    B, S, _ = x.shape
    tm = ROW_TILE
    t = NSA_T
    scale = LOG2E * HEAD_DIM ** -0.5
    cuts = [D_MODEL + n * KV_WIDTH for n in range(7)]
    wq, wkc, wvc, wks, wvs, wkw, wvw, wg = jnp.split(w_in, cuts, axis=1)
    wqt = (wq * scale).T.astype(BF16)
    padk = lambda w: jnp.pad(w.reshape(D_MODEL, N_GROUPS, HEAD_DIM),
                             ((0, 0), (0, 0), (0, LANES - HEAD_DIM))).reshape(D_MODEL, N_GROUPS * LANES)
    wnat = jnp.concatenate([wkc, wvc, padk(wks), padk(wkw)], axis=1).astype(BF16)
    wvt = jnp.concatenate([wvs, wvw], axis=1).T.astype(BF16)
    n_gate = 3 * N_HEADS
    wgt = wg.T.astype(BF16)
    bg = jnp.broadcast_to(b_gate.reshape(n_gate, 1), (n_gate, tm))

    full = lambda a: pl.BlockSpec(a.shape, lambda b, i: (0,) * a.ndim)
    nat_spec = pl.BlockSpec((1, N_GROUPS, tm, HEAD_DIM), lambda b, i: (b, 0, i, 0))
    pad_spec = pl.BlockSpec((1, N_GROUPS, tm, LANES), lambda b, i: (b, 0, i, 0))
    vt_spec = lambda n, rows: pl.BlockSpec((1, N_GROUPS, tm // n, rows, n), lambda b, i: (b, 0, i, 0, 0))
    return pl.pallas_call(
        _nsa_proj_kernel, name="nsa_proj",
        grid=(B, S // tm),
        in_specs=[
            pl.BlockSpec((1, tm, D_MODEL), lambda b, i: (b, i, 0)),
            pl.BlockSpec((1, 1, D_MODEL), lambda b, i: (b, 0, 0)),
            pl.BlockSpec((1, 1, D_MODEL), lambda b, i: (b, 0, 0)),
            full(wqt), full(wnat), full(wvt), full(wgt), full(bg),
        ],
        out_specs=[
            pl.BlockSpec((1, N_GROUPS, tm // t, HEAD_DIM, HEADS_PER_GROUP * t), lambda b, i: (b, 0, i, 0, 0)),
            nat_spec, nat_spec, pad_spec, pad_spec, vt_spec(KEY_CHUNK, V_ROWS), vt_spec(t, V_ROWS),
            pl.BlockSpec((1, N_GROUPS, 1, 8, HEADS_PER_GROUP * t), lambda b, i: (b, 0, i, 0, 0)),
        ],
        out_shape=[
            jax.ShapeDtypeStruct((B, N_GROUPS, S // t, HEAD_DIM, HEADS_PER_GROUP * t), BF16),
            jax.ShapeDtypeStruct((B, N_GROUPS, S, HEAD_DIM), BF16),
            jax.ShapeDtypeStruct((B, N_GROUPS, S, HEAD_DIM), BF16),
            jax.ShapeDtypeStruct((B, N_GROUPS, S, LANES), BF16),
            jax.ShapeDtypeStruct((B, N_GROUPS, S, LANES), BF16),
            jax.ShapeDtypeStruct((B, N_GROUPS, S // KEY_CHUNK, V_ROWS, KEY_CHUNK), BF16),
            jax.ShapeDtypeStruct((B, N_GROUPS, S // t, V_ROWS, t), BF16),
            jax.ShapeDtypeStruct((B, N_GROUPS, S // t, 8, HEADS_PER_GROUP * t), F32),
        ],
        compiler_params=_params("arbitrary", "arbitrary"),
    )(x, sc, sh, wqt, wnat, wvt, wgt, bg)


def _compress_kernel(tk_ref, tv_ref, pos_ref, w1_ref, w2k_ref, w2vt_ref, kc_ref, vct_ref):
    n = tk_ref.shape[2]
    half = CMP_STRIDE * HEAD_DIM

    def hidden(t_ref, idx):
        t16 = t_ref[0, 0]
        xa = (t16 + pos_ref[idx, 0:1, :]).astype(BF16)
        xb = (t16 + pos_ref[idx, 1:2, :]).astype(BF16)
        first = _dot(xa, w1_ref[idx, :half, :])
        second = _dot(xb, w1_ref[idx, half:, :])
        pre = first + pltpu.roll(second, n - 1, 0)
        return (pre / (1.0 + jnp.exp(-pre))).astype(BF16)

    kc = _dot(hidden(tk_ref, 0), w2k_ref[...])
    lane = lax.broadcasted_iota(jnp.int32, kc.shape, 1)
    ones = ((lane == HEAD_DIM) | (lane == HEAD_DIM + 1)).astype(F32)
    kc_ref[0, 0] = (kc + ones).astype(BF16)
    vct = _dot_nt(w2vt_ref[...], hidden(tv_ref, 1)).astype(BF16)
    for r in range(n // CMP_ROWS):
        vct_ref[0, 0, r] = vct[:, r * CMP_ROWS:(r + 1) * CMP_ROWS]


def _compress(kc, vc, cmp_pos, cmp_w1, cmp_w2):
    B, G, S, _ = kc.shape
    n = S // CMP_STRIDE
    width = CMP_STRIDE * HEAD_DIM
    tk = kc.reshape(B, G, n, width)
    tv = vc.reshape(B, G, n, width)
    pos = cmp_pos.reshape(2, 2, width)
    w1 = cmp_w1.astype(BF16)
    w2k = jnp.pad(cmp_w2[0], ((0, 0), (0, LANES - HEAD_DIM))).astype(BF16)
    w2vt = cmp_w2[1].T.astype(BF16)
    full = lambda a: pl.BlockSpec(a.shape, lambda b, g: (0,) * a.ndim)
    t_spec = pl.BlockSpec((1, 1, n, width), lambda b, g: (b, g, 0, 0))
    return pl.pallas_call(
        _compress_kernel, name="nsa_compress",
        grid=(B, G),
        in_specs=[t_spec, t_spec, full(pos), full(w1), full(w2k), full(w2vt)],
        out_specs=[
            pl.BlockSpec((1, 1, n, LANES), lambda b, g: (b, g, 0, 0)),
            pl.BlockSpec((1, 1, n // CMP_ROWS, HEAD_DIM, CMP_ROWS), lambda b, g: (b, g, 0, 0, 0)),
        ],
        out_shape=[
            jax.ShapeDtypeStruct((B, G, n, LANES), BF16),
            jax.ShapeDtypeStruct((B, G, n // CMP_ROWS, HEAD_DIM, CMP_ROWS), BF16),
        ],
        compiler_params=_params("arbitrary", "arbitrary"),
    )(tk, tv, pos, w1, w2k, w2vt)


def _bias_tiles_kernel(rb_ref, sel_ref, win_ref):
    hd = pl.program_id(0)
    far = rb_ref[FAR_BUCKET, hd]

    def dist(n, back):
        key = lax.broadcasted_iota(jnp.int32, (n, n), 0)
        qry = lax.broadcasted_iota(jnp.int32, (n, n), 1)
        return qry - key + back * n

    kc = sel_ref.shape[2]
    d_prev, d_diag = dist(kc, 1), dist(kc, 0)
    sel_ref[0, 0] = LOG2E * (_bucket_bias(d_prev, rb_ref, hd) - far)
    sel_ref[1, 0] = jnp.where(d_diag >= 0, LOG2E * (_bucket_bias(d_diag, rb_ref, hd) - far), NEG)
    t = win_ref.shape[2]
    n_back = win_ref.shape[0] - 1
    for r in range(n_back + 1):
        d = dist(t, n_back - r)
        win_ref[r, 0] = jnp.where((d >= 0) & (d < WINDOW), LOG2E * _bucket_bias(d, rb_ref, hd), NEG)


def _bias_tiles(rel_bias):
    t = NSA_T
    n_win = WINDOW // t + 1
    assert WINDOW % t == 0
    spec = lambda n, m: pl.BlockSpec((n, 1, m, m), lambda h: (0, h, 0, 0))
    return pl.pallas_call(
        _bias_tiles_kernel, name="rel_bias_tiles",
        grid=(N_HEADS,),
        in_specs=[pl.BlockSpec(memory_space=pltpu.SMEM)],
        out_specs=[spec(2, KEY_CHUNK), spec(n_win, t)],
        out_shape=[
            jax.ShapeDtypeStruct((2, N_HEADS, KEY_CHUNK, KEY_CHUNK), F32),
            jax.ShapeDtypeStruct((n_win, N_HEADS, t, t), F32),
        ],
        compiler_params=_params("arbitrary"),
    )(rel_bias)


def _far_bias_rows(rb_ref, g, width):
    t = width // HEADS_PER_GROUP
    lane = lax.broadcasted_iota(jnp.int32, (HEAD_DIM, width), 1)
    row = lax.broadcasted_iota(jnp.int32, (HEAD_DIM, width), 0)
    far = jnp.zeros((HEAD_DIM, width), F32)
    for hh in range(HEADS_PER_GROUP):
        far = jnp.where(lane >= hh * t, LOG2E * rb_ref[FAR_BUCKET, g * HEADS_PER_GROUP + hh], far)
    hi = far.astype(BF16).astype(F32)
    return jnp.where(row == 0, hi, jnp.where(row == 1, far - hi, 0.0)).astype(BF16)


def _nsa_cmp_kernel(rb_ref, qt_ref, kc_ref, vct_ref, at_ref, oc_ref, sb_ref, qp_ref, s_ref, imp_ref):
    g = pl.program_id(1)
    i = pl.program_id(2)
    width = qt_ref.shape[4]
    t = width // HEADS_PER_GROUP
    n = kc_ref.shape[2]
    nb = sb_ref.shape[3]
    t0 = i * t
    assert t & (t - 1) == 0 and n % CMP_ROWS == 0

    n_vis = (t0 + t - CMP_LEN) // CMP_STRIDE // CMP_ROWS + 1
    n_full = jnp.maximum((t0 - (CMP_LEN - 1)) // CMP_STRIDE + 1, 0) // CMP_ROWS
    rows_of = lambda r: pl.ds(pl.multiple_of(r * CMP_ROWS, CMP_ROWS), CMP_ROWS)

    def valid(r):
        blk = r * CMP_ROWS + lax.broadcasted_iota(jnp.int32, (CMP_ROWS, width), 0)
        qry = t0 + (lax.broadcasted_iota(jnp.int32, (CMP_ROWS, width), 1) & (t - 1))
        return qry >= blk * CMP_STRIDE + CMP_LEN - 1

    qp_ref[0:HEAD_DIM, :] = qt_ref[0, 0, 0]
    qp_ref[HEAD_DIM:LANES, :] = _far_bias_rows(rb_ref, g, width)

    def score_chunk(r, carry):
        s_ref[rows_of(r), :] = _dot(kc_ref[0, 0, rows_of(r), :], qp_ref[...])
        return carry

    lax.fori_loop(0, n_vis, score_chunk, 0)

    band = t // CMP_STRIDE + 16
    assert (FAR_DIST + CMP_LEN - 1) <= 16 * CMP_STRIDE and band <= n
    r0 = pl.multiple_of(jnp.clip(t0 // CMP_STRIDE - 16, 0, n - band), 8)
    blk_r = r0 + lax.broadcasted_iota(jnp.int32, (band, t), 0)
    qry_r = t0 + lax.broadcasted_iota(jnp.int32, (band, t), 1)
    dist_r = qry_r - (blk_r * CMP_STRIDE + CMP_LEN - 1)
    for hh in range(HEADS_PER_GROUP):
        hd = g * HEADS_PER_GROUP + hh
        corr = LOG2E * (_bucket_bias(dist_r, rb_ref, hd) - rb_ref[FAR_BUCKET, hd])
        s_ref[pl.ds(r0, band), hh * t:(hh + 1) * t] += corr

    def col_max(masked):
        def body(r, m):
            s = s_ref[rows_of(r), :]
            if masked:
                s = jnp.where(valid(r), s, NEG)
            return jnp.maximum(m, jnp.max(s, axis=0, keepdims=True))
        return body

    m = lax.fori_loop(0, n_full, col_max(False), jnp.full((1, width), NEG, F32))
    m = lax.fori_loop(n_full, n_vis, col_max(True), m)

    def exp_sum(masked):
        def body(r, l):
            p = jnp.exp2(s_ref[rows_of(r), :] - m)
            if masked:
                p = jnp.where(valid(r), p, 0.0)
            s_ref[rows_of(r), :] = p
            return l + jnp.sum(p, axis=0, keepdims=True)
        return body

    l = lax.fori_loop(0, n_full, exp_sum(False), jnp.zeros((1, width), F32))
    l = lax.fori_loop(n_full, n_vis, exp_sum(True), l)
    inv = 1.0 / jnp.maximum(l, TINY)

    oc_ref[0, 0, 0] = jnp.zeros((HEAD_DIM, width), F32)
    imp_ref[...] = jnp.zeros_like(imp_ref)

    def finish(r, carry):
        p = s_ref[rows_of(r), :] * inv
        oc_ref[0, 0, 0] += _dot(vct_ref[0, 0, r], p.astype(BF16))
        imp = p[:, 0:t]
        for hh in range(1, HEADS_PER_GROUP):
            imp = imp + p[:, hh * t:(hh + 1) * t]
        hi, mid, lo = _split3(imp)
        a = at_ref[...]
        band = pl.ds(pl.multiple_of(r * (CMP_ROWS * CMP_STRIDE // SEL_BLOCK), 8), a.shape[0])
        imp_ref[band, :] += _dot(a, hi) + _dot(a, mid) + _dot(a, lo)
        return carry

    lax.fori_loop(0, n_vis, finish, 0)

    sblk = lax.broadcasted_iota(jnp.int32, (nb, t), 0)
    cur = (t0 + lax.broadcasted_iota(jnp.int32, (nb, t), 1)) // SEL_BLOCK
    forced = (sblk == 0) | (sblk == cur) | (sblk == cur - 1)
    is_cand = (sblk >= 1) & (sblk <= cur - 2)
    cand = jnp.where(is_cand, imp_ref[0:nb, :], -1.0)
    sblk_f = sblk.astype(F32)
    for _ in range(SEL_TOPK - N_FORCED):
        best = jnp.max(cand, axis=0, keepdims=True)
        first = jnp.min(jnp.where(cand == best, sblk_f, float(nb)), axis=0, keepdims=True)
        cand = jnp.where(sblk_f == first, -1.0, cand)
    chosen = forced | (is_cand & (cand < 0.0))
    sb_ref[0, 0, 0] = jnp.where(chosen, 0.0, NEG).astype(BF16)


def _nsa_cmp(rel_bias, qt, kcmp, vcmp_t, n_sel):
    B, G, nq, _, width = qt.shape
    t = width // HEADS_PER_GROUP
    n = kcmp.shape[2]
    n_cmp = n - 1
    nb = -(-n_sel // LANES) * LANES
    R = SEL_BLOCK // CMP_STRIDE
    assert n_cmp == n - 1 and nb >= n_sel
    band_rows = -(-(CMP_ROWS // R + 1) // 8) * 8
    at = np.zeros((band_rows, CMP_ROWS), np.float32)
    for j in range(CMP_ROWS // R + 1):
        at[j, max(R * j - 1, 0):min(R * j + R, CMP_ROWS)] = 1.0
    at = jnp.asarray(at, BF16)
    return pl.pallas_call(
        _nsa_cmp_kernel, name="nsa_cmp_topk",
        grid=(B, G, nq),
        in_specs=[
            pl.BlockSpec(memory_space=pltpu.SMEM),
            pl.BlockSpec((1, 1, 1, HEAD_DIM, width), lambda b, g, i: (b, g, i, 0, 0)),
            pl.BlockSpec((1, 1, n, LANES), lambda b, g, i: (b, g, 0, 0)),
            pl.BlockSpec((1, 1, n // CMP_ROWS, HEAD_DIM, CMP_ROWS), lambda b, g, i: (b, g, 0, 0, 0)),
            pl.BlockSpec((band_rows, CMP_ROWS), lambda b, g, i: (0, 0)),
        ],
        out_specs=[
            pl.BlockSpec((1, 1, 1, HEAD_DIM, width), lambda b, g, i: (b, g, i, 0, 0)),
            pl.BlockSpec((1, 1, 1, nb, t), lambda b, g, i: (b, g, i, 0, 0)),
        ],
        out_shape=[
            jax.ShapeDtypeStruct((B, G, nq, HEAD_DIM, width), F32),
            jax.ShapeDtypeStruct((B, G, nq, nb, t), BF16),
        ],
        scratch_shapes=[pltpu.VMEM((LANES, width), BF16), pltpu.VMEM((n, width), F32),
                        pltpu.VMEM((nb + band_rows, t), F32)],
        compiler_params=_params("arbitrary", "arbitrary", "arbitrary"),
    )(rel_bias, qt, kcmp, vcmp_t, at)


def _nsa_win_kernel(qt_ref, *refs):
    n_win = (len(refs) - 5) // 2
    k_refs, v_refs = refs[:n_win], refs[n_win:2 * n_win]
    wb_ref, ow_ref, qp_ref, s_ref, p_ref = refs[2 * n_win:]
    i = pl.program_id(2)
    width = qt_ref.shape[4]
    t = width // HEADS_PER_GROUP
    qt = qt_ref[0, 0, 0]
    qp_ref[0:HEAD_DIM, :] = qt
    qp_ref[HEAD_DIM:LANES, :] = jnp.zeros_like(qt)
    m = None
    for c in range(n_win):
        k = k_refs[c][0, 0]
        missing = jnp.where(i - (n_win - 1 - c) < 0, NEG, 0.0)
        col_max = []
        for blk in range(width // QUERY_SUB):
            cols = slice(blk * QUERY_SUB, (blk + 1) * QUERY_SUB)
            hh, off = divmod(blk * QUERY_SUB, t)
            part = _dot(k, qp_ref[:, cols]) + wb_ref[c, hh, :, off:off + QUERY_SUB] + missing
            s_ref[c, :, cols] = part
            col_max.append(jnp.max(part, axis=0, keepdims=True))
        col_max = jnp.concatenate(col_max, axis=1)
        m = col_max if m is None else jnp.maximum(m, col_max)
    acc = jnp.zeros((V_ROWS, width), F32)
    for c in range(n_win):
        for blk in range(width // QUERY_SUB):
            cols = slice(blk * QUERY_SUB, (blk + 1) * QUERY_SUB)
            p_ref[c, :, cols] = jnp.exp2((s_ref[c, :, cols] - m[:, cols]).astype(BF16))
        acc = acc + _dot(v_refs[c][0, 0, 0], p_ref[c])
    ow_ref[0, 0, 0] = acc[:HEAD_DIM] / acc[HEAD_DIM:HEAD_DIM + 1]


def _nsa_window(qt, kw, vwt, win_bias):
    B, G, nq, _, width = qt.shape
    t = width // HEADS_PER_GROUP
    n_win = win_bias.shape[0]
    backs = list(range(n_win - 1, -1, -1))
    k_spec = lambda back: pl.BlockSpec((1, 1, t, LANES), lambda b, g, i: (b, g, jnp.maximum(i - back, 0), 0))
    v_spec = lambda back: pl.BlockSpec((1, 1, 1, V_ROWS, t),
                                       lambda b, g, i: (b, g, jnp.maximum(i - back, 0), 0, 0))
    return pl.pallas_call(
        _nsa_win_kernel, name="nsa_window",
        grid=(B, G, nq),
        in_specs=[pl.BlockSpec((1, 1, 1, HEAD_DIM, width), lambda b, g, i: (b, g, i, 0, 0))]
        + [k_spec(back) for back in backs] + [v_spec(back) for back in backs]
        + [pl.BlockSpec((n_win, HEADS_PER_GROUP, t, t), lambda b, g, i: (0, g, 0, 0))],
        out_specs=pl.BlockSpec((1, 1, 1, HEAD_DIM, width), lambda b, g, i: (b, g, i, 0, 0)),
        out_shape=jax.ShapeDtypeStruct((B, G, nq, HEAD_DIM, width), F32),
        scratch_shapes=[pltpu.VMEM((LANES, width), BF16),
                        pltpu.VMEM((n_win, t, width), F32), pltpu.VMEM((n_win, t, width), BF16)],
        compiler_params=_params("arbitrary", "arbitrary", "arbitrary"),
    )(qt, *([kw] * n_win), *([vwt] * n_win), win_bias)


def _nsa_sel_kernel(rb_ref, qt_ref, sb_ref, ks_ref, e_ref, vst_ref, cb_ref, oc_ref, ow_ref, gt_ref,
                    o_ref, qp_ref, s_buf, p_buf, acc_ref):
    g = pl.program_id(1)
    i = pl.program_id(2)
    width = qt_ref.shape[4]
    t = width // HEADS_PER_GROUP
    per_tile = t // KEY_CHUNK
    chunks_per_slab = LANES * SEL_BLOCK // KEY_CHUNK
    blocks_per_head = t // QUERY_SUB
    assert per_tile == 2 and t % QUERY_SUB == 0

    far_rows = _far_bias_rows(rb_ref, g, width)
    for slab in range(qp_ref.shape[0]):
        qp_ref[slab, 0:HEAD_DIM, :] = qt_ref[0, 0, 0]
        qp_ref[slab, HEAD_DIM:LANES, :] = far_rows
        sb = sb_ref[0, 0, 0, slab * LANES:(slab + 1) * LANES, :]
        qp_ref[slab, LANES:2 * LANES, :] = jnp.concatenate([sb] * HEADS_PER_GROUP, axis=1)

    def scores(c):
        c = jnp.maximum(c, 0)
        rows = pl.ds(pl.multiple_of(c * KEY_CHUNK, KEY_CHUNK), KEY_CHUNK)
        kp = jnp.concatenate([ks_ref[0, 0, rows, :], e_ref[rows, :]], axis=1)
        return lambda cols: _dot(kp, qp_ref[c // chunks_per_slab, :, cols])

    def values(c):
        return vst_ref[0, 0, jnp.maximum(c, 0)]

    def near_bias(r):
        def bias(c):
            hh, part = divmod(c, blocks_per_head)
            ahead = part * QUERY_SUB // KEY_CHUNK - (r - per_tile)
            if ahead == 0:
                return cb_ref[1, hh]
            if ahead == 1:
                return cb_ref[0, hh]
            if ahead < 0:
                return jnp.full((KEY_CHUNK, QUERY_SUB), NEG, F32)
            return None
        return bias

    tail = [(None, True), (None, True)] + [(near_bias(r), r < per_tile) for r in range(2 * per_tile)]
    o_sel = _flash_sweep(per_tile * jnp.minimum(i - 2, 0), jnp.maximum(i - 2, 0), scores, values, tail,
                         s_buf, p_buf, acc_ref)

    out = gt_ref[0, 0, 0, 0:1, :] * oc_ref[0, 0, 0] + gt_ref[0, 0, 0, 1:2, :] * o_sel \
        + gt_ref[0, 0, 0, 2:3, :] * ow_ref[0, 0, 0]
    for hh in range(HEADS_PER_GROUP):
        o_ref[0, hh * HEAD_DIM:(hh + 1) * HEAD_DIM, :] = out[:, hh * t:(hh + 1) * t].astype(o_ref.dtype)


def _nsa_select_combine(rel_bias, qt, sel_bias, ks, vst, sel_corr, oc, ow, gates):
    B, G, nq, _, width = qt.shape
    t = width // HEADS_PER_GROUP
    S = nq * t
    nb = sel_bias.shape[3]
    blocks = (np.arange(S) // SEL_BLOCK) % LANES
    onehot = jnp.asarray(blocks[:, None] == np.arange(LANES)[None, :], BF16)
    tile = pl.BlockSpec((1, 1, 1, HEAD_DIM, width), lambda b, g, i: (b, g, i, 0, 0))
    return pl.pallas_call(
        _nsa_sel_kernel, name="nsa_select",
        grid=(B, G, nq),
        in_specs=[
            pl.BlockSpec(memory_space=pltpu.SMEM),
            tile,
            pl.BlockSpec((1, 1, 1, nb, t), lambda b, g, i: (b, g, i, 0, 0)),
            pl.BlockSpec((1, 1, S, LANES), lambda b, g, i: (b, g, 0, 0)),
            pl.BlockSpec((S, LANES), lambda b, g, i: (0, 0)),
            pl.BlockSpec((1, 1, S // KEY_CHUNK, V_ROWS, KEY_CHUNK), lambda b, g, i: (b, g, 0, 0, 0)),
            pl.BlockSpec((2, HEADS_PER_GROUP, KEY_CHUNK, KEY_CHUNK), lambda b, g, i: (0, g, 0, 0)),
            tile, tile,
            pl.BlockSpec((1, 1, 1, 8, width), lambda b, g, i: (b, g, i, 0, 0)),
        ],
        out_specs=pl.BlockSpec((1, HEADS_PER_GROUP * HEAD_DIM, t), lambda b, g, i: (b, g, i)),
        out_shape=jax.ShapeDtypeStruct((B, D_MODEL, S), BF16),
        scratch_shapes=[pltpu.VMEM((nb // LANES, 2 * LANES, width), BF16),
                        pltpu.VMEM((2, KEY_CHUNK, width), F32), pltpu.VMEM((2, KEY_CHUNK, width), BF16),
                        pltpu.VMEM((V_ROWS, width), F32)],
        compiler_params=_params("arbitrary", "arbitrary", "arbitrary"),
    )(rel_bias, qt, sel_bias, ks, onehot, vst, sel_corr, oc, ow, gates)


def _nsa_attention_t(x, sc, sh, w_in, b_gate, cmp_pos, cmp_w1, cmp_w2, rel_bias):
    B, S, _ = x.shape
    t = NSA_T
    n_sel = S // SEL_BLOCK
    assert S % ROW_TILE == 0 and n_sel >= SEL_TOPK and S // CMP_STRIDE >= t // CMP_STRIDE + 16
    qt, kc, vc, ks, kw, vst, vwt, gt = _nsa_proj(x, sc, sh, w_in, b_gate)
    kcmp, vcmp_t = _compress(kc, vc, cmp_pos, cmp_w1, cmp_w2)
    sel_corr, win_bias = _bias_tiles(rel_bias)
    oc, sel_bias = _nsa_cmp(rel_bias, qt, kcmp, vcmp_t, n_sel)
    ow = _nsa_window(qt, kw, vwt, win_bias)
    return _nsa_select_combine(rel_bias, qt, sel_bias, ks, vst, sel_corr, oc, ow, gt)


def kernel(x, c, ada_w, ada_b, ln_g, ln_b, fox_w_in, fox_b_f, fox_w_out, nsa_w_in, nsa_b_gate,
           nsa_cmp_pos, nsa_cmp_w1, nsa_cmp_w2, nsa_w_out, rel_bias, ffn_w_in, ffn_w_out):
    B, S, _ = x.shape
    assert S % FOX_T == 0 and S % ROW_TILE == 0 and ROW_TILE == NSA_T
    mod = _modulation(c, ada_w, ada_b)
    for layer in range(DEPTH):
        sh_a, sc_a, g_a, sh_f, sc_f, g_f = [m.reshape(B, 1, D_MODEL) for m in jnp.split(mod[layer], 6, axis=-1)]
        j = layer // 2
        if layer % 2 == 0:
            kp, qt, vt = _fox_proj(x, sc_a, sh_a, fox_w_in[j], fox_b_f[j])
            attn_t = _fox_attention(qt, kp, vt)
            w_out = fox_w_out[j]
        else:
            attn_t = _nsa_attention_t(x, sc_a, sh_a, nsa_w_in[j], nsa_b_gate[j], nsa_cmp_pos[j],
                                      nsa_cmp_w1[j], nsa_cmp_w2[j], rel_bias)
            w_out = nsa_w_out[j]
        x = _outproj_ln(attn_t, w_out, x, g_a, ln_g[layer, 0], ln_b[layer, 0])
        x = _ffn_ln(x, sc_f, sh_f, g_f, ffn_w_in[layer], ffn_w_out[layer], ln_g[layer, 1], ln_b[layer, 1])
    return x
```

```python
import functools
import math

import numpy as np
import jax
import jax.numpy as jnp
from jax import lax
from jax.experimental import pallas as pl
from jax.experimental.pallas import tpu as pltpu

F32 = jnp.float32
BF16 = jnp.bfloat16
HIGHEST = lax.Precision.HIGHEST

D_MODEL = 1024
HEAD_DIM = 64
N_HEADS = 16
N_GROUPS = 4
HEADS_PER_GROUP = 4
KV_WIDTH = N_GROUPS * HEAD_DIM
CMP_LEN = 32
CMP_STRIDE = 16
CMP_HIDDEN = 128
SEL_BLOCK = 64
SEL_TOPK = 16
WINDOW = 512
REL_BUCKETS = 32
FF_HIDDEN = 2816
DEPTH = 2
DN_ALPHA = (2 * DEPTH) ** 0.25
LN_EPS = 1e-5
NEG = -1e30
TINY = 1e-30

LANES = 128
VMEM_LIMIT = 56 * 1024 * 1024

ROW_TILE = 512
FOX_T = 1024
NSA_T = 512
N_FORCED = 3

BUCKET_START = (0, 1, 2, 3, 4, 5, 6, 7, 8, 9, 10, 11, 12, 13, 14, 15,
                16, 19, 21, 24, 27, 31, 35, 40, 46, 52, 59, 67, 77, 87, 99, 113)
FAR_BUCKET = REL_BUCKETS - 1
FAR_DIST = BUCKET_START[FAR_BUCKET]

LOG2E = math.log2(math.e)
QUERY_SUB = 256
KEY_CHUNK = 256
PAIRS_PER_TRIP = 4
TOPK_ROWS = 64
CMP_ROWS = 128
V_ROWS = HEAD_DIM + 16
NT_DIMS = (((1,), (1,)), ((), ()))


def _params(*sem):
    return pltpu.CompilerParams(dimension_semantics=sem, vmem_limit_bytes=VMEM_LIMIT)


def _dot(a, b, **kw):
    return jnp.dot(a, b, preferred_element_type=F32, **kw)


def _dot_nt(a, b):
    return lax.dot_general(a, b, NT_DIMS, preferred_element_type=F32)


def _split3(v):
    hi = v.astype(BF16)
    r = v - hi.astype(F32)
    mid = r.astype(BF16)
    lo = (r - mid.astype(F32)).astype(BF16)
    return hi, mid, lo


def _layer_norm(z, g, b):
    mu = jnp.mean(z, axis=-1, keepdims=True)
    zc = z - mu
    var = jnp.mean(zc * zc, axis=-1, keepdims=True)
    return zc * lax.rsqrt(var + LN_EPS) * g + b


def _bucket_bias(dist, rb_ref, head):
    bias = jnp.full(dist.shape, rb_ref[0, head], F32)
    for k in range(1, REL_BUCKETS):
        bias = jnp.where(dist >= BUCKET_START[k], rb_ref[k, head], bias)
    return bias


def _mod_kernel(c_ref, w_ref, b_ref, o_ref):
    c = c_ref[...]
    cs = c / (1.0 + jnp.exp(-c))
    o_ref[0] = _dot(cs, w_ref[0], precision=HIGHEST) + b_ref[0]


def _modulation(c, ada_w, ada_b):
    B = c.shape[0]
    depth, _, n = ada_w.shape
    rows = 8
    c_pad = jnp.pad(c, ((0, rows - B), (0, 0)))
    tn = 1536
    out = pl.pallas_call(
        _mod_kernel, name="adaln_mod",
        grid=(depth, n // tn),
        in_specs=[
            pl.BlockSpec((rows, D_MODEL), lambda l, j: (0, 0)),
            pl.BlockSpec((1, D_MODEL, tn), lambda l, j: (l, 0, j)),
            pl.BlockSpec((1, 1, tn), lambda l, j: (l, 0, j)),
        ],
        out_specs=pl.BlockSpec((1, rows, tn), lambda l, j: (l, 0, j)),
        out_shape=jax.ShapeDtypeStruct((depth, rows, n), F32),
        compiler_params=_params("arbitrary", "arbitrary"),
    )(c_pad, ada_w, ada_b.reshape(depth, 1, n))
    return out[:, :B]


def _fox_proj_kernel(x_ref, sc_ref, sh_ref, wk_ref, wqt_ref, wvt_ref, wfh_ref, wfl_ref,
                     bf_ref, place_ref, kp_ref, qt_ref, vt_ref, carry_ref):
    tm = x_ref.shape[1]

    @pl.when(pl.program_id(1) == 0)
    def _():
        carry_ref[...] = jnp.zeros_like(carry_ref)

    h = x_ref[0] * (1.0 + sc_ref[0]) + sh_ref[0]
    hb = h.astype(BF16)
    hl = (h - hb.astype(F32)).astype(BF16)

    f = _dot(hb, wfh_ref[...]) + _dot(hl, wfh_ref[...]) + _dot(hb, wfl_ref[...])
    z = f + bf_ref[...]
    logf = jnp.minimum(z, 0.0) - jnp.log(1.0 + jnp.exp(-jnp.abs(z)))
    row = lax.broadcasted_iota(jnp.int32, (tm, tm), 0)
    col = lax.broadcasted_iota(jnp.int32, (tm, tm), 1)
    lower = (col <= row).astype(F32)
    cum = _dot(lower, logf, precision=HIGHEST) + carry_ref[0:1, :]
    carry_ref[...] = jnp.broadcast_to(cum[tm - 1:tm, :], carry_ref.shape)

    hi, mid, lo = [p.astype(F32) for p in _split3(-LOG2E * cum)]
    lane = lax.broadcasted_iota(jnp.int32, (tm, LANES), 1)
    pieces = jnp.where(lane < N_HEADS, hi, jnp.where(lane < 2 * N_HEADS, mid, lo)).astype(BF16)
    kp = _dot(hb, wk_ref[...]) + _dot(pieces, place_ref[...])
    qt = _dot_nt(wqt_ref[...], hb).astype(BF16)
    qrow = lax.broadcasted_iota(jnp.int32, (LANES - HEAD_DIM, tm), 0)
    q_ones = (qrow < 3).astype(F32).astype(BF16)
    vt = _dot_nt(wvt_ref[...], hb)
    for hd in range(N_HEADS):
        kp_ref[0, hd] = kp[:, hd * LANES:(hd + 1) * LANES].astype(BF16)
        qt_ref[0, hd, 0:HEAD_DIM, :] = qt[hd * HEAD_DIM:(hd + 1) * HEAD_DIM, :]
        qt_ref[0, hd, HEAD_DIM:LANES, :] = q_ones
        for c in range(tm // KEY_CHUNK):
            vt_ref[0, hd, c, 0:HEAD_DIM, :] = vt[hd * HEAD_DIM:(hd + 1) * HEAD_DIM,
                                                 c * KEY_CHUNK:(c + 1) * KEY_CHUNK].astype(BF16)
            vt_ref[0, hd, c, HEAD_DIM:V_ROWS, :] = _ones_row_tail(KEY_CHUNK)


def _fox_proj(x, sc, sh, w_in, b_f):
    B, S, _ = x.shape
    tm = ROW_TILE
    scale = LOG2E * HEAD_DIM ** -0.5
    wq = (w_in[:, :D_MODEL] * scale).reshape(D_MODEL, N_HEADS, HEAD_DIM)
    wk = w_in[:, D_MODEL:2 * D_MODEL].reshape(D_MODEL, N_HEADS, HEAD_DIM)
    wv = w_in[:, 2 * D_MODEL:3 * D_MODEL]
    wf = w_in[:, 3 * D_MODEL:]
    pad = ((0, 0), (0, 0), (0, LANES - HEAD_DIM))
    wk_p = jnp.pad(wk, pad).reshape(D_MODEL, N_HEADS * LANES).astype(BF16)
    wqt_p = wq.reshape(D_MODEL, N_HEADS * HEAD_DIM).T.astype(BF16)
    wvt = wv.T.astype(BF16)
    wf_rep = jnp.pad(jnp.tile(wf, (1, 3)), ((0, 0), (0, LANES - 3 * N_HEADS)))
    wf_hi = wf_rep.astype(BF16)
    wf_lo = (wf_rep - wf_hi.astype(F32)).astype(BF16)
    bf_rep = jnp.pad(jnp.tile(b_f, 3), (0, LANES - 3 * N_HEADS)).reshape(1, LANES)
    place = np.zeros((LANES, N_HEADS * LANES), np.float32)
    for r in range(3):
        for hd in range(N_HEADS):
            place[r * N_HEADS + hd, hd * LANES + HEAD_DIM + r] = 1.0
    place = jnp.asarray(place, BF16)

    full = lambda a: pl.BlockSpec(a.shape, lambda b, i: (0,) * a.ndim)
    return pl.pallas_call(
        _fox_proj_kernel, name="fox_proj",
        grid=(B, S // tm),
        in_specs=[
            pl.BlockSpec((1, tm, D_MODEL), lambda b, i: (b, i, 0)),
            pl.BlockSpec((1, 1, D_MODEL), lambda b, i: (b, 0, 0)),
            pl.BlockSpec((1, 1, D_MODEL), lambda b, i: (b, 0, 0)),
            full(wk_p), full(wqt_p), full(wvt), full(wf_hi), full(wf_lo), full(bf_rep), full(place),
        ],
        out_specs=[
            pl.BlockSpec((1, N_HEADS, tm, LANES), lambda b, i: (b, 0, i, 0)),
            pl.BlockSpec((1, N_HEADS, LANES, tm), lambda b, i: (b, 0, 0, i)),
            pl.BlockSpec((1, N_HEADS, tm // KEY_CHUNK, V_ROWS, KEY_CHUNK), lambda b, i: (b, 0, i, 0, 0)),
        ],
        out_shape=[
            jax.ShapeDtypeStruct((B, N_HEADS, S, LANES), BF16),
            jax.ShapeDtypeStruct((B, N_HEADS, LANES, S), BF16),
            jax.ShapeDtypeStruct((B, N_HEADS, S // KEY_CHUNK, V_ROWS, KEY_CHUNK), BF16),
        ],
        scratch_shapes=[pltpu.VMEM((8, LANES), F32)],
        compiler_params=_params("arbitrary", "arbitrary"),
    )(x, sc, sh, wk_p, wqt_p, wvt, wf_hi, wf_lo, bf_rep, place)


def _ones_row_tail(width):
    row = lax.broadcasted_iota(jnp.int32, (V_ROWS - HEAD_DIM, width), 0)
    return (row == 0).astype(F32).astype(BF16)


def _flash_sweep(first, n_far_pairs, scores, values, tail, s_buf, p_buf, acc_ref, far_pairs_even=False):
    width = acc_ref.shape[1]
    n_blocks = width // QUERY_SUB
    assert len(tail) % 2 == 0 and width % QUERY_SUB == 0
    assert tail[0][0] is None and tail[1][0] is None
    cat = lambda parts: jnp.concatenate(parts, axis=1)

    def fetch(c, par, bias, may_be_missing):
        block_scores = scores(c)
        missing = jnp.where(c < 0, NEG, 0.0) if may_be_missing else None
        col_max = []
        for blk in range(n_blocks):
            cols = slice(blk * QUERY_SUB, (blk + 1) * QUERY_SUB)
            part = block_scores(cols)
            extra = None if bias is None else bias(blk)
            if extra is not None:
                part = part + extra
            if missing is not None:
                part = part + missing
            s_buf[par, :, cols] = part
            col_max.append(jnp.max(part, axis=0, keepdims=True))
        return cat(col_max)

    def step(j, par, stats, nxt=None):
        m, corr1, corr2, cmax = stats
        acc_ref[...] = acc_ref[...] * corr2 + _dot(values(j - 2), p_buf[par])
        m_new = jnp.maximum(m, cmax[par])
        for blk in range(n_blocks):
            cols = slice(blk * QUERY_SUB, (blk + 1) * QUERY_SUB)
            p_buf[par, :, cols] = jnp.exp2((s_buf[par, :, cols] - m_new[:, cols]).astype(BF16))
        if nxt is not None:
            cmax = tuple(fetch(j + 2, par, *nxt) if q == par else cmax[q] for q in range(2))
        return m_new, jnp.exp2(m - m_new), corr1, cmax

    far = (None, False)

    def pairs(j, n, stats):
        for k in range(n):
            stats = step(j + 2 * k + 1, 1, step(j + 2 * k, 0, stats, far), far)
        return stats

    p_buf[...] = jnp.zeros_like(p_buf)
    acc_ref[...] = jnp.zeros_like(acc_ref)
    ones = jnp.ones((1, width), F32)
    cmax = (fetch(first, 0, None, True), fetch(first + 1, 1, None, True))
    stats = (jnp.full((1, width), NEG, F32), ones, ones, cmax)
    start = first
    for n in (1, 2):
        if n == 1 and far_pairs_even:
            continue
        group = (n_far_pairs // n) % 2
        stats = lax.cond(group == 1, lambda st, j=start, n=n: pairs(j, n, st), lambda st: st, stats)
        start = start + 2 * n * group
    stats = lax.fori_loop(0, n_far_pairs // PAIRS_PER_TRIP,
                          lambda jj, st: pairs(start + 2 * PAIRS_PER_TRIP * jj, PAIRS_PER_TRIP, st), stats)
    last = first + 2 * n_far_pairs + len(tail)
    for r in range(len(tail)):
        stats = step(last - len(tail) + r, r % 2, stats, tail[r + 2] if r + 2 < len(tail) else None)
    _, corr1, corr2, _ = stats
    acc = acc_ref[...] * corr2 + _dot(values(last - 2), p_buf[0])
    acc = acc * corr1 + _dot(values(last - 1), p_buf[1])
    return acc[:HEAD_DIM] / acc[HEAD_DIM:HEAD_DIM + 1]


def _fox_attn_kernel(qt_ref, kp_ref, vt_ref, o_ref, s_buf, p_buf, acc_ref):
    i = pl.program_id(2)
    t = qt_ref.shape[3]
    per_tile = t // KEY_CHUNK

    def scores(c):
        rows = pl.ds(pl.multiple_of(jnp.maximum(c, 0) * KEY_CHUNK, KEY_CHUNK), KEY_CHUNK)
        k = kp_ref[0, 0, rows, :]
        return lambda cols: _dot(k, qt_ref[0, 0, :, cols])

    def values(c):
        return vt_ref[0, 0, jnp.maximum(c, 0)]

    key = lax.broadcasted_iota(jnp.int32, (KEY_CHUNK, QUERY_SUB), 0)
    qry = lax.broadcasted_iota(jnp.int32, (KEY_CHUNK, QUERY_SUB), 1)
    causal = lambda r: (lambda c: jnp.where(key + r * KEY_CHUNK <= qry + c * QUERY_SUB, 0.0, NEG))
    assert per_tile % 4 == 0
    tail = [(None, True)] * per_tile + [(causal(r), False) for r in range(per_tile)]
    out = _flash_sweep(per_tile * jnp.minimum(i - 1, 0), (per_tile // 2) * jnp.maximum(i - 1, 0),
                       scores, values, tail, s_buf, p_buf, acc_ref, far_pairs_even=True)
    o_ref[0] = out.astype(o_ref.dtype)


def _fox_attention(qt, kp, vt):
    B, H, S, _ = kp.shape
    t = FOX_T
    return pl.pallas_call(
        _fox_attn_kernel, name="fox_attn",
        grid=(B, H, S // t),
        in_specs=[
            pl.BlockSpec((1, 1, LANES, t), lambda b, h, i: (b, h, 0, i)),
            pl.BlockSpec((1, 1, S, LANES), lambda b, h, i: (b, h, 0, 0)),
            pl.BlockSpec((1, 1, S // KEY_CHUNK, V_ROWS, KEY_CHUNK), lambda b, h, i: (b, h, 0, 0, 0)),
        ],
        out_specs=pl.BlockSpec((1, HEAD_DIM, t), lambda b, h, i: (b, h, i)),
        out_shape=jax.ShapeDtypeStruct((B, H * HEAD_DIM, S), BF16),
        scratch_shapes=[pltpu.VMEM((2, KEY_CHUNK, t), F32), pltpu.VMEM((2, KEY_CHUNK, t), BF16),
                        pltpu.VMEM((V_ROWS, t), F32)],
        compiler_params=_params("arbitrary", "arbitrary", "arbitrary"),
    )(qt, kp, vt)


def _block_tail_kernel(at_ref, wo_ref, x_ref, ga_ref, g1_ref, b1_ref, sc_ref, sh_ref, gf_ref,
                       wa_ref, wb_ref, w2_ref, g2_ref, b2_ref, o_ref):
    y = lax.dot_general(at_ref[0], wo_ref[...], (((0,), (0,)), ((), ())), preferred_element_type=F32)
    x1 = _layer_norm(DN_ALPHA * x_ref[0] + (1.0 + ga_ref[0]) * y, g1_ref[...], b1_ref[...])
    hb = (x1 * (1.0 + sc_ref[0]) + sh_ref[0]).astype(BF16)
    a = _dot(hb, wa_ref[...])
    b = _dot(hb, wb_ref[...])
    u = (a / (1.0 + jnp.exp(-a)) * b).astype(BF16)
    z = DN_ALPHA * x1 + (1.0 + gf_ref[0]) * _dot(u, w2_ref[...])
    o_ref[0] = _layer_norm(z, g2_ref[...], b2_ref[...])


def _block_tail(attn_t, w_out, x, gate_a, ln1, sc, sh, gate_f, w_in, w_out_ffn, ln2):
    B, S, _ = x.shape
    tm = ROW_TILE
    w_in = w_in.astype(BF16)
    row = lambda r: r.reshape(1, -1)
    mod = pl.BlockSpec((1, 1, D_MODEL), lambda b, i: (b, 0, 0))
    vec = pl.BlockSpec((1, D_MODEL), lambda b, i: (0, 0))
    resident = lambda shape, index: pl.BlockSpec(shape, index, pipeline_mode=pl.Buffered(1))
    return pl.pallas_call(
        _block_tail_kernel, name="outproj_ffn_ln",
        grid=(B, S // tm),
        in_specs=[
            pl.BlockSpec((1, D_MODEL, tm), lambda b, i: (b, 0, i)),
            resident((D_MODEL, D_MODEL), lambda b, i: (0, 0)),
            pl.BlockSpec((1, tm, D_MODEL), lambda b, i: (b, i, 0)),
            mod, vec, vec,
            mod, mod, mod,
            resident((D_MODEL, FF_HIDDEN), lambda b, i: (0, 0)),
            resident((D_MODEL, FF_HIDDEN), lambda b, i: (0, 1)),
            resident((FF_HIDDEN, D_MODEL), lambda b, i: (0, 0)),
            vec, vec,
        ],
        out_specs=pl.BlockSpec((1, tm, D_MODEL), lambda b, i: (b, i, 0)),
        out_shape=jax.ShapeDtypeStruct((B, S, D_MODEL), F32),
        compiler_params=_params("arbitrary", "arbitrary"),
    )(attn_t, w_out.astype(BF16), x, gate_a, row(ln1[0]), row(ln1[1]), sc, sh, gate_f,
      w_in, w_in, w_out_ffn.astype(BF16), row(ln2[0]), row(ln2[1]))


def _nsa_proj_kernel(x_ref, sc_ref, sh_ref, wqt_ref, wnat_ref, wvt_ref, wgt_ref, bg_ref,
                     qt_ref, kc_ref, vc_ref, ks_ref, kw_ref, vst_ref, vwt_ref, gt_ref):
    tm = x_ref.shape[1]
    t = NSA_T
    hb = (x_ref[0] * (1.0 + sc_ref[0]) + sh_ref[0]).astype(BF16)

    qt = _dot_nt(wqt_ref[...], hb).astype(BF16)
    for g in range(N_GROUPS):
        for hh in range(HEADS_PER_GROUP):
            r0 = (g * HEADS_PER_GROUP + hh) * HEAD_DIM
            for c in range(tm // t):
                qt_ref[0, g, c, :, hh * t:(hh + 1) * t] = qt[r0:r0 + HEAD_DIM, c * t:(c + 1) * t]

    nat = _dot(hb, wnat_ref[...])
    lane = lax.broadcasted_iota(jnp.int32, (tm, LANES), 1)
    ones = ((lane == HEAD_DIM) | (lane == HEAD_DIM + 1)).astype(F32)
    off_ks = 2 * KV_WIDTH
    off_kw = off_ks + N_GROUPS * LANES
    for g in range(N_GROUPS):
        kc_ref[0, g] = nat[:, g * HEAD_DIM:(g + 1) * HEAD_DIM].astype(BF16)
        vc_ref[0, g] = nat[:, KV_WIDTH + g * HEAD_DIM:KV_WIDTH + (g + 1) * HEAD_DIM].astype(BF16)
        ks_ref[0, g] = (nat[:, off_ks + g * LANES:off_ks + (g + 1) * LANES] + ones).astype(BF16)
        kw_ref[0, g] = nat[:, off_kw + g * LANES:off_kw + (g + 1) * LANES].astype(BF16)

    vt = _dot_nt(wvt_ref[...], hb).astype(BF16)
    for g in range(N_GROUPS):
        for c in range(tm // KEY_CHUNK):
            vst_ref[0, g, c, 0:HEAD_DIM, :] = vt[g * HEAD_DIM:(g + 1) * HEAD_DIM, c * KEY_CHUNK:(c + 1) * KEY_CHUNK]
            vst_ref[0, g, c, HEAD_DIM:V_ROWS, :] = _ones_row_tail(KEY_CHUNK)
        for c in range(tm // t):
            vwt_ref[0, g, c, 0:HEAD_DIM, :] = vt[KV_WIDTH + g * HEAD_DIM:KV_WIDTH + (g + 1) * HEAD_DIM,
                                                 c * t:(c + 1) * t]
            vwt_ref[0, g, c, HEAD_DIM:V_ROWS, :] = _ones_row_tail(t)

    gl = _dot_nt(wgt_ref[...], hb) + bg_ref[...]
    gates = 1.0 / (1.0 + jnp.exp(-gl))
    assert tm == t
    gt_ref[...] = jnp.zeros_like(gt_ref)
    for br in range(3):
        for g in range(N_GROUPS):
            for hh in range(HEADS_PER_GROUP):
                row = br * N_HEADS + g * HEADS_PER_GROUP + hh
                gt_ref[0, g, 0, br:br + 1, hh * t:(hh + 1) * t] = gates[row:row + 1, :]


def _nsa_proj(x, sc, sh, w_in, b_gate):
    B, S, _ = x.shape
    tm = ROW_TILE
    t = NSA_T
    scale = LOG2E * HEAD_DIM ** -0.5
    cuts = [D_MODEL + n * KV_WIDTH for n in range(7)]
    wq, wkc, wvc, wks, wvs, wkw, wvw, wg = jnp.split(w_in, cuts, axis=1)
    wqt = (wq * scale).T.astype(BF16)
    padk = lambda w: jnp.pad(w.reshape(D_MODEL, N_GROUPS, HEAD_DIM),
                             ((0, 0), (0, 0), (0, LANES - HEAD_DIM))).reshape(D_MODEL, N_GROUPS * LANES)
    wnat = jnp.concatenate([wkc, wvc, padk(wks), padk(wkw)], axis=1).astype(BF16)
    wvt = jnp.concatenate([wvs, wvw], axis=1).T.astype(BF16)
    n_gate = 3 * N_HEADS
    wgt = wg.T.astype(BF16)
    bg = jnp.broadcast_to(b_gate.reshape(n_gate, 1), (n_gate, tm))

    full = lambda a: pl.BlockSpec(a.shape, lambda b, i: (0,) * a.ndim)
    nat_spec = pl.BlockSpec((1, N_GROUPS, tm, HEAD_DIM), lambda b, i: (b, 0, i, 0))
    pad_spec = pl.BlockSpec((1, N_GROUPS, tm, LANES), lambda b, i: (b, 0, i, 0))
    vt_spec = lambda n, rows: pl.BlockSpec((1, N_GROUPS, tm // n, rows, n), lambda b, i: (b, 0, i, 0, 0))
    return pl.pallas_call(
        _nsa_proj_kernel, name="nsa_proj",
        grid=(B, S // tm),
        in_specs=[
            pl.BlockSpec((1, tm, D_MODEL), lambda b, i: (b, i, 0)),
            pl.BlockSpec((1, 1, D_MODEL), lambda b, i: (b, 0, 0)),
            pl.BlockSpec((1, 1, D_MODEL), lambda b, i: (b, 0, 0)),
            full(wqt), full(wnat), full(wvt), full(wgt), full(bg),
        ],
        out_specs=[
            pl.BlockSpec((1, N_GROUPS, tm // t, HEAD_DIM, HEADS_PER_GROUP * t), lambda b, i: (b, 0, i, 0, 0)),
            nat_spec, nat_spec, pad_spec, pad_spec, vt_spec(KEY_CHUNK, V_ROWS), vt_spec(t, V_ROWS),
            pl.BlockSpec((1, N_GROUPS, 1, 8, HEADS_PER_GROUP * t), lambda b, i: (b, 0, i, 0, 0)),
        ],
        out_shape=[
            jax.ShapeDtypeStruct((B, N_GROUPS, S // t, HEAD_DIM, HEADS_PER_GROUP * t), BF16),
            jax.ShapeDtypeStruct((B, N_GROUPS, S, HEAD_DIM), BF16),
            jax.ShapeDtypeStruct((B, N_GROUPS, S, HEAD_DIM), BF16),
            jax.ShapeDtypeStruct((B, N_GROUPS, S, LANES), BF16),
            jax.ShapeDtypeStruct((B, N_GROUPS, S, LANES), BF16),
            jax.ShapeDtypeStruct((B, N_GROUPS, S // KEY_CHUNK, V_ROWS, KEY_CHUNK), BF16),
            jax.ShapeDtypeStruct((B, N_GROUPS, S // t, V_ROWS, t), BF16),
            jax.ShapeDtypeStruct((B, N_GROUPS, S // t, 8, HEADS_PER_GROUP * t), F32),
        ],
        compiler_params=_params("arbitrary", "arbitrary"),
    )(x, sc, sh, wqt, wnat, wvt, wgt, bg)


def _compress_kernel(tk_ref, tv_ref, pos_ref, w1_ref, w2k_ref, w2vt_ref, kc_ref, vct_ref):
    n = tk_ref.shape[2]
    half = CMP_STRIDE * HEAD_DIM

    def hidden(t_ref, idx):
        t16 = t_ref[0, 0]
        xa = (t16 + pos_ref[idx, 0:1, :]).astype(BF16)
        xb = (t16 + pos_ref[idx, 1:2, :]).astype(BF16)
        first = _dot(xa, w1_ref[idx, :half, :])
        second = _dot(xb, w1_ref[idx, half:, :])
        pre = first + pltpu.roll(second, n - 1, 0)
        return (pre / (1.0 + jnp.exp(-pre))).astype(BF16)

    kc = _dot(hidden(tk_ref, 0), w2k_ref[...])
    lane = lax.broadcasted_iota(jnp.int32, kc.shape, 1)
    ones = ((lane == HEAD_DIM) | (lane == HEAD_DIM + 1)).astype(F32)
    kc_ref[0, 0] = (kc + ones).astype(BF16)
    vct = _dot_nt(w2vt_ref[...], hidden(tv_ref, 1)).astype(BF16)
    for r in range(n // CMP_ROWS):
        vct_ref[0, 0, r] = vct[:, r * CMP_ROWS:(r + 1) * CMP_ROWS]


def _compress(kc, vc, cmp_pos, cmp_w1, cmp_w2):
    B, G, S, _ = kc.shape
    n = S // CMP_STRIDE
    width = CMP_STRIDE * HEAD_DIM
    tk = kc.reshape(B, G, n, width)
    tv = vc.reshape(B, G, n, width)
    pos = cmp_pos.reshape(2, 2, width)
    w1 = cmp_w1.astype(BF16)
    w2k = jnp.pad(cmp_w2[0], ((0, 0), (0, LANES - HEAD_DIM))).astype(BF16)
    w2vt = cmp_w2[1].T.astype(BF16)
    full = lambda a: pl.BlockSpec(a.shape, lambda b, g: (0,) * a.ndim)
    t_spec = pl.BlockSpec((1, 1, n, width), lambda b, g: (b, g, 0, 0))
    return pl.pallas_call(
        _compress_kernel, name="nsa_compress",
        grid=(B, G),
        in_specs=[t_spec, t_spec, full(pos), full(w1), full(w2k), full(w2vt)],
        out_specs=[
            pl.BlockSpec((1, 1, n, LANES), lambda b, g: (b, g, 0, 0)),
            pl.BlockSpec((1, 1, n // CMP_ROWS, HEAD_DIM, CMP_ROWS), lambda b, g: (b, g, 0, 0, 0)),
        ],
        out_shape=[
            jax.ShapeDtypeStruct((B, G, n, LANES), BF16),
            jax.ShapeDtypeStruct((B, G, n // CMP_ROWS, HEAD_DIM, CMP_ROWS), BF16),
        ],
        compiler_params=_params("arbitrary", "arbitrary"),
    )(tk, tv, pos, w1, w2k, w2vt)


def _bias_tiles_kernel(rb_ref, sel_ref, win_ref):
    hd = pl.program_id(0)
    far = rb_ref[FAR_BUCKET, hd]

    def dist(n, back):
        key = lax.broadcasted_iota(jnp.int32, (n, n), 0)
        qry = lax.broadcasted_iota(jnp.int32, (n, n), 1)
        return qry - key + back * n

    kc = sel_ref.shape[2]
    d_prev, d_diag = dist(kc, 1), dist(kc, 0)
    sel_ref[0, 0] = LOG2E * (_bucket_bias(d_prev, rb_ref, hd) - far)
    sel_ref[1, 0] = jnp.where(d_diag >= 0, LOG2E * (_bucket_bias(d_diag, rb_ref, hd) - far), NEG)
    t = win_ref.shape[2]
    n_back = win_ref.shape[0] - 1
    for r in range(n_back + 1):
        d = dist(t, n_back - r)
        win_ref[r, 0] = jnp.where((d >= 0) & (d < WINDOW), LOG2E * _bucket_bias(d, rb_ref, hd), NEG)


def _bias_tiles(rel_bias):
    t = NSA_T
    n_win = WINDOW // t + 1
    assert WINDOW % t == 0
    spec = lambda n, m: pl.BlockSpec((n, 1, m, m), lambda h: (0, h, 0, 0))
    return pl.pallas_call(
        _bias_tiles_kernel, name="rel_bias_tiles",
        grid=(N_HEADS,),
        in_specs=[pl.BlockSpec(memory_space=pltpu.SMEM)],
        out_specs=[spec(2, KEY_CHUNK), spec(n_win, t)],
        out_shape=[
            jax.ShapeDtypeStruct((2, N_HEADS, KEY_CHUNK, KEY_CHUNK), F32),
            jax.ShapeDtypeStruct((n_win, N_HEADS, t, t), F32),
        ],
        compiler_params=_params("arbitrary"),
    )(rel_bias)


def _far_bias_rows(rb_ref, g, width):
    t = width // HEADS_PER_GROUP
    lane = lax.broadcasted_iota(jnp.int32, (HEAD_DIM, width), 1)
    row = lax.broadcasted_iota(jnp.int32, (HEAD_DIM, width), 0)
    far = jnp.zeros((HEAD_DIM, width), F32)
    for hh in range(HEADS_PER_GROUP):
        far = jnp.where(lane >= hh * t, LOG2E * rb_ref[FAR_BUCKET, g * HEADS_PER_GROUP + hh], far)
    hi = far.astype(BF16).astype(F32)
    return jnp.where(row == 0, hi, jnp.where(row == 1, far - hi, 0.0)).astype(BF16)


def _nsa_cmp_kernel(rb_ref, qt_ref, kc_ref, vct_ref, at_ref, oc_ref, sb_ref, qp_ref, s_ref, imp_ref):
    g = pl.program_id(1)
    i = pl.program_id(2)
    width = qt_ref.shape[4]
    t = width // HEADS_PER_GROUP
    n = kc_ref.shape[2]
    nb = sb_ref.shape[3]
    t0 = i * t
    assert t & (t - 1) == 0 and n % CMP_ROWS == 0

    n_vis = (t0 + t - CMP_LEN) // CMP_STRIDE // CMP_ROWS + 1
    n_full = jnp.maximum((t0 - (CMP_LEN - 1)) // CMP_STRIDE + 1, 0) // CMP_ROWS
    rows_of = lambda r: pl.ds(pl.multiple_of(r * CMP_ROWS, CMP_ROWS), CMP_ROWS)

    def valid(r):
        blk = r * CMP_ROWS + lax.broadcasted_iota(jnp.int32, (CMP_ROWS, width), 0)
        qry = t0 + (lax.broadcasted_iota(jnp.int32, (CMP_ROWS, width), 1) & (t - 1))
        return qry >= blk * CMP_STRIDE + CMP_LEN - 1

    qp_ref[0:HEAD_DIM, :] = qt_ref[0, 0, 0]
    qp_ref[HEAD_DIM:LANES, :] = _far_bias_rows(rb_ref, g, width)

    def score_chunk(r, carry):
        s_ref[rows_of(r), :] = _dot(kc_ref[0, 0, rows_of(r), :], qp_ref[...])
        return carry

    lax.fori_loop(0, n_vis, score_chunk, 0)

    band = t // CMP_STRIDE + 16
    assert (FAR_DIST + CMP_LEN - 1) <= 16 * CMP_STRIDE and band <= n
    r0 = pl.multiple_of(jnp.clip(t0 // CMP_STRIDE - 16, 0, n - band), 8)
    blk_r = r0 + lax.broadcasted_iota(jnp.int32, (band, t), 0)
    qry_r = t0 + lax.broadcasted_iota(jnp.int32, (band, t), 1)
    dist_r = qry_r - (blk_r * CMP_STRIDE + CMP_LEN - 1)
    for hh in range(HEADS_PER_GROUP):
        hd = g * HEADS_PER_GROUP + hh
        corr = LOG2E * (_bucket_bias(dist_r, rb_ref, hd) - rb_ref[FAR_BUCKET, hd])
        s_ref[pl.ds(r0, band), hh * t:(hh + 1) * t] += corr

    def col_max(masked):
        def body(r, m):
            s = s_ref[rows_of(r), :]
            if masked:
                s = jnp.where(valid(r), s, NEG)
            return jnp.maximum(m, jnp.max(s, axis=0, keepdims=True))
        return body

    m = lax.fori_loop(0, n_full, col_max(False), jnp.full((1, width), NEG, F32))
    m = lax.fori_loop(n_full, n_vis, col_max(True), m)

    def exp_sum(masked):
        def body(r, l):
            p = jnp.exp2(s_ref[rows_of(r), :] - m)
            if masked:
                p = jnp.where(valid(r), p, 0.0)
            s_ref[rows_of(r), :] = p
            return l + jnp.sum(p, axis=0, keepdims=True)
        return body

    l = lax.fori_loop(0, n_full, exp_sum(False), jnp.zeros((1, width), F32))
    l = lax.fori_loop(n_full, n_vis, exp_sum(True), l)
    inv = 1.0 / jnp.maximum(l, TINY)

    oc_ref[0, 0, 0] = jnp.zeros((HEAD_DIM, width), F32)
    imp_ref[...] = jnp.zeros_like(imp_ref)

    def finish(r, carry):
        p = s_ref[rows_of(r), :] * inv
        oc_ref[0, 0, 0] += _dot(vct_ref[0, 0, r], p.astype(BF16))
        imp = p[:, 0:t]
        for hh in range(1, HEADS_PER_GROUP):
            imp = imp + p[:, hh * t:(hh + 1) * t]
        hi, mid, lo = _split3(imp)
        a = at_ref[...]
        band = pl.ds(pl.multiple_of(r * (CMP_ROWS * CMP_STRIDE // SEL_BLOCK), 8), a.shape[0])
        imp_ref[band, :] += _dot(a, hi) + _dot(a, mid) + _dot(a, lo)
        return carry

    lax.fori_loop(0, n_vis, finish, 0)

    def select_among(rows):
        def run():
            sblk = lax.broadcasted_iota(jnp.int32, (rows, t), 0)
            cur = (t0 + lax.broadcasted_iota(jnp.int32, (rows, t), 1)) // SEL_BLOCK
            forced = (sblk == 0) | (sblk == cur) | (sblk == cur - 1)
            is_cand = (sblk >= 1) & (sblk <= cur - 2)
            cand = jnp.where(is_cand, imp_ref[0:rows, :], -1.0)
            sblk_f = sblk.astype(F32)
            for _ in range(SEL_TOPK - N_FORCED):
                best = jnp.max(cand, axis=0, keepdims=True)
                first = jnp.min(jnp.where(cand == best, sblk_f, float(nb)), axis=0, keepdims=True)
                cand = jnp.where(sblk_f == first, -1.0, cand)
            chosen = forced | (is_cand & (cand < 0.0))
            sb_ref[0, 0, 0, 0:rows, :] = jnp.where(chosen, 0.0, NEG).astype(BF16)
            if rows < nb:
                sb_ref[0, 0, 0, rows:nb, :] = jnp.full((nb - rows, t), NEG, BF16)
        return run

    last_block = (t0 + t - 1) // SEL_BLOCK
    lax.switch(jnp.minimum(last_block // TOPK_ROWS, nb // TOPK_ROWS - 1),
               [select_among(rows) for rows in range(TOPK_ROWS, nb + 1, TOPK_ROWS)])


def _nsa_cmp(rel_bias, qt, kcmp, vcmp_t, n_sel):
    B, G, nq, _, width = qt.shape
    t = width // HEADS_PER_GROUP
    n = kcmp.shape[2]
    n_cmp = n - 1
    nb = -(-n_sel // LANES) * LANES
    R = SEL_BLOCK // CMP_STRIDE
    assert n_cmp == n - 1 and nb >= n_sel
    band_rows = -(-(CMP_ROWS // R + 1) // 8) * 8
    at = np.zeros((band_rows, CMP_ROWS), np.float32)
    for j in range(CMP_ROWS // R + 1):
        at[j, max(R * j - 1, 0):min(R * j + R, CMP_ROWS)] = 1.0
    at = jnp.asarray(at, BF16)
    return pl.pallas_call(
        _nsa_cmp_kernel, name="nsa_cmp_topk",
        grid=(B, G, nq),
        in_specs=[
            pl.BlockSpec(memory_space=pltpu.SMEM),
            pl.BlockSpec((1, 1, 1, HEAD_DIM, width), lambda b, g, i: (b, g, i, 0, 0)),
            pl.BlockSpec((1, 1, n, LANES), lambda b, g, i: (b, g, 0, 0)),
            pl.BlockSpec((1, 1, n // CMP_ROWS, HEAD_DIM, CMP_ROWS), lambda b, g, i: (b, g, 0, 0, 0)),
            pl.BlockSpec((band_rows, CMP_ROWS), lambda b, g, i: (0, 0)),
        ],
        out_specs=[
            pl.BlockSpec((1, 1, 1, HEAD_DIM, width), lambda b, g, i: (b, g, i, 0, 0)),
            pl.BlockSpec((1, 1, 1, nb, t), lambda b, g, i: (b, g, i, 0, 0)),
        ],
        out_shape=[
            jax.ShapeDtypeStruct((B, G, nq, HEAD_DIM, width), F32),
            jax.ShapeDtypeStruct((B, G, nq, nb, t), BF16),
        ],
        scratch_shapes=[pltpu.VMEM((LANES, width), BF16), pltpu.VMEM((n, width), F32),
                        pltpu.VMEM((nb + band_rows, t), F32)],
        compiler_params=_params("arbitrary", "arbitrary", "arbitrary"),
    )(rel_bias, qt, kcmp, vcmp_t, at)


def _nsa_win_kernel(qt_ref, *refs):
    n_win = (len(refs) - 5) // 2
    k_refs, v_refs = refs[:n_win], refs[n_win:2 * n_win]
    wb_ref, ow_ref, qp_ref, s_ref, p_ref = refs[2 * n_win:]
    i = pl.program_id(2)
    width = qt_ref.shape[4]
    t = width // HEADS_PER_GROUP
    qt = qt_ref[0, 0, 0]
    qp_ref[0:HEAD_DIM, :] = qt
    qp_ref[HEAD_DIM:LANES, :] = jnp.zeros_like(qt)
    m = None
    for c in range(n_win):
        k = k_refs[c][0, 0]
        missing = jnp.where(i - (n_win - 1 - c) < 0, NEG, 0.0)
        col_max = []
        for blk in range(width // QUERY_SUB):
            cols = slice(blk * QUERY_SUB, (blk + 1) * QUERY_SUB)
            hh, off = divmod(blk * QUERY_SUB, t)
            part = _dot(k, qp_ref[:, cols]) + wb_ref[c, hh, :, off:off + QUERY_SUB] + missing
            s_ref[c, :, cols] = part
            col_max.append(jnp.max(part, axis=0, keepdims=True))
        col_max = jnp.concatenate(col_max, axis=1)
        m = col_max if m is None else jnp.maximum(m, col_max)
    acc = jnp.zeros((V_ROWS, width), F32)
    for c in range(n_win):
        for blk in range(width // QUERY_SUB):
            cols = slice(blk * QUERY_SUB, (blk + 1) * QUERY_SUB)
            p_ref[c, :, cols] = jnp.exp2((s_ref[c, :, cols] - m[:, cols]).astype(BF16))
        acc = acc + _dot(v_refs[c][0, 0, 0], p_ref[c])
    ow_ref[0, 0, 0] = acc[:HEAD_DIM] / acc[HEAD_DIM:HEAD_DIM + 1]


def _nsa_window(qt, kw, vwt, win_bias):
    B, G, nq, _, width = qt.shape
    t = width // HEADS_PER_GROUP
    n_win = win_bias.shape[0]
    backs = list(range(n_win - 1, -1, -1))
    k_spec = lambda back: pl.BlockSpec((1, 1, t, LANES), lambda b, g, i: (b, g, jnp.maximum(i - back, 0), 0))
    v_spec = lambda back: pl.BlockSpec((1, 1, 1, V_ROWS, t),
                                       lambda b, g, i: (b, g, jnp.maximum(i - back, 0), 0, 0))
    return pl.pallas_call(
        _nsa_win_kernel, name="nsa_window",
        grid=(B, G, nq),
        in_specs=[pl.BlockSpec((1, 1, 1, HEAD_DIM, width), lambda b, g, i: (b, g, i, 0, 0))]
        + [k_spec(back) for back in backs] + [v_spec(back) for back in backs]
        + [pl.BlockSpec((n_win, HEADS_PER_GROUP, t, t), lambda b, g, i: (0, g, 0, 0))],
        out_specs=pl.BlockSpec((1, 1, 1, HEAD_DIM, width), lambda b, g, i: (b, g, i, 0, 0)),
        out_shape=jax.ShapeDtypeStruct((B, G, nq, HEAD_DIM, width), F32),
        scratch_shapes=[pltpu.VMEM((LANES, width), BF16),
                        pltpu.VMEM((n_win, t, width), F32), pltpu.VMEM((n_win, t, width), BF16)],
        compiler_params=_params("arbitrary", "arbitrary", "arbitrary"),
    )(qt, *([kw] * n_win), *([vwt] * n_win), win_bias)


def _nsa_sel_kernel(rb_ref, qt_ref, sb_ref, ks_ref, e_ref, vst_ref, cb_ref, oc_ref, ow_ref, gt_ref,
                    o_ref, qp_ref, s_buf, p_buf, acc_ref):
    g = pl.program_id(1)
    i = pl.program_id(2)
    width = qt_ref.shape[4]
    t = width // HEADS_PER_GROUP
    per_tile = t // KEY_CHUNK
    chunks_per_slab = LANES * SEL_BLOCK // KEY_CHUNK
    blocks_per_head = t // QUERY_SUB
    assert per_tile == 2 and t % QUERY_SUB == 0

    far_rows = _far_bias_rows(rb_ref, g, width)
    for slab in range(qp_ref.shape[0]):
        qp_ref[slab, 0:HEAD_DIM, :] = qt_ref[0, 0, 0]
        qp_ref[slab, HEAD_DIM:LANES, :] = far_rows
        sb = sb_ref[0, 0, 0, slab * LANES:(slab + 1) * LANES, :]
        qp_ref[slab, LANES:2 * LANES, :] = jnp.concatenate([sb] * HEADS_PER_GROUP, axis=1)

    def scores(c):
        c = jnp.maximum(c, 0)
        rows = pl.ds(pl.multiple_of(c * KEY_CHUNK, KEY_CHUNK), KEY_CHUNK)
        kp = jnp.concatenate([ks_ref[0, 0, rows, :], e_ref[rows, :]], axis=1)
        return lambda cols: _dot(kp, qp_ref[c // chunks_per_slab, :, cols])

    def values(c):
        return vst_ref[0, 0, jnp.maximum(c, 0)]

    def near_bias(r):
        def bias(c):
            hh, part = divmod(c, blocks_per_head)
            ahead = part * QUERY_SUB // KEY_CHUNK - (r - per_tile)
            if ahead == 0:
                return cb_ref[1, hh]
            if ahead == 1:
                return cb_ref[0, hh]
            if ahead < 0:
                return jnp.full((KEY_CHUNK, QUERY_SUB), NEG, F32)
            return None
        return bias

    tail = [(None, True), (None, True)] + [(near_bias(r), r < per_tile) for r in range(2 * per_tile)]
    o_sel = _flash_sweep(per_tile * jnp.minimum(i - 2, 0), jnp.maximum(i - 2, 0), scores, values, tail,
                         s_buf, p_buf, acc_ref)

    out = gt_ref[0, 0, 0, 0:1, :] * oc_ref[0, 0, 0] + gt_ref[0, 0, 0, 1:2, :] * o_sel \
        + gt_ref[0, 0, 0, 2:3, :] * ow_ref[0, 0, 0]
    for hh in range(HEADS_PER_GROUP):
        o_ref[0, hh * HEAD_DIM:(hh + 1) * HEAD_DIM, :] = out[:, hh * t:(hh + 1) * t].astype(o_ref.dtype)


def _nsa_select_combine(rel_bias, qt, sel_bias, ks, vst, sel_corr, oc, ow, gates):
    B, G, nq, _, width = qt.shape
    t = width // HEADS_PER_GROUP
    S = nq * t
    nb = sel_bias.shape[3]
    blocks = (np.arange(S) // SEL_BLOCK) % LANES
    onehot = jnp.asarray(blocks[:, None] == np.arange(LANES)[None, :], BF16)
    tile = pl.BlockSpec((1, 1, 1, HEAD_DIM, width), lambda b, g, i: (b, g, i, 0, 0))
    return pl.pallas_call(
        _nsa_sel_kernel, name="nsa_select",
        grid=(B, G, nq),
        in_specs=[
            pl.BlockSpec(memory_space=pltpu.SMEM),
            tile,
            pl.BlockSpec((1, 1, 1, nb, t), lambda b, g, i: (b, g, i, 0, 0)),
            pl.BlockSpec((1, 1, S, LANES), lambda b, g, i: (b, g, 0, 0)),
            pl.BlockSpec((S, LANES), lambda b, g, i: (0, 0)),
            pl.BlockSpec((1, 1, S // KEY_CHUNK, V_ROWS, KEY_CHUNK), lambda b, g, i: (b, g, 0, 0, 0)),
            pl.BlockSpec((2, HEADS_PER_GROUP, KEY_CHUNK, KEY_CHUNK), lambda b, g, i: (0, g, 0, 0)),
            tile, tile,
            pl.BlockSpec((1, 1, 1, 8, width), lambda b, g, i: (b, g, i, 0, 0)),
        ],
        out_specs=pl.BlockSpec((1, HEADS_PER_GROUP * HEAD_DIM, t), lambda b, g, i: (b, g, i)),
        out_shape=jax.ShapeDtypeStruct((B, D_MODEL, S), BF16),
        scratch_shapes=[pltpu.VMEM((nb // LANES, 2 * LANES, width), BF16),
                        pltpu.VMEM((2, KEY_CHUNK, width), F32), pltpu.VMEM((2, KEY_CHUNK, width), BF16),
                        pltpu.VMEM((V_ROWS, width), F32)],
        compiler_params=_params("arbitrary", "arbitrary", "arbitrary"),
    )(rel_bias, qt, sel_bias, ks, onehot, vst, sel_corr, oc, ow, gates)


def _nsa_attention_t(x, sc, sh, w_in, b_gate, cmp_pos, cmp_w1, cmp_w2, rel_bias):
    B, S, _ = x.shape
    t = NSA_T
    n_sel = S // SEL_BLOCK
    assert S % ROW_TILE == 0 and n_sel >= SEL_TOPK and S // CMP_STRIDE >= t // CMP_STRIDE + 16
    qt, kc, vc, ks, kw, vst, vwt, gt = _nsa_proj(x, sc, sh, w_in, b_gate)
    kcmp, vcmp_t = _compress(kc, vc, cmp_pos, cmp_w1, cmp_w2)
    sel_corr, win_bias = _bias_tiles(rel_bias)
    oc, sel_bias = _nsa_cmp(rel_bias, qt, kcmp, vcmp_t, n_sel)
    ow = _nsa_window(qt, kw, vwt, win_bias)
    return _nsa_select_combine(rel_bias, qt, sel_bias, ks, vst, sel_corr, oc, ow, gt)


def kernel(x, c, ada_w, ada_b, ln_g, ln_b, fox_w_in, fox_b_f, fox_w_out, nsa_w_in, nsa_b_gate,
           nsa_cmp_pos, nsa_cmp_w1, nsa_cmp_w2, nsa_w_out, rel_bias, ffn_w_in, ffn_w_out):
    B, S, _ = x.shape
    assert S % FOX_T == 0 and S % ROW_TILE == 0 and ROW_TILE == NSA_T
    mod = _modulation(c, ada_w, ada_b)
    for layer in range(DEPTH):
        sh_a, sc_a, g_a, sh_f, sc_f, g_f = [m.reshape(B, 1, D_MODEL) for m in jnp.split(mod[layer], 6, axis=-1)]
        j = layer // 2
        if layer % 2 == 0:
            kp, qt, vt = _fox_proj(x, sc_a, sh_a, fox_w_in[j], fox_b_f[j])
            attn_t = _fox_attention(qt, kp, vt)
            w_out = fox_w_out[j]
        else:
            attn_t = _nsa_attention_t(x, sc_a, sh_a, nsa_w_in[j], nsa_b_gate[j], nsa_cmp_pos[j],
                                      nsa_cmp_w1[j], nsa_cmp_w2[j], rel_bias)
            w_out = nsa_w_out[j]
        x = _block_tail(attn_t, w_out, x, g_a, (ln_g[layer, 0], ln_b[layer, 0]), sc_f, sh_f, g_f,
                        ffn_w_in[layer], ffn_w_out[layer], (ln_g[layer, 1], ln_b[layer, 1]))
    return x
```

```python
import math

import numpy as np
import jax
import jax.numpy as jnp
from jax import lax
from jax.experimental import pallas as pl
from jax.experimental.pallas import tpu as pltpu

F32 = jnp.float32
BF16 = jnp.bfloat16
HIGHEST = lax.Precision.HIGHEST

D_MODEL = 1024
HEAD_DIM = 64
N_HEADS = 16
N_GROUPS = 4
HEADS_PER_GROUP = 4
KV_WIDTH = N_GROUPS * HEAD_DIM
CMP_LEN = 32
CMP_STRIDE = 16
SEL_BLOCK = 64
SEL_TOPK = 16
WINDOW = 512
REL_BUCKETS = 32
FF_HIDDEN = 2816
DEPTH = 2
DN_ALPHA = (2 * DEPTH) ** 0.25
LN_EPS = 1e-5
NEG = -1e30
TINY = 1e-30

LANES = 128
SUBLANES = 8
VMEM_LIMIT = 56 * 1024 * 1024

ROW_TILE = 512
FOX_T = 1024
NSA_T = 512
MOD_TILE = 1536
N_FORCED = 3
NEAR_CMP_BLOCKS = 16

BUCKET_START = (0, 1, 2, 3, 4, 5, 6, 7, 8, 9, 10, 11, 12, 13, 14, 15,
                16, 19, 21, 24, 27, 31, 35, 40, 46, 52, 59, 67, 77, 87, 99, 113)
FAR_BUCKET = REL_BUCKETS - 1
FAR_DIST = BUCKET_START[FAR_BUCKET]

LOG2E = math.log2(math.e)
QUERY_SUB = 256
KEY_CHUNK = 256
PAIRS_PER_TRIP = 4
TOPK_ROWS = 64
CMP_ROWS = 128
BF16_ROWS = 2 * SUBLANES
V_ROWS = HEAD_DIM + BF16_ROWS
NT_DIMS = (((1,), (1,)), ((), ()))


def _params(*sem):
    return pltpu.CompilerParams(dimension_semantics=sem, vmem_limit_bytes=VMEM_LIMIT)


def _dot(a, b, **kw):
    return jnp.dot(a, b, preferred_element_type=F32, **kw)


def _dot_nt(a, b):
    return lax.dot_general(a, b, NT_DIMS, preferred_element_type=F32)


def _split3(v):
    hi = v.astype(BF16)
    r = v - hi.astype(F32)
    mid = r.astype(BF16)
    lo = (r - mid.astype(F32)).astype(BF16)
    return hi, mid, lo


def _layer_norm(z, g, b):
    mu = jnp.mean(z, axis=-1, keepdims=True)
    zc = z - mu
    var = jnp.mean(zc * zc, axis=-1, keepdims=True)
    return zc * lax.rsqrt(var + LN_EPS) * g + b


def _bucket_bias(dist, rb_ref, head):
    bias = jnp.full(dist.shape, rb_ref[0, head], F32)
    for k in range(1, REL_BUCKETS):
        bias = jnp.where(dist >= BUCKET_START[k], rb_ref[k, head], bias)
    return bias


def _mod_kernel(c_ref, w_ref, b_ref, o_ref):
    c = c_ref[...]
    cs = c / (1.0 + jnp.exp(-c))
    o_ref[0] = _dot(cs, w_ref[0], precision=HIGHEST) + b_ref[0]


def _modulation(c, ada_w, ada_b):
    B = c.shape[0]
    depth, _, n = ada_w.shape
    rows = SUBLANES
    c_pad = jnp.pad(c, ((0, rows - B), (0, 0)))
    tn = MOD_TILE
    out = pl.pallas_call(
        _mod_kernel, name="adaln_mod",
        grid=(depth, n // tn),
        in_specs=[
            pl.BlockSpec((rows, D_MODEL), lambda l, j: (0, 0)),
            pl.BlockSpec((1, D_MODEL, tn), lambda l, j: (l, 0, j)),
            pl.BlockSpec((1, 1, tn), lambda l, j: (l, 0, j)),
        ],
        out_specs=pl.BlockSpec((1, rows, tn), lambda l, j: (l, 0, j)),
        out_shape=jax.ShapeDtypeStruct((depth, rows, n), F32),
        compiler_params=_params("arbitrary", "arbitrary"),
    )(c_pad, ada_w, ada_b.reshape(depth, 1, n))
    return out[:, :B]


def _fox_proj_kernel(x_ref, sc_ref, sh_ref, wk_ref, wqt_ref, wvt_ref, wfh_ref, wfl_ref,
                     bf_ref, place_ref, kp_ref, qt_ref, vt_ref, carry_ref):
    tm = x_ref.shape[1]

    @pl.when(pl.program_id(1) == 0)
    def _():
        carry_ref[...] = jnp.zeros_like(carry_ref)

    h = x_ref[0] * (1.0 + sc_ref[0]) + sh_ref[0]
    hb = h.astype(BF16)
    hl = (h - hb.astype(F32)).astype(BF16)

    f = _dot(hb, wfh_ref[...]) + _dot(hl, wfh_ref[...]) + _dot(hb, wfl_ref[...])
    z = f + bf_ref[...]
    logf = jnp.minimum(z, 0.0) - jnp.log(1.0 + jnp.exp(-jnp.abs(z)))
    row = lax.broadcasted_iota(jnp.int32, (tm, tm), 0)
    col = lax.broadcasted_iota(jnp.int32, (tm, tm), 1)
    lower = (col <= row).astype(F32)
    cum = _dot(lower, logf, precision=HIGHEST) + carry_ref[0:1, :]
    carry_ref[...] = jnp.broadcast_to(cum[tm - 1:tm, :], carry_ref.shape)

    hi, mid, lo = [p.astype(F32) for p in _split3(-LOG2E * cum)]
    lane = lax.broadcasted_iota(jnp.int32, (tm, LANES), 1)
    pieces = jnp.where(lane < N_HEADS, hi, jnp.where(lane < 2 * N_HEADS, mid, lo)).astype(BF16)
    k = _dot(hb, wk_ref[...])
    kb = _dot(pieces, place_ref[...])
    qt = _dot_nt(wqt_ref[...], hb).astype(BF16)
    qrow = lax.broadcasted_iota(jnp.int32, (LANES - HEAD_DIM, tm), 0)
    q_ones = (qrow < 3).astype(F32).astype(BF16)
    vt = _dot_nt(wvt_ref[...], hb)
    for hd in range(N_HEADS):
        pair = slice((hd // 2) * LANES, (hd // 2 + 1) * LANES)
        own_half = (lane < HEAD_DIM) == (hd % 2 == 0)
        kp_ref[0, hd] = jnp.where(own_half, k[:, pair], kb[:, pair]).astype(BF16)
        q_rows = slice(0, HEAD_DIM) if hd % 2 == 0 else slice(HEAD_DIM, LANES)
        one_rows = slice(HEAD_DIM, LANES) if hd % 2 == 0 else slice(0, HEAD_DIM)
        qt_ref[0, hd, q_rows, :] = qt[hd * HEAD_DIM:(hd + 1) * HEAD_DIM, :]
        qt_ref[0, hd, one_rows, :] = q_ones
        for c in range(tm // KEY_CHUNK):
            vt_ref[0, hd, c, 0:HEAD_DIM, :] = vt[hd * HEAD_DIM:(hd + 1) * HEAD_DIM,
                                                 c * KEY_CHUNK:(c + 1) * KEY_CHUNK].astype(BF16)
            vt_ref[0, hd, c, HEAD_DIM:V_ROWS, :] = _ones_row_tail(KEY_CHUNK)


def _fox_proj(x, sc, sh, w_in, b_f):
    B, S, _ = x.shape
    tm = ROW_TILE
    scale = LOG2E * HEAD_DIM ** -0.5
    wq = (w_in[:, :D_MODEL] * scale).reshape(D_MODEL, N_HEADS, HEAD_DIM)
    wk = w_in[:, D_MODEL:2 * D_MODEL].astype(BF16)
    wv = w_in[:, 2 * D_MODEL:3 * D_MODEL]
    wf = w_in[:, 3 * D_MODEL:]
    wqt_p = wq.reshape(D_MODEL, N_HEADS * HEAD_DIM).T.astype(BF16)
    wvt = wv.T.astype(BF16)
    wf_rep = jnp.pad(jnp.tile(wf, (1, 3)), ((0, 0), (0, LANES - 3 * N_HEADS)))
    wf_hi = wf_rep.astype(BF16)
    wf_lo = (wf_rep - wf_hi.astype(F32)).astype(BF16)
    bf_rep = jnp.pad(jnp.tile(b_f, 3), (0, LANES - 3 * N_HEADS)).reshape(1, LANES)
    place = np.zeros((LANES, N_HEADS * HEAD_DIM), np.float32)
    for r in range(3):
        for hd in range(N_HEADS):
            place[r * N_HEADS + hd, (hd ^ 1) * HEAD_DIM + r] = 1.0
    place = jnp.asarray(place, BF16)

    full = lambda a: pl.BlockSpec(a.shape, lambda b, i: (0,) * a.ndim)
    return pl.pallas_call(
        _fox_proj_kernel, name="fox_proj",
        grid=(B, S // tm),
        in_specs=[
            pl.BlockSpec((1, tm, D_MODEL), lambda b, i: (b, i, 0)),
            pl.BlockSpec((1, 1, D_MODEL), lambda b, i: (b, 0, 0)),
            pl.BlockSpec((1, 1, D_MODEL), lambda b, i: (b, 0, 0)),
            full(wk), full(wqt_p), full(wvt), full(wf_hi), full(wf_lo), full(bf_rep), full(place),
        ],
        out_specs=[
            pl.BlockSpec((1, N_HEADS, tm, LANES), lambda b, i: (b, 0, i, 0)),
            pl.BlockSpec((1, N_HEADS, LANES, tm), lambda b, i: (b, 0, 0, i)),
            pl.BlockSpec((1, N_HEADS, tm // KEY_CHUNK, V_ROWS, KEY_CHUNK), lambda b, i: (b, 0, i, 0, 0)),
        ],
        out_shape=[
            jax.ShapeDtypeStruct((B, N_HEADS, S, LANES), BF16),
            jax.ShapeDtypeStruct((B, N_HEADS, LANES, S), BF16),
            jax.ShapeDtypeStruct((B, N_HEADS, S // KEY_CHUNK, V_ROWS, KEY_CHUNK), BF16),
        ],
        scratch_shapes=[pltpu.VMEM((SUBLANES, LANES), F32)],
        compiler_params=_params("arbitrary", "arbitrary"),
    )(x, sc, sh, wk, wqt_p, wvt, wf_hi, wf_lo, bf_rep, place)


def _ones_row_tail(width):
    row = lax.broadcasted_iota(jnp.int32, (V_ROWS - HEAD_DIM, width), 0)
    return (row == 0).astype(F32).astype(BF16)


def _flash_sweep(first, n_far_pairs, scores, values, tail, s_buf, p_buf, acc_ref, far_pairs_even=False):
    width = acc_ref.shape[1]
    n_blocks = width // QUERY_SUB
    assert len(tail) % 2 == 0 and width % QUERY_SUB == 0
    assert tail[0][0] is None and tail[1][0] is None
    cat = lambda parts: jnp.concatenate(parts, axis=1)

    def fetch(c, par, bias, may_be_missing):
        block_scores = scores(c)
        missing = jnp.where(c < 0, NEG, 0.0) if may_be_missing else None
        col_max = []
        for blk in range(n_blocks):
            cols = slice(blk * QUERY_SUB, (blk + 1) * QUERY_SUB)
            part = block_scores(cols)
            extra = None if bias is None else bias(blk)
            if extra is not None:
                part = part + extra
            if missing is not None:
                part = part + missing
            s_buf[par, :, cols] = part
            col_max.append(jnp.max(part, axis=0, keepdims=True))
        return cat(col_max)

    def step(j, par, stats, nxt=None):
        m, corr1, corr2, cmax = stats
        acc_ref[...] = acc_ref[...] * corr2 + _dot(values(j - 2), p_buf[par])
        m_new = jnp.maximum(m, cmax[par])
        for blk in range(n_blocks):
            cols = slice(blk * QUERY_SUB, (blk + 1) * QUERY_SUB)
            p_buf[par, :, cols] = jnp.exp2((s_buf[par, :, cols] - m_new[:, cols]).astype(BF16))
        if nxt is not None:
            cmax = tuple(fetch(j + 2, par, *nxt) if q == par else cmax[q] for q in range(2))
        return m_new, jnp.exp2(m - m_new), corr1, cmax

    far = (None, False)

    def pairs(j, n, stats):
        for k in range(n):
            stats = step(j + 2 * k + 1, 1, step(j + 2 * k, 0, stats, far), far)
        return stats

    p_buf[...] = jnp.zeros_like(p_buf)
    acc_ref[...] = jnp.zeros_like(acc_ref)
    ones = jnp.ones((1, width), F32)
    cmax = (fetch(first, 0, None, True), fetch(first + 1, 1, None, True))
    stats = (jnp.full((1, width), NEG, F32), ones, ones, cmax)
    start = first
    for n in (1, 2):
        if n == 1 and far_pairs_even:
            continue
        group = (n_far_pairs // n) % 2
        stats = lax.cond(group == 1, lambda st, j=start, n=n: pairs(j, n, st), lambda st: st, stats)
        start = start + 2 * n * group
    stats = lax.fori_loop(0, n_far_pairs // PAIRS_PER_TRIP,
                          lambda jj, st: pairs(start + 2 * PAIRS_PER_TRIP * jj, PAIRS_PER_TRIP, st), stats)
    last = first + 2 * n_far_pairs + len(tail)
    for r in range(len(tail)):
        stats = step(last - len(tail) + r, r % 2, stats, tail[r + 2] if r + 2 < len(tail) else None)
    _, corr1, corr2, _ = stats
    acc = acc_ref[...] * corr2 + _dot(values(last - 2), p_buf[0])
    acc = acc * corr1 + _dot(values(last - 1), p_buf[1])
    return acc[:HEAD_DIM] / acc[HEAD_DIM:HEAD_DIM + 1]


def _fox_attn_kernel(qt_ref, kp_ref, vt_ref, o_ref, s_buf, p_buf, acc_ref):
    i = pl.program_id(2)
    t = qt_ref.shape[3]
    per_tile = t // KEY_CHUNK

    def scores(c):
        rows = pl.ds(pl.multiple_of(jnp.maximum(c, 0) * KEY_CHUNK, KEY_CHUNK), KEY_CHUNK)
        k = kp_ref[0, 0, rows, :]
        return lambda cols: _dot(k, qt_ref[0, 0, :, cols])

    def values(c):
        return vt_ref[0, 0, jnp.maximum(c, 0)]

    key = lax.broadcasted_iota(jnp.int32, (KEY_CHUNK, QUERY_SUB), 0)
    qry = lax.broadcasted_iota(jnp.int32, (KEY_CHUNK, QUERY_SUB), 1)
    causal = lambda r: (lambda c: jnp.where(key + r * KEY_CHUNK <= qry + c * QUERY_SUB, 0.0, NEG))
    assert per_tile % 4 == 0
    tail = [(None, True)] * per_tile + [(causal(r), False) for r in range(per_tile)]
    out = _flash_sweep(per_tile * jnp.minimum(i - 1, 0), (per_tile // 2) * jnp.maximum(i - 1, 0),
                       scores, values, tail, s_buf, p_buf, acc_ref, far_pairs_even=True)
    o_ref[0] = out.astype(o_ref.dtype)


def _fox_attention(qt, kp, vt):
    B, H, S, _ = kp.shape
    t = FOX_T
    return pl.pallas_call(
        _fox_attn_kernel, name="fox_attn",
        grid=(B, H, S // t),
        in_specs=[
            pl.BlockSpec((1, 1, LANES, t), lambda b, h, i: (b, h, 0, i)),
            pl.BlockSpec((1, 1, S, LANES), lambda b, h, i: (b, h, 0, 0)),
            pl.BlockSpec((1, 1, S // KEY_CHUNK, V_ROWS, KEY_CHUNK), lambda b, h, i: (b, h, 0, 0, 0)),
        ],
        out_specs=pl.BlockSpec((1, HEAD_DIM, t), lambda b, h, i: (b, h, i)),
        out_shape=jax.ShapeDtypeStruct((B, H * HEAD_DIM, S), BF16),
        scratch_shapes=[pltpu.VMEM((2, KEY_CHUNK, t), F32), pltpu.VMEM((2, KEY_CHUNK, t), BF16),
                        pltpu.VMEM((V_ROWS, t), F32)],
        compiler_params=_params("arbitrary", "arbitrary", "arbitrary"),
    )(qt, kp, vt)


def _block_tail_kernel(at_ref, wo_ref, x_ref, ga_ref, g1_ref, b1_ref, sc_ref, sh_ref, gf_ref,
                       wa_ref, wb_ref, w2_ref, g2_ref, b2_ref, o_ref):
    y = lax.dot_general(at_ref[0], wo_ref[...], (((0,), (0,)), ((), ())), preferred_element_type=F32)
    x1 = _layer_norm(DN_ALPHA * x_ref[0] + (1.0 + ga_ref[0]) * y, g1_ref[...], b1_ref[...])
    hb = (x1 * (1.0 + sc_ref[0]) + sh_ref[0]).astype(BF16)
    a = _dot(hb, wa_ref[...])
    b = _dot(hb, wb_ref[...])
    u = (a / (1.0 + jnp.exp(-a)) * b).astype(BF16)
    z = DN_ALPHA * x1 + (1.0 + gf_ref[0]) * _dot(u, w2_ref[...])
    o_ref[0] = _layer_norm(z, g2_ref[...], b2_ref[...])


def _block_tail(attn_t, w_out, x, gate_a, ln1, sc, sh, gate_f, w_in, w_out_ffn, ln2):
    B, S, _ = x.shape
    tm = ROW_TILE
    w_in = w_in.astype(BF16)
    row = lambda r: r.reshape(1, -1)
    mod = pl.BlockSpec((1, 1, D_MODEL), lambda b, i: (b, 0, 0))
    vec = pl.BlockSpec((1, D_MODEL), lambda b, i: (0, 0))
    resident = lambda shape, index: pl.BlockSpec(shape, index, pipeline_mode=pl.Buffered(1))
    return pl.pallas_call(
        _block_tail_kernel, name="outproj_ffn_ln",
        grid=(B, S // tm),
        in_specs=[
            pl.BlockSpec((1, D_MODEL, tm), lambda b, i: (b, 0, i)),
            resident((D_MODEL, D_MODEL), lambda b, i: (0, 0)),
            pl.BlockSpec((1, tm, D_MODEL), lambda b, i: (b, i, 0)),
            mod, vec, vec,
            mod, mod, mod,
            resident((D_MODEL, FF_HIDDEN), lambda b, i: (0, 0)),
            resident((D_MODEL, FF_HIDDEN), lambda b, i: (0, 1)),
            resident((FF_HIDDEN, D_MODEL), lambda b, i: (0, 0)),
            vec, vec,
        ],
        out_specs=pl.BlockSpec((1, tm, D_MODEL), lambda b, i: (b, i, 0)),
        out_shape=jax.ShapeDtypeStruct((B, S, D_MODEL), F32),
        compiler_params=_params("arbitrary", "arbitrary"),
    )(attn_t, w_out.astype(BF16), x, gate_a, row(ln1[0]), row(ln1[1]), sc, sh, gate_f,
      w_in, w_in, w_out_ffn.astype(BF16), row(ln2[0]), row(ln2[1]))


def _nsa_proj_kernel(x_ref, sc_ref, sh_ref, wqt_ref, wnat_ref, wvt_ref, wgt_ref, bg_ref,
                     qt_ref, kc_ref, vc_ref, ks_ref, kw_ref, vst_ref, vwt_ref, gt_ref):
    tm = x_ref.shape[1]
    t = NSA_T
    hb = (x_ref[0] * (1.0 + sc_ref[0]) + sh_ref[0]).astype(BF16)

    qt = _dot_nt(wqt_ref[...], hb).astype(BF16)
    for g in range(N_GROUPS):
        for hh in range(HEADS_PER_GROUP):
            r0 = (g * HEADS_PER_GROUP + hh) * HEAD_DIM
            for c in range(tm // t):
                qt_ref[0, g, c, :, hh * t:(hh + 1) * t] = qt[r0:r0 + HEAD_DIM, c * t:(c + 1) * t]

    nat = _dot(hb, wnat_ref[...])
    lane = lax.broadcasted_iota(jnp.int32, (tm, LANES), 1)
    ones = ((lane == HEAD_DIM) | (lane == HEAD_DIM + 1)).astype(F32)
    off_ks = 2 * KV_WIDTH
    off_kw = off_ks + N_GROUPS * LANES
    for g in range(N_GROUPS):
        kc_ref[0, g] = nat[:, g * HEAD_DIM:(g + 1) * HEAD_DIM].astype(BF16)
        vc_ref[0, g] = nat[:, KV_WIDTH + g * HEAD_DIM:KV_WIDTH + (g + 1) * HEAD_DIM].astype(BF16)
        ks_ref[0, g] = (nat[:, off_ks + g * LANES:off_ks + (g + 1) * LANES] + ones).astype(BF16)
        kw_ref[0, g] = nat[:, off_kw + g * LANES:off_kw + (g + 1) * LANES].astype(BF16)

    vt = _dot_nt(wvt_ref[...], hb).astype(BF16)
    for g in range(N_GROUPS):
        for c in range(tm // KEY_CHUNK):
            vst_ref[0, g, c, 0:HEAD_DIM, :] = vt[g * HEAD_DIM:(g + 1) * HEAD_DIM, c * KEY_CHUNK:(c + 1) * KEY_CHUNK]
            vst_ref[0, g, c, HEAD_DIM:V_ROWS, :] = _ones_row_tail(KEY_CHUNK)
        for c in range(tm // t):
            vwt_ref[0, g, c, 0:HEAD_DIM, :] = vt[KV_WIDTH + g * HEAD_DIM:KV_WIDTH + (g + 1) * HEAD_DIM,
                                                 c * t:(c + 1) * t]
            vwt_ref[0, g, c, HEAD_DIM:V_ROWS, :] = _ones_row_tail(t)

    gl = _dot_nt(wgt_ref[...], hb) + bg_ref[...]
    gates = 1.0 / (1.0 + jnp.exp(-gl))
    assert tm == t
    gt_ref[...] = jnp.zeros_like(gt_ref)
    for br in range(3):
        for g in range(N_GROUPS):
            for hh in range(HEADS_PER_GROUP):
                row = br * N_HEADS + g * HEADS_PER_GROUP + hh
                gt_ref[0, g, 0, br:br + 1, hh * t:(hh + 1) * t] = gates[row:row + 1, :]


def _nsa_proj(x, sc, sh, w_in, b_gate):
    B, S, _ = x.shape
    tm = ROW_TILE
    t = NSA_T
    scale = LOG2E * HEAD_DIM ** -0.5
    cuts = [D_MODEL + n * KV_WIDTH for n in range(7)]
    wq, wkc, wvc, wks, wvs, wkw, wvw, wg = jnp.split(w_in, cuts, axis=1)
    wqt = (wq * scale).T.astype(BF16)
    padk = lambda w: jnp.pad(w.reshape(D_MODEL, N_GROUPS, HEAD_DIM),
                             ((0, 0), (0, 0), (0, LANES - HEAD_DIM))).reshape(D_MODEL, N_GROUPS * LANES)
    wnat = jnp.concatenate([wkc, wvc, padk(wks), padk(wkw)], axis=1).astype(BF16)
    wvt = jnp.concatenate([wvs, wvw], axis=1).T.astype(BF16)
    n_gate = 3 * N_HEADS
    wgt = wg.T.astype(BF16)
    bg = jnp.broadcast_to(b_gate.reshape(n_gate, 1), (n_gate, tm))

    full = lambda a: pl.BlockSpec(a.shape, lambda b, i: (0,) * a.ndim)
    nat_spec = pl.BlockSpec((1, N_GROUPS, tm, HEAD_DIM), lambda b, i: (b, 0, i, 0))
    pad_spec = pl.BlockSpec((1, N_GROUPS, tm, LANES), lambda b, i: (b, 0, i, 0))
    vt_spec = lambda n, rows: pl.BlockSpec((1, N_GROUPS, tm // n, rows, n), lambda b, i: (b, 0, i, 0, 0))
    return pl.pallas_call(
        _nsa_proj_kernel, name="nsa_proj",
        grid=(B, S // tm),
        in_specs=[
            pl.BlockSpec((1, tm, D_MODEL), lambda b, i: (b, i, 0)),
            pl.BlockSpec((1, 1, D_MODEL), lambda b, i: (b, 0, 0)),
            pl.BlockSpec((1, 1, D_MODEL), lambda b, i: (b, 0, 0)),
            full(wqt), full(wnat), full(wvt), full(wgt), full(bg),
        ],
        out_specs=[
            pl.BlockSpec((1, N_GROUPS, tm // t, HEAD_DIM, HEADS_PER_GROUP * t), lambda b, i: (b, 0, i, 0, 0)),
            nat_spec, nat_spec, pad_spec, pad_spec, vt_spec(KEY_CHUNK, V_ROWS), vt_spec(t, V_ROWS),
            pl.BlockSpec((1, N_GROUPS, 1, SUBLANES, HEADS_PER_GROUP * t), lambda b, i: (b, 0, i, 0, 0)),
        ],
        out_shape=[
            jax.ShapeDtypeStruct((B, N_GROUPS, S // t, HEAD_DIM, HEADS_PER_GROUP * t), BF16),
            jax.ShapeDtypeStruct((B, N_GROUPS, S, HEAD_DIM), BF16),
            jax.ShapeDtypeStruct((B, N_GROUPS, S, HEAD_DIM), BF16),
            jax.ShapeDtypeStruct((B, N_GROUPS, S, LANES), BF16),
            jax.ShapeDtypeStruct((B, N_GROUPS, S, LANES), BF16),
            jax.ShapeDtypeStruct((B, N_GROUPS, S // KEY_CHUNK, V_ROWS, KEY_CHUNK), BF16),
            jax.ShapeDtypeStruct((B, N_GROUPS, S // t, V_ROWS, t), BF16),
            jax.ShapeDtypeStruct((B, N_GROUPS, S // t, SUBLANES, HEADS_PER_GROUP * t), F32),
        ],
        compiler_params=_params("arbitrary", "arbitrary"),
    )(x, sc, sh, wqt, wnat, wvt, wgt, bg)


def _compress_kernel(tk_ref, tv_ref, pos_ref, w1_ref, w2k_ref, w2vt_ref, kc_ref, vct_ref):
    n = tk_ref.shape[2]
    half = CMP_STRIDE * HEAD_DIM

    def hidden(t_ref, idx):
        t16 = t_ref[0, 0]
        xa = (t16 + pos_ref[idx, 0:1, :]).astype(BF16)
        xb = (t16 + pos_ref[idx, 1:2, :]).astype(BF16)
        first = _dot(xa, w1_ref[idx, :half, :])
        second = _dot(xb, w1_ref[idx, half:, :])
        pre = first + pltpu.roll(second, n - 1, 0)
        return (pre / (1.0 + jnp.exp(-pre))).astype(BF16)

    kc = _dot(hidden(tk_ref, 0), w2k_ref[...])
    lane = lax.broadcasted_iota(jnp.int32, kc.shape, 1)
    ones = ((lane == HEAD_DIM) | (lane == HEAD_DIM + 1)).astype(F32)
    kc_ref[0, 0] = (kc + ones).astype(BF16)
    vct = _dot_nt(w2vt_ref[...], hidden(tv_ref, 1)).astype(BF16)
    for r in range(n // CMP_ROWS):
        vct_ref[0, 0, r] = vct[:, r * CMP_ROWS:(r + 1) * CMP_ROWS]


def _compress(kc, vc, cmp_pos, cmp_w1, cmp_w2):
    B, G, S, _ = kc.shape
    n = S // CMP_STRIDE
    width = CMP_STRIDE * HEAD_DIM
    tk = kc.reshape(B, G, n, width)
    tv = vc.reshape(B, G, n, width)
    pos = cmp_pos.reshape(2, 2, width)
    w1 = cmp_w1.astype(BF16)
    w2k = jnp.pad(cmp_w2[0], ((0, 0), (0, LANES - HEAD_DIM))).astype(BF16)
    w2vt = cmp_w2[1].T.astype(BF16)
    full = lambda a: pl.BlockSpec(a.shape, lambda b, g: (0,) * a.ndim)
    t_spec = pl.BlockSpec((1, 1, n, width), lambda b, g: (b, g, 0, 0))
    return pl.pallas_call(
        _compress_kernel, name="nsa_compress",
        grid=(B, G),
        in_specs=[t_spec, t_spec, full(pos), full(w1), full(w2k), full(w2vt)],
        out_specs=[
            pl.BlockSpec((1, 1, n, LANES), lambda b, g: (b, g, 0, 0)),
            pl.BlockSpec((1, 1, n // CMP_ROWS, HEAD_DIM, CMP_ROWS), lambda b, g: (b, g, 0, 0, 0)),
        ],
        out_shape=[
            jax.ShapeDtypeStruct((B, G, n, LANES), BF16),
            jax.ShapeDtypeStruct((B, G, n // CMP_ROWS, HEAD_DIM, CMP_ROWS), BF16),
        ],
        compiler_params=_params("arbitrary", "arbitrary"),
    )(tk, tv, pos, w1, w2k, w2vt)


def _bias_tiles_kernel(rb_ref, sel_ref, win_ref):
    hd = pl.program_id(0)
    far = rb_ref[FAR_BUCKET, hd]

    def dist(n, back):
        key = lax.broadcasted_iota(jnp.int32, (n, n), 0)
        qry = lax.broadcasted_iota(jnp.int32, (n, n), 1)
        return qry - key + back * n

    kc = sel_ref.shape[2]
    d_prev, d_diag = dist(kc, 1), dist(kc, 0)
    sel_ref[0, 0] = LOG2E * (_bucket_bias(d_prev, rb_ref, hd) - far)
    sel_ref[1, 0] = jnp.where(d_diag >= 0, LOG2E * (_bucket_bias(d_diag, rb_ref, hd) - far), NEG)
    t = win_ref.shape[2]
    n_back = win_ref.shape[0] - 1
    for r in range(n_back + 1):
        d = dist(t, n_back - r)
        win_ref[r, 0] = jnp.where((d >= 0) & (d < WINDOW), LOG2E * _bucket_bias(d, rb_ref, hd), NEG)


def _bias_tiles(rel_bias):
    t = NSA_T
    n_win = WINDOW // t + 1
    assert WINDOW % t == 0
    spec = lambda n, m: pl.BlockSpec((n, 1, m, m), lambda h: (0, h, 0, 0))
    return pl.pallas_call(
        _bias_tiles_kernel, name="rel_bias_tiles",
        grid=(N_HEADS,),
        in_specs=[pl.BlockSpec(memory_space=pltpu.SMEM)],
        out_specs=[spec(2, KEY_CHUNK), spec(n_win, t)],
        out_shape=[
            jax.ShapeDtypeStruct((2, N_HEADS, KEY_CHUNK, KEY_CHUNK), F32),
            jax.ShapeDtypeStruct((n_win, N_HEADS, t, t), F32),
        ],
        compiler_params=_params("arbitrary"),
    )(rel_bias)


def _far_bias_rows(rb_ref, g, width):
    t = width // HEADS_PER_GROUP
    lane = lax.broadcasted_iota(jnp.int32, (HEAD_DIM, width), 1)
    row = lax.broadcasted_iota(jnp.int32, (HEAD_DIM, width), 0)
    far = jnp.zeros((HEAD_DIM, width), F32)
    for hh in range(HEADS_PER_GROUP):
        far = jnp.where(lane >= hh * t, LOG2E * rb_ref[FAR_BUCKET, g * HEADS_PER_GROUP + hh], far)
    hi = far.astype(BF16).astype(F32)
    return jnp.where(row == 0, hi, jnp.where(row == 1, far - hi, 0.0)).astype(BF16)


def _nsa_cmp_kernel(rb_ref, qt_ref, kc_ref, vct_ref, at_ref, oc_ref, sb_ref, qp_ref, s_ref, imp_ref):
    g = pl.program_id(1)
    i = pl.program_id(2)
    width = qt_ref.shape[4]
    t = width // HEADS_PER_GROUP
    n = kc_ref.shape[2]
    nb = sb_ref.shape[3]
    t0 = i * t
    assert t & (t - 1) == 0 and n % CMP_ROWS == 0

    n_vis = (t0 + t - CMP_LEN) // CMP_STRIDE // CMP_ROWS + 1
    n_full = jnp.maximum((t0 - (CMP_LEN - 1)) // CMP_STRIDE + 1, 0) // CMP_ROWS
    rows_of = lambda r: pl.ds(pl.multiple_of(r * CMP_ROWS, CMP_ROWS), CMP_ROWS)

    def valid(r):
        blk = r * CMP_ROWS + lax.broadcasted_iota(jnp.int32, (CMP_ROWS, width), 0)
        qry = t0 + (lax.broadcasted_iota(jnp.int32, (CMP_ROWS, width), 1) & (t - 1))
        return qry >= blk * CMP_STRIDE + CMP_LEN - 1

    qp_ref[0:HEAD_DIM, :] = qt_ref[0, 0, 0]
    qp_ref[HEAD_DIM:LANES, :] = _far_bias_rows(rb_ref, g, width)

    def score_chunk(r, carry):
        s_ref[rows_of(r), :] = _dot(kc_ref[0, 0, rows_of(r), :], qp_ref[...])
        return carry

    lax.fori_loop(0, n_vis, score_chunk, 0)

    band = t // CMP_STRIDE + NEAR_CMP_BLOCKS
    assert (FAR_DIST + CMP_LEN - 1) <= NEAR_CMP_BLOCKS * CMP_STRIDE and band <= n
    r0 = pl.multiple_of(jnp.clip(t0 // CMP_STRIDE - NEAR_CMP_BLOCKS, 0, n - band), SUBLANES)
    blk_r = r0 + lax.broadcasted_iota(jnp.int32, (band, t), 0)
    qry_r = t0 + lax.broadcasted_iota(jnp.int32, (band, t), 1)
    dist_r = qry_r - (blk_r * CMP_STRIDE + CMP_LEN - 1)
    for hh in range(HEADS_PER_GROUP):
        hd = g * HEADS_PER_GROUP + hh
        corr = LOG2E * (_bucket_bias(dist_r, rb_ref, hd) - rb_ref[FAR_BUCKET, hd])
        s_ref[pl.ds(r0, band), hh * t:(hh + 1) * t] += corr

    def col_max(masked):
        def body(r, m):
            s = s_ref[rows_of(r), :]
            if masked:
                s = jnp.where(valid(r), s, NEG)
            return jnp.maximum(m, jnp.max(s, axis=0, keepdims=True))
        return body

    m = lax.fori_loop(0, n_full, col_max(False), jnp.full((1, width), NEG, F32))
    m = lax.fori_loop(n_full, n_vis, col_max(True), m)

    def exp_sum(masked):
        def body(r, l):
            p = jnp.exp2(s_ref[rows_of(r), :] - m)
            if masked:
                p = jnp.where(valid(r), p, 0.0)
            s_ref[rows_of(r), :] = p
            return l + jnp.sum(p, axis=0, keepdims=True)
        return body

    l = lax.fori_loop(0, n_full, exp_sum(False), jnp.zeros((1, width), F32))
    l = lax.fori_loop(n_full, n_vis, exp_sum(True), l)
    inv = 1.0 / jnp.maximum(l, TINY)

    oc_ref[0, 0, 0] = jnp.zeros((HEAD_DIM, width), F32)
    imp_ref[...] = jnp.zeros_like(imp_ref)

    def finish(r, carry):
        p = s_ref[rows_of(r), :] * inv
        oc_ref[0, 0, 0] += _dot(vct_ref[0, 0, r], p.astype(BF16))
        imp = p[:, 0:t]
        for hh in range(1, HEADS_PER_GROUP):
            imp = imp + p[:, hh * t:(hh + 1) * t]
        hi, mid, lo = _split3(imp)
        a = at_ref[...]
        band = pl.ds(pl.multiple_of(r * (CMP_ROWS * CMP_STRIDE // SEL_BLOCK), 8), a.shape[0])
        imp_ref[band, :] += _dot(a, hi) + _dot(a, mid) + _dot(a, lo)
        return carry

    lax.fori_loop(0, n_vis, finish, 0)

    def select_among(rows):
        def run():
            sblk = lax.broadcasted_iota(jnp.int32, (rows, t), 0)
            cur = (t0 + lax.broadcasted_iota(jnp.int32, (rows, t), 1)) // SEL_BLOCK
            forced = (sblk == 0) | (sblk == cur) | (sblk == cur - 1)
            is_cand = (sblk >= 1) & (sblk <= cur - 2)
            cand = jnp.where(is_cand, imp_ref[0:rows, :], -1.0)
            sblk_f = sblk.astype(F32)
            for _ in range(SEL_TOPK - N_FORCED):
                best = jnp.max(cand, axis=0, keepdims=True)
                first = jnp.min(jnp.where(cand == best, sblk_f, float(nb)), axis=0, keepdims=True)
                cand = jnp.where(sblk_f == first, -1.0, cand)
            chosen = forced | (is_cand & (cand < 0.0))
            sb_ref[0, 0, 0, 0:rows, :] = jnp.where(chosen, 0.0, NEG).astype(BF16)
            if rows < nb:
                sb_ref[0, 0, 0, rows:nb, :] = jnp.full((nb - rows, t), NEG, BF16)
        return run

    last_block = (t0 + t - 1) // SEL_BLOCK
    lax.switch(jnp.minimum(last_block // TOPK_ROWS, nb // TOPK_ROWS - 1),
               [select_among(rows) for rows in range(TOPK_ROWS, nb + 1, TOPK_ROWS)])


def _nsa_cmp(rel_bias, qt, kcmp, vcmp_t, n_sel):
    B, G, nq, _, width = qt.shape
    t = width // HEADS_PER_GROUP
    n = kcmp.shape[2]
    n_cmp = n - 1
    nb = -(-n_sel // LANES) * LANES
    R = SEL_BLOCK // CMP_STRIDE
    assert n_cmp == n - 1 and nb >= n_sel
    band_rows = -(-(CMP_ROWS // R + 1) // 8) * 8
    at = np.zeros((band_rows, CMP_ROWS), np.float32)
    for j in range(CMP_ROWS // R + 1):
        at[j, max(R * j - 1, 0):min(R * j + R, CMP_ROWS)] = 1.0
    at = jnp.asarray(at, BF16)
    return pl.pallas_call(
        _nsa_cmp_kernel, name="nsa_cmp_topk",
        grid=(B, G, nq),
        in_specs=[
            pl.BlockSpec(memory_space=pltpu.SMEM),
            pl.BlockSpec((1, 1, 1, HEAD_DIM, width), lambda b, g, i: (b, g, i, 0, 0)),
            pl.BlockSpec((1, 1, n, LANES), lambda b, g, i: (b, g, 0, 0)),
            pl.BlockSpec((1, 1, n // CMP_ROWS, HEAD_DIM, CMP_ROWS), lambda b, g, i: (b, g, 0, 0, 0)),
            pl.BlockSpec((band_rows, CMP_ROWS), lambda b, g, i: (0, 0)),
        ],
        out_specs=[
            pl.BlockSpec((1, 1, 1, HEAD_DIM, width), lambda b, g, i: (b, g, i, 0, 0)),
            pl.BlockSpec((1, 1, 1, nb, t), lambda b, g, i: (b, g, i, 0, 0)),
        ],
        out_shape=[
            jax.ShapeDtypeStruct((B, G, nq, HEAD_DIM, width), F32),
            jax.ShapeDtypeStruct((B, G, nq, nb, t), BF16),
        ],
        scratch_shapes=[pltpu.VMEM((LANES, width), BF16), pltpu.VMEM((n, width), F32),
                        pltpu.VMEM((nb + band_rows, t), F32)],
        compiler_params=_params("arbitrary", "arbitrary", "arbitrary"),
    )(rel_bias, qt, kcmp, vcmp_t, at)


def _nsa_win_kernel(qt_ref, *refs):
    n_win = (len(refs) - 5) // 2
    k_refs, v_refs = refs[:n_win], refs[n_win:2 * n_win]
    wb_ref, ow_ref, qp_ref, s_ref, p_ref = refs[2 * n_win:]
    i = pl.program_id(2)
    width = qt_ref.shape[4]
    t = width // HEADS_PER_GROUP
    qt = qt_ref[0, 0, 0]
    qp_ref[0:HEAD_DIM, :] = qt
    qp_ref[HEAD_DIM:LANES, :] = jnp.zeros_like(qt)
    m = None
    for c in range(n_win):
        k = k_refs[c][0, 0]
        missing = jnp.where(i - (n_win - 1 - c) < 0, NEG, 0.0)
        col_max = []
        for blk in range(width // QUERY_SUB):
            cols = slice(blk * QUERY_SUB, (blk + 1) * QUERY_SUB)
            hh, off = divmod(blk * QUERY_SUB, t)
            part = _dot(k, qp_ref[:, cols]) + wb_ref[c, hh, :, off:off + QUERY_SUB] + missing
            s_ref[c, :, cols] = part
            col_max.append(jnp.max(part, axis=0, keepdims=True))
        col_max = jnp.concatenate(col_max, axis=1)
        m = col_max if m is None else jnp.maximum(m, col_max)
    acc = jnp.zeros((V_ROWS, width), F32)
    for c in range(n_win):
        for blk in range(width // QUERY_SUB):
            cols = slice(blk * QUERY_SUB, (blk + 1) * QUERY_SUB)
            p_ref[c, :, cols] = jnp.exp2((s_ref[c, :, cols] - m[:, cols]).astype(BF16))
        acc = acc + _dot(v_refs[c][0, 0, 0], p_ref[c])
    ow_ref[0, 0, 0] = acc[:HEAD_DIM] / acc[HEAD_DIM:HEAD_DIM + 1]


def _nsa_window(qt, kw, vwt, win_bias):
    B, G, nq, _, width = qt.shape
    t = width // HEADS_PER_GROUP
    n_win = win_bias.shape[0]
    backs = list(range(n_win - 1, -1, -1))
    k_spec = lambda back: pl.BlockSpec((1, 1, t, LANES), lambda b, g, i: (b, g, jnp.maximum(i - back, 0), 0))
    v_spec = lambda back: pl.BlockSpec((1, 1, 1, V_ROWS, t),
                                       lambda b, g, i: (b, g, jnp.maximum(i - back, 0), 0, 0))
    return pl.pallas_call(
        _nsa_win_kernel, name="nsa_window",
        grid=(B, G, nq),
        in_specs=[pl.BlockSpec((1, 1, 1, HEAD_DIM, width), lambda b, g, i: (b, g, i, 0, 0))]
        + [k_spec(back) for back in backs] + [v_spec(back) for back in backs]
        + [pl.BlockSpec((n_win, HEADS_PER_GROUP, t, t), lambda b, g, i: (0, g, 0, 0))],
        out_specs=pl.BlockSpec((1, 1, 1, HEAD_DIM, width), lambda b, g, i: (b, g, i, 0, 0)),
        out_shape=jax.ShapeDtypeStruct((B, G, nq, HEAD_DIM, width), F32),
        scratch_shapes=[pltpu.VMEM((LANES, width), BF16),
                        pltpu.VMEM((n_win, t, width), F32), pltpu.VMEM((n_win, t, width), BF16)],
        compiler_params=_params("arbitrary", "arbitrary", "arbitrary"),
    )(qt, *([kw] * n_win), *([vwt] * n_win), win_bias)


def _nsa_sel_kernel(rb_ref, qt_ref, sb_ref, ks_ref, e_ref, vst_ref, cb_ref, oc_ref, ow_ref, gt_ref,
                    o_ref, qp_ref, s_buf, p_buf, acc_ref):
    g = pl.program_id(1)
    i = pl.program_id(2)
    width = qt_ref.shape[4]
    t = width // HEADS_PER_GROUP
    per_tile = t // KEY_CHUNK
    chunks_per_slab = LANES * SEL_BLOCK // KEY_CHUNK
    blocks_per_head = t // QUERY_SUB
    assert per_tile == 2 and t % QUERY_SUB == 0

    far_rows = _far_bias_rows(rb_ref, g, width)
    for slab in range(qp_ref.shape[0]):
        qp_ref[slab, 0:HEAD_DIM, :] = qt_ref[0, 0, 0]
        qp_ref[slab, HEAD_DIM:LANES, :] = far_rows
        sb = sb_ref[0, 0, 0, slab * LANES:(slab + 1) * LANES, :]
        qp_ref[slab, LANES:2 * LANES, :] = jnp.concatenate([sb] * HEADS_PER_GROUP, axis=1)

    def scores(c):
        c = jnp.maximum(c, 0)
        rows = pl.ds(pl.multiple_of(c * KEY_CHUNK, KEY_CHUNK), KEY_CHUNK)
        kp = jnp.concatenate([ks_ref[0, 0, rows, :], e_ref[rows, :]], axis=1)
        return lambda cols: _dot(kp, qp_ref[c // chunks_per_slab, :, cols])

    def values(c):
        return vst_ref[0, 0, jnp.maximum(c, 0)]

    def near_bias(r):
        def bias(c):
            hh, part = divmod(c, blocks_per_head)
            ahead = part * QUERY_SUB // KEY_CHUNK - (r - per_tile)
            if ahead == 0:
                return cb_ref[1, hh]
            if ahead == 1:
                return cb_ref[0, hh]
            if ahead < 0:
                return jnp.full((KEY_CHUNK, QUERY_SUB), NEG, F32)
            return None
        return bias

    tail = [(None, True), (None, True)] + [(near_bias(r), r < per_tile) for r in range(2 * per_tile)]
    o_sel = _flash_sweep(per_tile * jnp.minimum(i - 2, 0), jnp.maximum(i - 2, 0), scores, values, tail,
                         s_buf, p_buf, acc_ref)

    out = gt_ref[0, 0, 0, 0:1, :] * oc_ref[0, 0, 0] + gt_ref[0, 0, 0, 1:2, :] * o_sel \
        + gt_ref[0, 0, 0, 2:3, :] * ow_ref[0, 0, 0]
    for hh in range(HEADS_PER_GROUP):
        o_ref[0, hh * HEAD_DIM:(hh + 1) * HEAD_DIM, :] = out[:, hh * t:(hh + 1) * t].astype(o_ref.dtype)


def _nsa_select_combine(rel_bias, qt, sel_bias, ks, vst, sel_corr, oc, ow, gates):
    B, G, nq, _, width = qt.shape
    t = width // HEADS_PER_GROUP
    S = nq * t
    nb = sel_bias.shape[3]
    blocks = (np.arange(S) // SEL_BLOCK) % LANES
    onehot = jnp.asarray(blocks[:, None] == np.arange(LANES)[None, :], BF16)
    tile = pl.BlockSpec((1, 1, 1, HEAD_DIM, width), lambda b, g, i: (b, g, i, 0, 0))
    return pl.pallas_call(
        _nsa_sel_kernel, name="nsa_select",
        grid=(B, G, nq),
        in_specs=[
            pl.BlockSpec(memory_space=pltpu.SMEM),
            tile,
            pl.BlockSpec((1, 1, 1, nb, t), lambda b, g, i: (b, g, i, 0, 0)),
            pl.BlockSpec((1, 1, S, LANES), lambda b, g, i: (b, g, 0, 0)),
            pl.BlockSpec((S, LANES), lambda b, g, i: (0, 0)),
            pl.BlockSpec((1, 1, S // KEY_CHUNK, V_ROWS, KEY_CHUNK), lambda b, g, i: (b, g, 0, 0, 0)),
            pl.BlockSpec((2, HEADS_PER_GROUP, KEY_CHUNK, KEY_CHUNK), lambda b, g, i: (0, g, 0, 0)),
            tile, tile,
            pl.BlockSpec((1, 1, 1, SUBLANES, width), lambda b, g, i: (b, g, i, 0, 0)),
        ],
        out_specs=pl.BlockSpec((1, HEADS_PER_GROUP * HEAD_DIM, t), lambda b, g, i: (b, g, i)),
        out_shape=jax.ShapeDtypeStruct((B, D_MODEL, S), BF16),
        scratch_shapes=[pltpu.VMEM((nb // LANES, 2 * LANES, width), BF16),
                        pltpu.VMEM((2, KEY_CHUNK, width), F32), pltpu.VMEM((2, KEY_CHUNK, width), BF16),
                        pltpu.VMEM((V_ROWS, width), F32)],
        compiler_params=_params("arbitrary", "arbitrary", "arbitrary"),
    )(rel_bias, qt, sel_bias, ks, onehot, vst, sel_corr, oc, ow, gates)


def _nsa_attention_t(x, sc, sh, w_in, b_gate, cmp_pos, cmp_w1, cmp_w2, rel_bias):
    B, S, _ = x.shape
    t = NSA_T
    n_sel = S // SEL_BLOCK
    assert S % ROW_TILE == 0 and n_sel >= SEL_TOPK and S // CMP_STRIDE >= t // CMP_STRIDE + NEAR_CMP_BLOCKS
    qt, kc, vc, ks, kw, vst, vwt, gt = _nsa_proj(x, sc, sh, w_in, b_gate)
    kcmp, vcmp_t = _compress(kc, vc, cmp_pos, cmp_w1, cmp_w2)
    sel_corr, win_bias = _bias_tiles(rel_bias)
    oc, sel_bias = _nsa_cmp(rel_bias, qt, kcmp, vcmp_t, n_sel)
    ow = _nsa_window(qt, kw, vwt, win_bias)
    return _nsa_select_combine(rel_bias, qt, sel_bias, ks, vst, sel_corr, oc, ow, gt)


def kernel(x, c, ada_w, ada_b, ln_g, ln_b, fox_w_in, fox_b_f, fox_w_out, nsa_w_in, nsa_b_gate,
           nsa_cmp_pos, nsa_cmp_w1, nsa_cmp_w2, nsa_w_out, rel_bias, ffn_w_in, ffn_w_out):
    B, S, _ = x.shape
    assert S % FOX_T == 0 and S % ROW_TILE == 0 and ROW_TILE == NSA_T
    mod = _modulation(c, ada_w, ada_b)
    for layer in range(DEPTH):
        sh_a, sc_a, g_a, sh_f, sc_f, g_f = [m.reshape(B, 1, D_MODEL) for m in jnp.split(mod[layer], 6, axis=-1)]
        j = layer // 2
        if layer % 2 == 0:
            kp, qt, vt = _fox_proj(x, sc_a, sh_a, fox_w_in[j], fox_b_f[j])
            attn_t = _fox_attention(qt, kp, vt)
            w_out = fox_w_out[j]
        else:
            attn_t = _nsa_attention_t(x, sc_a, sh_a, nsa_w_in[j], nsa_b_gate[j], nsa_cmp_pos[j],
                                      nsa_cmp_w1[j], nsa_cmp_w2[j], rel_bias)
            w_out = nsa_w_out[j]
        x = _block_tail(attn_t, w_out, x, g_a, (ln_g[layer, 0], ln_b[layer, 0]), sc_f, sh_f, g_f,
                        ffn_w_in[layer], ffn_w_out[layer], (ln_g[layer, 1], ln_b[layer, 1]))
    return x
```

```python
import math

import numpy as np
import jax
import jax.numpy as jnp
from jax import lax
from jax.experimental import pallas as pl
from jax.experimental.pallas import tpu as pltpu

F32 = jnp.float32
BF16 = jnp.bfloat16
HIGHEST = lax.Precision.HIGHEST

D_MODEL = 1024
HEAD_DIM = 64
N_HEADS = 16
N_GROUPS = 4
HEADS_PER_GROUP = 4
KV_WIDTH = N_GROUPS * HEAD_DIM
CMP_LEN = 32
CMP_STRIDE = 16
SEL_BLOCK = 64
SEL_TOPK = 16
WINDOW = 512
REL_BUCKETS = 32
FF_HIDDEN = 2816
DEPTH = 2
DN_ALPHA = (2 * DEPTH) ** 0.25
LN_EPS = 1e-5
NEG = -1e30
TINY = 1e-30

LANES = 128
SUBLANES = 8
VMEM_LIMIT = 56 * 1024 * 1024

ROW_TILE = 512
FOX_T = 1024
NSA_T = 512
MOD_TILE = 1536
N_FORCED = 3
NEAR_CMP_BLOCKS = 16

BUCKET_START = (0, 1, 2, 3, 4, 5, 6, 7, 8, 9, 10, 11, 12, 13, 14, 15,
                16, 19, 21, 24, 27, 31, 35, 40, 46, 52, 59, 67, 77, 87, 99, 113)
FAR_BUCKET = REL_BUCKETS - 1
FAR_DIST = BUCKET_START[FAR_BUCKET]

LOG2E = math.log2(math.e)
QUERY_SUB = 256
KEY_CHUNK = 256
PAIRS_PER_TRIP = 4
TOPK_ROWS = 64
CMP_ROWS = 128
BF16_ROWS = 2 * SUBLANES
V_ROWS = HEAD_DIM + BF16_ROWS
NT_DIMS = (((1,), (1,)), ((), ()))


def _params(*sem):
    return pltpu.CompilerParams(dimension_semantics=sem, vmem_limit_bytes=VMEM_LIMIT)


def _dot(a, b, **kw):
    return jnp.dot(a, b, preferred_element_type=F32, **kw)


def _dot_nt(a, b):
    return lax.dot_general(a, b, NT_DIMS, preferred_element_type=F32)


def _split3(v):
    hi = v.astype(BF16)
    r = v - hi.astype(F32)
    mid = r.astype(BF16)
    lo = (r - mid.astype(F32)).astype(BF16)
    return hi, mid, lo


def _layer_norm(z, g, b):
    mu = jnp.mean(z, axis=-1, keepdims=True)
    zc = z - mu
    var = jnp.mean(zc * zc, axis=-1, keepdims=True)
    return zc * lax.rsqrt(var + LN_EPS) * g + b


def _bucket_bias(dist, rb_ref, head):
    bias = jnp.full(dist.shape, rb_ref[0, head], F32)
    for k in range(1, REL_BUCKETS):
        bias = jnp.where(dist >= BUCKET_START[k], rb_ref[k, head], bias)
    return bias


def _mod_kernel(c_ref, w_ref, b_ref, o_ref):
    c = c_ref[...]
    cs = c / (1.0 + jnp.exp(-c))
    o_ref[0] = _dot(cs, w_ref[0], precision=HIGHEST) + b_ref[0]


def _modulation(c, ada_w, ada_b):
    B = c.shape[0]
    depth, _, n = ada_w.shape
    rows = SUBLANES
    c_pad = jnp.pad(c, ((0, rows - B), (0, 0)))
    tn = MOD_TILE
    out = pl.pallas_call(
        _mod_kernel, name="adaln_mod",
        grid=(depth, n // tn),
        in_specs=[
            pl.BlockSpec((rows, D_MODEL), lambda l, j: (0, 0)),
            pl.BlockSpec((1, D_MODEL, tn), lambda l, j: (l, 0, j)),
            pl.BlockSpec((1, 1, tn), lambda l, j: (l, 0, j)),
        ],
        out_specs=pl.BlockSpec((1, rows, tn), lambda l, j: (l, 0, j)),
        out_shape=jax.ShapeDtypeStruct((depth, rows, n), F32),
        compiler_params=_params("arbitrary", "arbitrary"),
    )(c_pad, ada_w, ada_b.reshape(depth, 1, n))
    return out[:, :B]


def _fox_proj_kernel(x_ref, sc_ref, sh_ref, wk_ref, wqt_ref, wvt_ref, wfh_ref, wfl_ref,
                     bf_ref, place_ref, kp_ref, qt_ref, vt_ref, carry_ref):
    tm = x_ref.shape[1]

    @pl.when(pl.program_id(1) == 0)
    def _():
        carry_ref[...] = jnp.zeros_like(carry_ref)

    h = x_ref[0] * (1.0 + sc_ref[0]) + sh_ref[0]
    hb = h.astype(BF16)
    hl = (h - hb.astype(F32)).astype(BF16)

    f = _dot(hb, wfh_ref[...]) + _dot(hl, wfh_ref[...]) + _dot(hb, wfl_ref[...])
    z = f + bf_ref[...]
    logf = jnp.minimum(z, 0.0) - jnp.log(1.0 + jnp.exp(-jnp.abs(z)))
    row = lax.broadcasted_iota(jnp.int32, (tm, tm), 0)
    col = lax.broadcasted_iota(jnp.int32, (tm, tm), 1)
    lower = (col <= row).astype(F32)
    cum = _dot(lower, logf, precision=HIGHEST) + carry_ref[0:1, :]
    carry_ref[...] = jnp.broadcast_to(cum[tm - 1:tm, :], carry_ref.shape)

    hi, mid, lo = [p.astype(F32) for p in _split3(-LOG2E * cum)]
    lane = lax.broadcasted_iota(jnp.int32, (tm, LANES), 1)
    pieces = jnp.where(lane < N_HEADS, hi, jnp.where(lane < 2 * N_HEADS, mid, lo)).astype(BF16)
    k = _dot(hb, wk_ref[...])
    kb = _dot(pieces, place_ref[...])
    qt = _dot_nt(wqt_ref[...], hb).astype(BF16)
    qrow = lax.broadcasted_iota(jnp.int32, (LANES - HEAD_DIM, tm), 0)
    q_ones = (qrow < 3).astype(F32).astype(BF16)
    vt = _dot_nt(wvt_ref[...], hb)
    for hd in range(N_HEADS):
        pair = slice((hd // 2) * LANES, (hd // 2 + 1) * LANES)
        own_half = (lane < HEAD_DIM) == (hd % 2 == 0)
        kp_ref[0, hd] = jnp.where(own_half, k[:, pair], kb[:, pair]).astype(BF16)
        q_rows = slice(0, HEAD_DIM) if hd % 2 == 0 else slice(HEAD_DIM, LANES)
        one_rows = slice(HEAD_DIM, LANES) if hd % 2 == 0 else slice(0, HEAD_DIM)
        qt_ref[0, hd, q_rows, :] = qt[hd * HEAD_DIM:(hd + 1) * HEAD_DIM, :]
        qt_ref[0, hd, one_rows, :] = q_ones
        for c in range(tm // KEY_CHUNK):
            vt_ref[0, hd, c, 0:HEAD_DIM, :] = vt[hd * HEAD_DIM:(hd + 1) * HEAD_DIM,
                                                 c * KEY_CHUNK:(c + 1) * KEY_CHUNK].astype(BF16)
            vt_ref[0, hd, c, HEAD_DIM:V_ROWS, :] = _ones_row_tail(KEY_CHUNK)


def _fox_proj(x, sc, sh, w_in, b_f):
    B, S, _ = x.shape
    tm = ROW_TILE
    scale = LOG2E * HEAD_DIM ** -0.5
    wq = (w_in[:, :D_MODEL] * scale).reshape(D_MODEL, N_HEADS, HEAD_DIM)
    wk = w_in[:, D_MODEL:2 * D_MODEL].astype(BF16)
    wv = w_in[:, 2 * D_MODEL:3 * D_MODEL]
    wf = w_in[:, 3 * D_MODEL:]
    wqt_p = wq.reshape(D_MODEL, N_HEADS * HEAD_DIM).T.astype(BF16)
    wvt = wv.T.astype(BF16)
    wf_rep = jnp.pad(jnp.tile(wf, (1, 3)), ((0, 0), (0, LANES - 3 * N_HEADS)))
    wf_hi = wf_rep.astype(BF16)
    wf_lo = (wf_rep - wf_hi.astype(F32)).astype(BF16)
    bf_rep = jnp.pad(jnp.tile(b_f, 3), (0, LANES - 3 * N_HEADS)).reshape(1, LANES)
    place = np.zeros((LANES, N_HEADS * HEAD_DIM), np.float32)
    for r in range(3):
        for hd in range(N_HEADS):
            place[r * N_HEADS + hd, (hd ^ 1) * HEAD_DIM + r] = 1.0
    place = jnp.asarray(place, BF16)

    full = lambda a: pl.BlockSpec(a.shape, lambda b, i: (0,) * a.ndim)
    return pl.pallas_call(
        _fox_proj_kernel, name="fox_proj",
        grid=(B, S // tm),
        in_specs=[
            pl.BlockSpec((1, tm, D_MODEL), lambda b, i: (b, i, 0)),
            pl.BlockSpec((1, 1, D_MODEL), lambda b, i: (b, 0, 0)),
            pl.BlockSpec((1, 1, D_MODEL), lambda b, i: (b, 0, 0)),
            full(wk), full(wqt_p), full(wvt), full(wf_hi), full(wf_lo), full(bf_rep), full(place),
        ],
        out_specs=[
            pl.BlockSpec((1, N_HEADS, tm, LANES), lambda b, i: (b, 0, i, 0)),
            pl.BlockSpec((1, N_HEADS, LANES, tm), lambda b, i: (b, 0, 0, i)),
            pl.BlockSpec((1, N_HEADS, tm // KEY_CHUNK, V_ROWS, KEY_CHUNK), lambda b, i: (b, 0, i, 0, 0)),
        ],
        out_shape=[
            jax.ShapeDtypeStruct((B, N_HEADS, S, LANES), BF16),
            jax.ShapeDtypeStruct((B, N_HEADS, LANES, S), BF16),
            jax.ShapeDtypeStruct((B, N_HEADS, S // KEY_CHUNK, V_ROWS, KEY_CHUNK), BF16),
        ],
        scratch_shapes=[pltpu.VMEM((SUBLANES, LANES), F32)],
        compiler_params=_params("arbitrary", "arbitrary"),
    )(x, sc, sh, wk, wqt_p, wvt, wf_hi, wf_lo, bf_rep, place)


def _ones_row_tail(width):
    row = lax.broadcasted_iota(jnp.int32, (V_ROWS - HEAD_DIM, width), 0)
    return (row == 0).astype(F32).astype(BF16)


def _flash_sweep(first, n_far_pairs, scores, values, tail, s_buf, p_buf, acc_ref, far_pairs_even=False):
    width = acc_ref.shape[1]
    n_blocks = width // QUERY_SUB
    assert len(tail) % 2 == 0 and width % QUERY_SUB == 0
    assert tail[0][0] is None and tail[1][0] is None
    cat = lambda parts: jnp.concatenate(parts, axis=1)

    def fetch(c, par, bias, may_be_missing):
        block_scores = scores(c)
        missing = jnp.where(c < 0, NEG, 0.0) if may_be_missing else None
        col_max = []
        for blk in range(n_blocks):
            cols = slice(blk * QUERY_SUB, (blk + 1) * QUERY_SUB)
            part = block_scores(cols)
            extra = None if bias is None else bias(blk)
            if extra is not None:
                part = part + extra
            if missing is not None:
                part = part + missing
            s_buf[par, :, cols] = part
            col_max.append(jnp.max(part, axis=0, keepdims=True))
        return cat(col_max)

    def step(j, par, stats, nxt=None):
        m, corr1, corr2, cmax = stats
        acc_ref[...] = acc_ref[...] * corr2 + _dot(values(j - 2), p_buf[par])
        m_new = jnp.maximum(m, cmax[par])
        for blk in range(n_blocks):
            cols = slice(blk * QUERY_SUB, (blk + 1) * QUERY_SUB)
            p_buf[par, :, cols] = jnp.exp2((s_buf[par, :, cols] - m_new[:, cols]).astype(BF16))
        if nxt is not None:
            cmax = tuple(fetch(j + 2, par, *nxt) if q == par else cmax[q] for q in range(2))
        return m_new, jnp.exp2(m - m_new), corr1, cmax

    far = (None, False)

    def pairs(j, n, stats):
        for k in range(n):
            stats = step(j + 2 * k + 1, 1, step(j + 2 * k, 0, stats, far), far)
        return stats

    p_buf[...] = jnp.zeros_like(p_buf)
    acc_ref[...] = jnp.zeros_like(acc_ref)
    ones = jnp.ones((1, width), F32)
    cmax = (fetch(first, 0, None, True), fetch(first + 1, 1, None, True))
    stats = (jnp.full((1, width), NEG, F32), ones, ones, cmax)
    start = first
    for n in (1, 2):
        if n == 1 and far_pairs_even:
            continue
        group = (n_far_pairs // n) % 2
        stats = lax.cond(group == 1, lambda st, j=start, n=n: pairs(j, n, st), lambda st: st, stats)
        start = start + 2 * n * group
    stats = lax.fori_loop(0, n_far_pairs // PAIRS_PER_TRIP,
                          lambda jj, st: pairs(start + 2 * PAIRS_PER_TRIP * jj, PAIRS_PER_TRIP, st), stats)
    last = first + 2 * n_far_pairs + len(tail)
    for r in range(len(tail)):
        stats = step(last - len(tail) + r, r % 2, stats, tail[r + 2] if r + 2 < len(tail) else None)
    _, corr1, corr2, _ = stats
    acc = acc_ref[...] * corr2 + _dot(values(last - 2), p_buf[0])
    acc = acc * corr1 + _dot(values(last - 1), p_buf[1])
    return acc[:HEAD_DIM] / acc[HEAD_DIM:HEAD_DIM + 1]


def _fox_attn_kernel(qt_ref, kp_ref, vt_ref, o_ref, s_buf, p_buf, acc_ref):
    i = pl.program_id(2)
    t = qt_ref.shape[3]
    per_tile = t // KEY_CHUNK

    def scores(c):
        rows = pl.ds(pl.multiple_of(jnp.maximum(c, 0) * KEY_CHUNK, KEY_CHUNK), KEY_CHUNK)
        k = kp_ref[0, 0, rows, :]
        return lambda cols: _dot(k, qt_ref[0, 0, :, cols])

    def values(c):
        return vt_ref[0, 0, jnp.maximum(c, 0)]

    key = lax.broadcasted_iota(jnp.int32, (KEY_CHUNK, QUERY_SUB), 0)
    qry = lax.broadcasted_iota(jnp.int32, (KEY_CHUNK, QUERY_SUB), 1)
    causal = lambda r: (lambda c: jnp.where(key + r * KEY_CHUNK <= qry + c * QUERY_SUB, 0.0, NEG))
    assert per_tile % 4 == 0
    tail = [(None, True)] * per_tile + [(causal(r), False) for r in range(per_tile)]
    out = _flash_sweep(per_tile * jnp.minimum(i - 1, 0), (per_tile // 2) * jnp.maximum(i - 1, 0),
                       scores, values, tail, s_buf, p_buf, acc_ref, far_pairs_even=True)
    o_ref[0] = out.astype(o_ref.dtype)


def _fox_attention(qt, kp, vt):
    B, H, S, _ = kp.shape
    t = FOX_T
    return pl.pallas_call(
        _fox_attn_kernel, name="fox_attn",
        grid=(B, H, S // t),
        in_specs=[
            pl.BlockSpec((1, 1, LANES, t), lambda b, h, i: (b, h, 0, i)),
            pl.BlockSpec((1, 1, S, LANES), lambda b, h, i: (b, h, 0, 0)),
            pl.BlockSpec((1, 1, S // KEY_CHUNK, V_ROWS, KEY_CHUNK), lambda b, h, i: (b, h, 0, 0, 0)),
        ],
        out_specs=pl.BlockSpec((1, HEAD_DIM, t), lambda b, h, i: (b, h, i)),
        out_shape=jax.ShapeDtypeStruct((B, H * HEAD_DIM, S), BF16),
        scratch_shapes=[pltpu.VMEM((2, KEY_CHUNK, t), F32), pltpu.VMEM((2, KEY_CHUNK, t), BF16),
                        pltpu.VMEM((V_ROWS, t), F32)],
        compiler_params=_params("arbitrary", "arbitrary", "arbitrary"),
    )(qt, kp, vt)


def _block_tail_kernel(at_ref, wo_ref, x_ref, ga_ref, g1_ref, b1_ref, sc_ref, sh_ref, gf_ref,
                       wa_ref, wb_ref, w2_ref, g2_ref, b2_ref, o_ref):
    y = lax.dot_general(at_ref[0], wo_ref[...], (((0,), (0,)), ((), ())), preferred_element_type=F32)
    x1 = _layer_norm(DN_ALPHA * x_ref[0] + (1.0 + ga_ref[0]) * y, g1_ref[...], b1_ref[...])
    hb = (x1 * (1.0 + sc_ref[0]) + sh_ref[0]).astype(BF16)
    a = _dot(hb, wa_ref[...])
    b = _dot(hb, wb_ref[...])
    u = (a / (1.0 + jnp.exp(-a)) * b).astype(BF16)
    z = DN_ALPHA * x1 + (1.0 + gf_ref[0]) * _dot(u, w2_ref[...])
    o_ref[0] = _layer_norm(z, g2_ref[...], b2_ref[...])


def _block_tail(attn_t, w_out, x, gate_a, ln1, sc, sh, gate_f, w_in, w_out_ffn, ln2):
    B, S, _ = x.shape
    tm = ROW_TILE
    w_in = w_in.astype(BF16)
    row = lambda r: r.reshape(1, -1)
    mod = pl.BlockSpec((1, 1, D_MODEL), lambda b, i: (b, 0, 0))
    vec = pl.BlockSpec((1, D_MODEL), lambda b, i: (0, 0))
    resident = lambda shape, index: pl.BlockSpec(shape, index, pipeline_mode=pl.Buffered(1))
    return pl.pallas_call(
        _block_tail_kernel, name="outproj_ffn_ln",
        grid=(B, S // tm),
        in_specs=[
            pl.BlockSpec((1, D_MODEL, tm), lambda b, i: (b, 0, i)),
            resident((D_MODEL, D_MODEL), lambda b, i: (0, 0)),
            pl.BlockSpec((1, tm, D_MODEL), lambda b, i: (b, i, 0)),
            mod, vec, vec,
            mod, mod, mod,
            resident((D_MODEL, FF_HIDDEN), lambda b, i: (0, 0)),
            resident((D_MODEL, FF_HIDDEN), lambda b, i: (0, 1)),
            resident((FF_HIDDEN, D_MODEL), lambda b, i: (0, 0)),
            vec, vec,
        ],
        out_specs=pl.BlockSpec((1, tm, D_MODEL), lambda b, i: (b, i, 0)),
        out_shape=jax.ShapeDtypeStruct((B, S, D_MODEL), F32),
        compiler_params=_params("arbitrary", "arbitrary"),
    )(attn_t, w_out.astype(BF16), x, gate_a, row(ln1[0]), row(ln1[1]), sc, sh, gate_f,
      w_in, w_in, w_out_ffn.astype(BF16), row(ln2[0]), row(ln2[1]))


def _nsa_proj_kernel(x_ref, sc_ref, sh_ref, wqt_ref, wnat_ref, wvt_ref, wgt_ref, bg_ref,
                     qt_ref, kc_ref, vc_ref, ks_ref, kw_ref, vst_ref, vwt_ref, gt_ref):
    tm = x_ref.shape[1]
    t = NSA_T
    hb = (x_ref[0] * (1.0 + sc_ref[0]) + sh_ref[0]).astype(BF16)

    qt = _dot_nt(wqt_ref[...], hb).astype(BF16)
    for g in range(N_GROUPS):
        for hh in range(HEADS_PER_GROUP):
            r0 = (g * HEADS_PER_GROUP + hh) * HEAD_DIM
            for c in range(tm // t):
                qt_ref[0, g, c, :, hh * t:(hh + 1) * t] = qt[r0:r0 + HEAD_DIM, c * t:(c + 1) * t]

    nat = _dot(hb, wnat_ref[...])
    lane = lax.broadcasted_iota(jnp.int32, (tm, LANES), 1)
    ones = ((lane == HEAD_DIM) | (lane == HEAD_DIM + 1)).astype(F32)
    off_ks = 2 * KV_WIDTH
    off_kw = off_ks + N_GROUPS * LANES
    for g in range(N_GROUPS):
        kc_ref[0, g] = nat[:, g * HEAD_DIM:(g + 1) * HEAD_DIM].astype(BF16)
        vc_ref[0, g] = nat[:, KV_WIDTH + g * HEAD_DIM:KV_WIDTH + (g + 1) * HEAD_DIM].astype(BF16)
        ks_ref[0, g] = (nat[:, off_ks + g * LANES:off_ks + (g + 1) * LANES] + ones).astype(BF16)
        kw_ref[0, g] = nat[:, off_kw + g * LANES:off_kw + (g + 1) * LANES].astype(BF16)

    vt = _dot_nt(wvt_ref[...], hb).astype(BF16)
    for g in range(N_GROUPS):
        for c in range(tm // KEY_CHUNK):
            vst_ref[0, g, c, 0:HEAD_DIM, :] = vt[g * HEAD_DIM:(g + 1) * HEAD_DIM, c * KEY_CHUNK:(c + 1) * KEY_CHUNK]
            vst_ref[0, g, c, HEAD_DIM:V_ROWS, :] = _ones_row_tail(KEY_CHUNK)
        for c in range(tm // t):
            vwt_ref[0, g, c, 0:HEAD_DIM, :] = vt[KV_WIDTH + g * HEAD_DIM:KV_WIDTH + (g + 1) * HEAD_DIM,
                                                 c * t:(c + 1) * t]
            vwt_ref[0, g, c, HEAD_DIM:V_ROWS, :] = _ones_row_tail(t)

    gl = _dot_nt(wgt_ref[...], hb) + bg_ref[...]
    gates = 1.0 / (1.0 + jnp.exp(-gl))
    assert tm == t
    gt_ref[...] = jnp.zeros_like(gt_ref)
    for br in range(3):
        for g in range(N_GROUPS):
            for hh in range(HEADS_PER_GROUP):
                row = br * N_HEADS + g * HEADS_PER_GROUP + hh
                gt_ref[0, g, 0, br:br + 1, hh * t:(hh + 1) * t] = gates[row:row + 1, :]


def _nsa_proj(x, sc, sh, w_in, b_gate):
    B, S, _ = x.shape
    tm = ROW_TILE
    t = NSA_T
    scale = LOG2E * HEAD_DIM ** -0.5
    cuts = [D_MODEL + n * KV_WIDTH for n in range(7)]
    wq, wkc, wvc, wks, wvs, wkw, wvw, wg = jnp.split(w_in, cuts, axis=1)
    wqt = (wq * scale).T.astype(BF16)
    padk = lambda w: jnp.pad(w.reshape(D_MODEL, N_GROUPS, HEAD_DIM),
                             ((0, 0), (0, 0), (0, LANES - HEAD_DIM))).reshape(D_MODEL, N_GROUPS * LANES)
    wnat = jnp.concatenate([wkc, wvc, padk(wks), padk(wkw)], axis=1).astype(BF16)
    wvt = jnp.concatenate([wvs, wvw], axis=1).T.astype(BF16)
    n_gate = 3 * N_HEADS
    wgt = wg.T.astype(BF16)
    bg = jnp.broadcast_to(b_gate.reshape(n_gate, 1), (n_gate, tm))

    full = lambda a: pl.BlockSpec(a.shape, lambda b, i: (0,) * a.ndim)
    nat_spec = pl.BlockSpec((1, N_GROUPS, tm, HEAD_DIM), lambda b, i: (b, 0, i, 0))
    pad_spec = pl.BlockSpec((1, N_GROUPS, tm, LANES), lambda b, i: (b, 0, i, 0))
    vt_spec = lambda n, rows: pl.BlockSpec((1, N_GROUPS, tm // n, rows, n), lambda b, i: (b, 0, i, 0, 0))
    return pl.pallas_call(
        _nsa_proj_kernel, name="nsa_proj",
        grid=(B, S // tm),
        in_specs=[
            pl.BlockSpec((1, tm, D_MODEL), lambda b, i: (b, i, 0)),
            pl.BlockSpec((1, 1, D_MODEL), lambda b, i: (b, 0, 0)),
            pl.BlockSpec((1, 1, D_MODEL), lambda b, i: (b, 0, 0)),
            full(wqt), full(wnat), full(wvt), full(wgt), full(bg),
        ],
        out_specs=[
            pl.BlockSpec((1, N_GROUPS, tm // t, HEAD_DIM, HEADS_PER_GROUP * t), lambda b, i: (b, 0, i, 0, 0)),
            nat_spec, nat_spec, pad_spec, pad_spec, vt_spec(KEY_CHUNK, V_ROWS), vt_spec(t, V_ROWS),
            pl.BlockSpec((1, N_GROUPS, 1, SUBLANES, HEADS_PER_GROUP * t), lambda b, i: (b, 0, i, 0, 0)),
        ],
        out_shape=[
            jax.ShapeDtypeStruct((B, N_GROUPS, S // t, HEAD_DIM, HEADS_PER_GROUP * t), BF16),
            jax.ShapeDtypeStruct((B, N_GROUPS, S, HEAD_DIM), BF16),
            jax.ShapeDtypeStruct((B, N_GROUPS, S, HEAD_DIM), BF16),
            jax.ShapeDtypeStruct((B, N_GROUPS, S, LANES), BF16),
            jax.ShapeDtypeStruct((B, N_GROUPS, S, LANES), BF16),
            jax.ShapeDtypeStruct((B, N_GROUPS, S // KEY_CHUNK, V_ROWS, KEY_CHUNK), BF16),
            jax.ShapeDtypeStruct((B, N_GROUPS, S // t, V_ROWS, t), BF16),
            jax.ShapeDtypeStruct((B, N_GROUPS, S // t, SUBLANES, HEADS_PER_GROUP * t), F32),
        ],
        compiler_params=_params("arbitrary", "arbitrary"),
    )(x, sc, sh, wqt, wnat, wvt, wgt, bg)


def _compress_kernel(tk_ref, tv_ref, pos_ref, w1_ref, w2k_ref, w2vt_ref, kc_ref, vct_ref):
    n = tk_ref.shape[2]
    half = CMP_STRIDE * HEAD_DIM

    def hidden(t_ref, idx):
        t16 = t_ref[0, 0]
        xa = (t16 + pos_ref[idx, 0:1, :]).astype(BF16)
        xb = (t16 + pos_ref[idx, 1:2, :]).astype(BF16)
        first = _dot(xa, w1_ref[idx, :half, :])
        second = _dot(xb, w1_ref[idx, half:, :])
        pre = first + pltpu.roll(second, n - 1, 0)
        return (pre / (1.0 + jnp.exp(-pre))).astype(BF16)

    kc = _dot(hidden(tk_ref, 0), w2k_ref[...])
    lane = lax.broadcasted_iota(jnp.int32, kc.shape, 1)
    ones = ((lane == HEAD_DIM) | (lane == HEAD_DIM + 1)).astype(F32)
    kc_ref[0, 0] = (kc + ones).astype(BF16)
    vct = _dot_nt(w2vt_ref[...], hidden(tv_ref, 1)).astype(BF16)
    for r in range(n // CMP_ROWS):
        vct_ref[0, 0, r] = vct[:, r * CMP_ROWS:(r + 1) * CMP_ROWS]


def _compress(kc, vc, cmp_pos, cmp_w1, cmp_w2):
    B, G, S, _ = kc.shape
    n = S // CMP_STRIDE
    width = CMP_STRIDE * HEAD_DIM
    tk = kc.reshape(B, G, n, width)
    tv = vc.reshape(B, G, n, width)
    pos = cmp_pos.reshape(2, 2, width)
    w1 = cmp_w1.astype(BF16)
    w2k = jnp.pad(cmp_w2[0], ((0, 0), (0, LANES - HEAD_DIM))).astype(BF16)
    w2vt = cmp_w2[1].T.astype(BF16)
    full = lambda a: pl.BlockSpec(a.shape, lambda b, g: (0,) * a.ndim)
    t_spec = pl.BlockSpec((1, 1, n, width), lambda b, g: (b, g, 0, 0))
    return pl.pallas_call(
        _compress_kernel, name="nsa_compress",
        grid=(B, G),
        in_specs=[t_spec, t_spec, full(pos), full(w1), full(w2k), full(w2vt)],
        out_specs=[
            pl.BlockSpec((1, 1, n, LANES), lambda b, g: (b, g, 0, 0)),
            pl.BlockSpec((1, 1, n // CMP_ROWS, HEAD_DIM, CMP_ROWS), lambda b, g: (b, g, 0, 0, 0)),
        ],
        out_shape=[
            jax.ShapeDtypeStruct((B, G, n, LANES), BF16),
            jax.ShapeDtypeStruct((B, G, n // CMP_ROWS, HEAD_DIM, CMP_ROWS), BF16),
        ],
        compiler_params=_params("arbitrary", "arbitrary"),
    )(tk, tv, pos, w1, w2k, w2vt)


def _cmp_band_rows(t):
    return t // CMP_STRIDE + NEAR_CMP_BLOCKS


def _cmp_band_start(i, t):
    return jnp.maximum(i * t // CMP_STRIDE - NEAR_CMP_BLOCKS, 0)


def _bias_tiles_kernel(rb_ref, sel_ref, win_ref, cmp_ref):
    hd = pl.program_id(0)
    far = rb_ref[FAR_BUCKET, hd]

    def dist(n, back):
        key = lax.broadcasted_iota(jnp.int32, (n, n), 0)
        qry = lax.broadcasted_iota(jnp.int32, (n, n), 1)
        return qry - key + back * n

    kc = sel_ref.shape[2]
    d_prev, d_diag = dist(kc, 1), dist(kc, 0)
    sel_ref[0, 0] = LOG2E * (_bucket_bias(d_prev, rb_ref, hd) - far)
    sel_ref[1, 0] = jnp.where(d_diag >= 0, LOG2E * (_bucket_bias(d_diag, rb_ref, hd) - far), NEG)
    t = win_ref.shape[2]
    n_back = win_ref.shape[0] - 1
    for r in range(n_back + 1):
        d = dist(t, n_back - r)
        win_ref[r, 0] = jnp.where((d >= 0) & (d < WINDOW), LOG2E * _bucket_bias(d, rb_ref, hd), NEG)
    band = cmp_ref.shape[2]
    blk = lax.broadcasted_iota(jnp.int32, (band, t), 0)
    qry = lax.broadcasted_iota(jnp.int32, (band, t), 1)
    for later in range(2):
        d = (later * t + qry) - ((_cmp_band_start(later, t) + blk) * CMP_STRIDE + CMP_LEN - 1)
        cmp_ref[later, 0] = LOG2E * (_bucket_bias(d, rb_ref, hd) - far)


def _bias_tiles(rel_bias):
    t = NSA_T
    n_win = WINDOW // t + 1
    band = _cmp_band_rows(t)
    assert WINDOW % t == 0 and (FAR_DIST + CMP_LEN - 1) <= NEAR_CMP_BLOCKS * CMP_STRIDE
    spec = lambda n, rows, cols: pl.BlockSpec((n, 1, rows, cols), lambda h: (0, h, 0, 0))
    return pl.pallas_call(
        _bias_tiles_kernel, name="rel_bias_tiles",
        grid=(N_HEADS,),
        in_specs=[pl.BlockSpec(memory_space=pltpu.SMEM)],
        out_specs=[spec(2, KEY_CHUNK, KEY_CHUNK), spec(n_win, t, t), spec(2, band, t)],
        out_shape=[
            jax.ShapeDtypeStruct((2, N_HEADS, KEY_CHUNK, KEY_CHUNK), F32),
            jax.ShapeDtypeStruct((n_win, N_HEADS, t, t), F32),
            jax.ShapeDtypeStruct((2, N_HEADS, band, t), F32),
        ],
        compiler_params=_params("arbitrary"),
    )(rel_bias)


def _far_bias_rows(rb_ref, g, width):
    t = width // HEADS_PER_GROUP
    lane = lax.broadcasted_iota(jnp.int32, (HEAD_DIM, width), 1)
    row = lax.broadcasted_iota(jnp.int32, (HEAD_DIM, width), 0)
    far = jnp.zeros((HEAD_DIM, width), F32)
    for hh in range(HEADS_PER_GROUP):
        far = jnp.where(lane >= hh * t, LOG2E * rb_ref[FAR_BUCKET, g * HEADS_PER_GROUP + hh], far)
    hi = far.astype(BF16).astype(F32)
    return jnp.where(row == 0, hi, jnp.where(row == 1, far - hi, 0.0)).astype(BF16)


def _nsa_cmp_kernel(rb_ref, qt_ref, kc_ref, vct_ref, at_ref, band_ref, oc_ref, sb_ref,
                    qp_ref, s_ref, imp_ref):
    g = pl.program_id(1)
    i = pl.program_id(2)
    width = qt_ref.shape[4]
    t = width // HEADS_PER_GROUP
    n = kc_ref.shape[2]
    nb = sb_ref.shape[3]
    t0 = i * t
    assert t & (t - 1) == 0 and n % CMP_ROWS == 0

    n_vis = (t0 + t - CMP_LEN) // CMP_STRIDE // CMP_ROWS + 1
    n_full = jnp.maximum((t0 - (CMP_LEN - 1)) // CMP_STRIDE + 1, 0) // CMP_ROWS
    rows_of = lambda r: pl.ds(pl.multiple_of(r * CMP_ROWS, CMP_ROWS), CMP_ROWS)

    def valid(r):
        blk = r * CMP_ROWS + lax.broadcasted_iota(jnp.int32, (CMP_ROWS, width), 0)
        qry = t0 + (lax.broadcasted_iota(jnp.int32, (CMP_ROWS, width), 1) & (t - 1))
        return qry >= blk * CMP_STRIDE + CMP_LEN - 1

    qp_ref[0:HEAD_DIM, :] = qt_ref[0, 0, 0]
    qp_ref[HEAD_DIM:LANES, :] = _far_bias_rows(rb_ref, g, width)

    def score_chunk(r, carry):
        s_ref[rows_of(r), :] = _dot(kc_ref[0, 0, rows_of(r), :], qp_ref[...])
        return carry

    lax.fori_loop(0, n_vis, score_chunk, 0)

    band = band_ref.shape[2]
    assert band == _cmp_band_rows(t) and band <= n
    r0 = pl.multiple_of(_cmp_band_start(i, t), SUBLANES)
    for hh in range(HEADS_PER_GROUP):
        s_ref[pl.ds(r0, band), hh * t:(hh + 1) * t] += band_ref[jnp.minimum(i, 1), hh]

    def col_max(masked):
        def body(r, m):
            s = s_ref[rows_of(r), :]
            if masked:
                s = jnp.where(valid(r), s, NEG)
            return jnp.maximum(m, jnp.max(s, axis=0, keepdims=True))
        return body

    m = lax.fori_loop(0, n_full, col_max(False), jnp.full((1, width), NEG, F32))
    m = lax.fori_loop(n_full, n_vis, col_max(True), m)

    def exp_sum(masked):
        def body(r, l):
            p = jnp.exp2(s_ref[rows_of(r), :] - m)
            if masked:
                p = jnp.where(valid(r), p, 0.0)
            s_ref[rows_of(r), :] = p
            return l + jnp.sum(p, axis=0, keepdims=True)
        return body

    l = lax.fori_loop(0, n_full, exp_sum(False), jnp.zeros((1, width), F32))
    l = lax.fori_loop(n_full, n_vis, exp_sum(True), l)
    inv = 1.0 / jnp.maximum(l, TINY)

    oc_ref[0, 0, 0] = jnp.zeros((HEAD_DIM, width), F32)
    imp_ref[...] = jnp.zeros_like(imp_ref)

    def finish(r, carry):
        p = s_ref[rows_of(r), :] * inv
        oc_ref[0, 0, 0] += _dot(vct_ref[0, 0, r], p.astype(BF16))
        imp = p[:, 0:t]
        for hh in range(1, HEADS_PER_GROUP):
            imp = imp + p[:, hh * t:(hh + 1) * t]
        hi, mid, lo = _split3(imp)
        a = at_ref[...]
        band = pl.ds(pl.multiple_of(r * (CMP_ROWS * CMP_STRIDE // SEL_BLOCK), SUBLANES), a.shape[0])
        imp_ref[band, :] += _dot(a, hi) + _dot(a, mid) + _dot(a, lo)
        return carry

    lax.fori_loop(0, n_vis, finish, 0)

    def select_among(rows):
        def run():
            sblk = lax.broadcasted_iota(jnp.int32, (rows, t), 0)
            cur = (t0 + lax.broadcasted_iota(jnp.int32, (rows, t), 1)) // SEL_BLOCK
            forced = (sblk == 0) | (sblk == cur) | (sblk == cur - 1)
            is_cand = (sblk >= 1) & (sblk <= cur - 2)
            cand = jnp.where(is_cand, imp_ref[0:rows, :], -1.0)
            sblk_f = sblk.astype(F32)
            for _ in range(SEL_TOPK - N_FORCED):
                best = jnp.max(cand, axis=0, keepdims=True)
                first = jnp.min(jnp.where(cand == best, sblk_f, float(nb)), axis=0, keepdims=True)
                cand = jnp.where(sblk_f == first, -1.0, cand)
            chosen = forced | (is_cand & (cand < 0.0))
            sb_ref[0, 0, 0, 0:rows, :] = jnp.where(chosen, 0.0, NEG).astype(BF16)
            if rows < nb:
                sb_ref[0, 0, 0, rows:nb, :] = jnp.full((nb - rows, t), NEG, BF16)
        return run

    last_block = (t0 + t - 1) // SEL_BLOCK
    lax.switch(jnp.minimum(last_block // TOPK_ROWS, nb // TOPK_ROWS - 1),
               [select_among(rows) for rows in range(TOPK_ROWS, nb + 1, TOPK_ROWS)])


def _nsa_cmp(rel_bias, qt, kcmp, vcmp_t, cmp_band, n_sel):
    B, G, nq, _, width = qt.shape
    t = width // HEADS_PER_GROUP
    n = kcmp.shape[2]
    n_cmp = n - 1
    nb = -(-n_sel // LANES) * LANES
    R = SEL_BLOCK // CMP_STRIDE
    assert n_cmp == n - 1 and nb >= n_sel
    band_rows = -(-(CMP_ROWS // R + 1) // 8) * 8
    at = np.zeros((band_rows, CMP_ROWS), np.float32)
    for j in range(CMP_ROWS // R + 1):
        at[j, max(R * j - 1, 0):min(R * j + R, CMP_ROWS)] = 1.0
    at = jnp.asarray(at, BF16)
    return pl.pallas_call(
        _nsa_cmp_kernel, name="nsa_cmp_topk",
        grid=(B, G, nq),
        in_specs=[
            pl.BlockSpec(memory_space=pltpu.SMEM),
            pl.BlockSpec((1, 1, 1, HEAD_DIM, width), lambda b, g, i: (b, g, i, 0, 0)),
            pl.BlockSpec((1, 1, n, LANES), lambda b, g, i: (b, g, 0, 0)),
            pl.BlockSpec((1, 1, n // CMP_ROWS, HEAD_DIM, CMP_ROWS), lambda b, g, i: (b, g, 0, 0, 0)),
            pl.BlockSpec((band_rows, CMP_ROWS), lambda b, g, i: (0, 0)),
            pl.BlockSpec((2, HEADS_PER_GROUP) + cmp_band.shape[2:], lambda b, g, i: (0, g, 0, 0)),
        ],
        out_specs=[
            pl.BlockSpec((1, 1, 1, HEAD_DIM, width), lambda b, g, i: (b, g, i, 0, 0)),
            pl.BlockSpec((1, 1, 1, nb, t), lambda b, g, i: (b, g, i, 0, 0)),
        ],
        out_shape=[
            jax.ShapeDtypeStruct((B, G, nq, HEAD_DIM, width), F32),
            jax.ShapeDtypeStruct((B, G, nq, nb, t), BF16),
        ],
        scratch_shapes=[pltpu.VMEM((LANES, width), BF16), pltpu.VMEM((n, width), F32),
                        pltpu.VMEM((nb + band_rows, t), F32)],
        compiler_params=_params("arbitrary", "arbitrary", "arbitrary"),
    )(rel_bias, qt, kcmp, vcmp_t, at, cmp_band)


def _nsa_win_kernel(qt_ref, *refs):
    n_win = (len(refs) - 5) // 2
    k_refs, v_refs = refs[:n_win], refs[n_win:2 * n_win]
    wb_ref, ow_ref, qp_ref, s_ref, p_ref = refs[2 * n_win:]
    i = pl.program_id(2)
    width = qt_ref.shape[4]
    t = width // HEADS_PER_GROUP
    qt = qt_ref[0, 0, 0]
    qp_ref[0:HEAD_DIM, :] = qt
    qp_ref[HEAD_DIM:LANES, :] = jnp.zeros_like(qt)
    m = None
    for c in range(n_win):
        k = k_refs[c][0, 0]
        missing = jnp.where(i - (n_win - 1 - c) < 0, NEG, 0.0)
        col_max = []
        for blk in range(width // QUERY_SUB):
            cols = slice(blk * QUERY_SUB, (blk + 1) * QUERY_SUB)
            hh, off = divmod(blk * QUERY_SUB, t)
            part = _dot(k, qp_ref[:, cols]) + wb_ref[c, hh, :, off:off + QUERY_SUB] + missing
            s_ref[c, :, cols] = part
            col_max.append(jnp.max(part, axis=0, keepdims=True))
        col_max = jnp.concatenate(col_max, axis=1)
        m = col_max if m is None else jnp.maximum(m, col_max)
    acc = jnp.zeros((V_ROWS, width), F32)
    for c in range(n_win):
        for blk in range(width // QUERY_SUB):
            cols = slice(blk * QUERY_SUB, (blk + 1) * QUERY_SUB)
            p_ref[c, :, cols] = jnp.exp2((s_ref[c, :, cols] - m[:, cols]).astype(BF16))
        acc = acc + _dot(v_refs[c][0, 0, 0], p_ref[c])
    ow_ref[0, 0, 0] = acc[:HEAD_DIM] / acc[HEAD_DIM:HEAD_DIM + 1]


def _nsa_window(qt, kw, vwt, win_bias):
    B, G, nq, _, width = qt.shape
    t = width // HEADS_PER_GROUP
    n_win = win_bias.shape[0]
    backs = list(range(n_win - 1, -1, -1))
    k_spec = lambda back: pl.BlockSpec((1, 1, t, LANES), lambda b, g, i: (b, g, jnp.maximum(i - back, 0), 0))
    v_spec = lambda back: pl.BlockSpec((1, 1, 1, V_ROWS, t),
                                       lambda b, g, i: (b, g, jnp.maximum(i - back, 0), 0, 0))
    return pl.pallas_call(
        _nsa_win_kernel, name="nsa_window",
        grid=(B, G, nq),
        in_specs=[pl.BlockSpec((1, 1, 1, HEAD_DIM, width), lambda b, g, i: (b, g, i, 0, 0))]
        + [k_spec(back) for back in backs] + [v_spec(back) for back in backs]
        + [pl.BlockSpec((n_win, HEADS_PER_GROUP, t, t), lambda b, g, i: (0, g, 0, 0))],
        out_specs=pl.BlockSpec((1, 1, 1, HEAD_DIM, width), lambda b, g, i: (b, g, i, 0, 0)),
        out_shape=jax.ShapeDtypeStruct((B, G, nq, HEAD_DIM, width), F32),
        scratch_shapes=[pltpu.VMEM((LANES, width), BF16),
                        pltpu.VMEM((n_win, t, width), F32), pltpu.VMEM((n_win, t, width), BF16)],
        compiler_params=_params("arbitrary", "arbitrary", "arbitrary"),
    )(qt, *([kw] * n_win), *([vwt] * n_win), win_bias)


def _nsa_sel_kernel(rb_ref, qt_ref, sb_ref, ks_ref, e_ref, vst_ref, cb_ref, oc_ref, ow_ref, gt_ref,
                    o_ref, qp_ref, s_buf, p_buf, acc_ref):
    g = pl.program_id(1)
    i = pl.program_id(2)
    width = qt_ref.shape[4]
    t = width // HEADS_PER_GROUP
    per_tile = t // KEY_CHUNK
    chunks_per_slab = LANES * SEL_BLOCK // KEY_CHUNK
    blocks_per_head = t // QUERY_SUB
    assert per_tile == 2 and t % QUERY_SUB == 0

    far_rows = _far_bias_rows(rb_ref, g, width)
    for slab in range(qp_ref.shape[0]):
        qp_ref[slab, 0:HEAD_DIM, :] = qt_ref[0, 0, 0]
        qp_ref[slab, HEAD_DIM:LANES, :] = far_rows
        sb = sb_ref[0, 0, 0, slab * LANES:(slab + 1) * LANES, :]
        qp_ref[slab, LANES:2 * LANES, :] = jnp.concatenate([sb] * HEADS_PER_GROUP, axis=1)

    def scores(c):
        c = jnp.maximum(c, 0)
        rows = pl.ds(pl.multiple_of(c * KEY_CHUNK, KEY_CHUNK), KEY_CHUNK)
        kp = jnp.concatenate([ks_ref[0, 0, rows, :], e_ref[rows, :]], axis=1)
        return lambda cols: _dot(kp, qp_ref[c // chunks_per_slab, :, cols])

    def values(c):
        return vst_ref[0, 0, jnp.maximum(c, 0)]

    def near_bias(r):
        def bias(c):
            hh, part = divmod(c, blocks_per_head)
            ahead = part * QUERY_SUB // KEY_CHUNK - (r - per_tile)
            if ahead == 0:
                return cb_ref[1, hh]
            if ahead == 1:
                return cb_ref[0, hh]
            if ahead < 0:
                return jnp.full((KEY_CHUNK, QUERY_SUB), NEG, F32)
            return None
        return bias

    tail = [(None, True), (None, True)] + [(near_bias(r), r < per_tile) for r in range(2 * per_tile)]
    o_sel = _flash_sweep(per_tile * jnp.minimum(i - 2, 0), jnp.maximum(i - 2, 0), scores, values, tail,
                         s_buf, p_buf, acc_ref)

    out = gt_ref[0, 0, 0, 0:1, :] * oc_ref[0, 0, 0] + gt_ref[0, 0, 0, 1:2, :] * o_sel \
        + gt_ref[0, 0, 0, 2:3, :] * ow_ref[0, 0, 0]
    for hh in range(HEADS_PER_GROUP):
        o_ref[0, hh * HEAD_DIM:(hh + 1) * HEAD_DIM, :] = out[:, hh * t:(hh + 1) * t].astype(o_ref.dtype)


def _nsa_select_combine(rel_bias, qt, sel_bias, ks, vst, sel_corr, oc, ow, gates):
    B, G, nq, _, width = qt.shape
    t = width // HEADS_PER_GROUP
    S = nq * t
    nb = sel_bias.shape[3]
    blocks = (np.arange(S) // SEL_BLOCK) % LANES
    onehot = jnp.asarray(blocks[:, None] == np.arange(LANES)[None, :], BF16)
    tile = pl.BlockSpec((1, 1, 1, HEAD_DIM, width), lambda b, g, i: (b, g, i, 0, 0))
    return pl.pallas_call(
        _nsa_sel_kernel, name="nsa_select",
        grid=(B, G, nq),
        in_specs=[
            pl.BlockSpec(memory_space=pltpu.SMEM),
            tile,
            pl.BlockSpec((1, 1, 1, nb, t), lambda b, g, i: (b, g, i, 0, 0)),
            pl.BlockSpec((1, 1, S, LANES), lambda b, g, i: (b, g, 0, 0)),
            pl.BlockSpec((S, LANES), lambda b, g, i: (0, 0)),
            pl.BlockSpec((1, 1, S // KEY_CHUNK, V_ROWS, KEY_CHUNK), lambda b, g, i: (b, g, 0, 0, 0)),
            pl.BlockSpec((2, HEADS_PER_GROUP, KEY_CHUNK, KEY_CHUNK), lambda b, g, i: (0, g, 0, 0)),
            tile, tile,
            pl.BlockSpec((1, 1, 1, SUBLANES, width), lambda b, g, i: (b, g, i, 0, 0)),
        ],
        out_specs=pl.BlockSpec((1, HEADS_PER_GROUP * HEAD_DIM, t), lambda b, g, i: (b, g, i)),
        out_shape=jax.ShapeDtypeStruct((B, D_MODEL, S), BF16),
        scratch_shapes=[pltpu.VMEM((nb // LANES, 2 * LANES, width), BF16),
                        pltpu.VMEM((2, KEY_CHUNK, width), F32), pltpu.VMEM((2, KEY_CHUNK, width), BF16),
                        pltpu.VMEM((V_ROWS, width), F32)],
        compiler_params=_params("arbitrary", "arbitrary", "arbitrary"),
    )(rel_bias, qt, sel_bias, ks, onehot, vst, sel_corr, oc, ow, gates)


def _nsa_attention_t(x, sc, sh, w_in, b_gate, cmp_pos, cmp_w1, cmp_w2, rel_bias):
    B, S, _ = x.shape
    t = NSA_T
    n_sel = S // SEL_BLOCK
    assert S % ROW_TILE == 0 and n_sel >= SEL_TOPK and S // CMP_STRIDE >= t // CMP_STRIDE + NEAR_CMP_BLOCKS
    qt, kc, vc, ks, kw, vst, vwt, gt = _nsa_proj(x, sc, sh, w_in, b_gate)
    kcmp, vcmp_t = _compress(kc, vc, cmp_pos, cmp_w1, cmp_w2)
    sel_corr, win_bias, cmp_band = _bias_tiles(rel_bias)
    oc, sel_bias = _nsa_cmp(rel_bias, qt, kcmp, vcmp_t, cmp_band, n_sel)
    ow = _nsa_window(qt, kw, vwt, win_bias)
    return _nsa_select_combine(rel_bias, qt, sel_bias, ks, vst, sel_corr, oc, ow, gt)


def kernel(x, c, ada_w, ada_b, ln_g, ln_b, fox_w_in, fox_b_f, fox_w_out, nsa_w_in, nsa_b_gate,
           nsa_cmp_pos, nsa_cmp_w1, nsa_cmp_w2, nsa_w_out, rel_bias, ffn_w_in, ffn_w_out):
    B, S, _ = x.shape
    assert S % FOX_T == 0 and S % ROW_TILE == 0 and ROW_TILE == NSA_T
    mod = _modulation(c, ada_w, ada_b)
    for layer in range(DEPTH):
        sh_a, sc_a, g_a, sh_f, sc_f, g_f = [m.reshape(B, 1, D_MODEL) for m in jnp.split(mod[layer], 6, axis=-1)]
        j = layer // 2
        if layer % 2 == 0:
            kp, qt, vt = _fox_proj(x, sc_a, sh_a, fox_w_in[j], fox_b_f[j])
            attn_t = _fox_attention(qt, kp, vt)
            w_out = fox_w_out[j]
        else:
            attn_t = _nsa_attention_t(x, sc_a, sh_a, nsa_w_in[j], nsa_b_gate[j], nsa_cmp_pos[j],
                                      nsa_cmp_w1[j], nsa_cmp_w2[j], rel_bias)
            w_out = nsa_w_out[j]
        x = _block_tail(attn_t, w_out, x, g_a, (ln_g[layer, 0], ln_b[layer, 0]), sc_f, sh_f, g_f,
                        ffn_w_in[layer], ffn_w_out[layer], (ln_g[layer, 1], ln_b[layer, 1]))
    return x
```

```python
import math

import numpy as np
import jax
import jax.numpy as jnp
from jax import lax
from jax.experimental import pallas as pl
from jax.experimental.pallas import tpu as pltpu

F32 = jnp.float32
BF16 = jnp.bfloat16
HIGHEST = lax.Precision.HIGHEST

D_MODEL = 1024
HEAD_DIM = 64
N_HEADS = 16
N_GROUPS = 4
HEADS_PER_GROUP = 4
KV_WIDTH = N_GROUPS * HEAD_DIM
CMP_LEN = 32
CMP_STRIDE = 16
SEL_BLOCK = 64
SEL_TOPK = 16
WINDOW = 512
REL_BUCKETS = 32
FF_HIDDEN = 2816
DEPTH = 2
DN_ALPHA = (2 * DEPTH) ** 0.25
LN_EPS = 1e-5
NEG = -1e30
TINY = 1e-30

LANES = 128
SUBLANES = 8
VMEM_LIMIT = 56 * 1024 * 1024

ROW_TILE = 512
FOX_T = 1024
NSA_T = 512
MOD_TILE = 1536
N_FORCED = 3
NEAR_CMP_BLOCKS = 16

BUCKET_START = (0, 1, 2, 3, 4, 5, 6, 7, 8, 9, 10, 11, 12, 13, 14, 15,
                16, 19, 21, 24, 27, 31, 35, 40, 46, 52, 59, 67, 77, 87, 99, 113)
FAR_BUCKET = REL_BUCKETS - 1
FAR_DIST = BUCKET_START[FAR_BUCKET]

LOG2E = math.log2(math.e)
QUERY_SUB = 256
KEY_CHUNK = 256
PAIRS_PER_TRIP = 4
TOPK_ROWS = 64
CMP_ROWS = 128
BF16_ROWS = 2 * SUBLANES
V_ROWS = HEAD_DIM + BF16_ROWS
NT_DIMS = (((1,), (1,)), ((), ()))


def _params(*sem):
    return pltpu.CompilerParams(dimension_semantics=sem, vmem_limit_bytes=VMEM_LIMIT)


def _dot(a, b, **kw):
    return jnp.dot(a, b, preferred_element_type=F32, **kw)


def _dot_nt(a, b):
    return lax.dot_general(a, b, NT_DIMS, preferred_element_type=F32)


def _split3(v):
    hi = v.astype(BF16)
    r = v - hi.astype(F32)
    mid = r.astype(BF16)
    lo = (r - mid.astype(F32)).astype(BF16)
    return hi, mid, lo


def _layer_norm(z, g, b):
    mu = jnp.mean(z, axis=-1, keepdims=True)
    zc = z - mu
    var = jnp.mean(zc * zc, axis=-1, keepdims=True)
    return zc * lax.rsqrt(var + LN_EPS) * g + b


def _bucket_bias(dist, rb_ref, head):
    bias = jnp.full(dist.shape, rb_ref[0, head], F32)
    for k in range(1, REL_BUCKETS):
        bias = jnp.where(dist >= BUCKET_START[k], rb_ref[k, head], bias)
    return bias


def _mod_kernel(c_ref, w_ref, b_ref, o_ref):
    c = c_ref[...]
    cs = c / (1.0 + jnp.exp(-c))
    o_ref[0] = _dot(cs, w_ref[0], precision=HIGHEST) + b_ref[0]


def _modulation(c, ada_w, ada_b):
    B = c.shape[0]
    depth, _, n = ada_w.shape
    rows = SUBLANES
    c_pad = jnp.pad(c, ((0, rows - B), (0, 0)))
    tn = MOD_TILE
    out = pl.pallas_call(
        _mod_kernel, name="adaln_mod",
        grid=(depth, n // tn),
        in_specs=[
            pl.BlockSpec((rows, D_MODEL), lambda l, j: (0, 0)),
            pl.BlockSpec((1, D_MODEL, tn), lambda l, j: (l, 0, j)),
            pl.BlockSpec((1, 1, tn), lambda l, j: (l, 0, j)),
        ],
        out_specs=pl.BlockSpec((1, rows, tn), lambda l, j: (l, 0, j)),
        out_shape=jax.ShapeDtypeStruct((depth, rows, n), F32),
        compiler_params=_params("arbitrary", "arbitrary"),
    )(c_pad, ada_w, ada_b.reshape(depth, 1, n))
    return out[:, :B]


def _fox_proj_kernel(x_ref, sc_ref, sh_ref, wk_ref, wqt_ref, wvt_ref, wfh_ref, wfl_ref,
                     bf_ref, place_ref, kp_ref, qt_ref, vt_ref, carry_ref):
    tm = x_ref.shape[1]

    @pl.when(pl.program_id(1) == 0)
    def _():
        carry_ref[...] = jnp.zeros_like(carry_ref)

    h = x_ref[0] * (1.0 + sc_ref[0]) + sh_ref[0]
    hb = h.astype(BF16)
    hl = (h - hb.astype(F32)).astype(BF16)

    f = _dot(hb, wfh_ref[...]) + _dot(hl, wfh_ref[...]) + _dot(hb, wfl_ref[...])
    z = f + bf_ref[...]
    logf = jnp.minimum(z, 0.0) - jnp.log(1.0 + jnp.exp(-jnp.abs(z)))
    row = lax.broadcasted_iota(jnp.int32, (tm, tm), 0)
    col = lax.broadcasted_iota(jnp.int32, (tm, tm), 1)
    lower = (col <= row).astype(F32)
    cum = _dot(lower, logf, precision=HIGHEST) + carry_ref[0:1, :]
    carry_ref[...] = jnp.broadcast_to(cum[tm - 1:tm, :], carry_ref.shape)

    hi, mid, lo = [p.astype(F32) for p in _split3(-LOG2E * cum)]
    lane = lax.broadcasted_iota(jnp.int32, (tm, LANES), 1)
    pieces = jnp.where(lane < N_HEADS, hi, jnp.where(lane < 2 * N_HEADS, mid, lo)).astype(BF16)
    k = _dot(hb, wk_ref[...])
    kb = _dot(pieces, place_ref[...])
    qt = _dot_nt(wqt_ref[...], hb).astype(BF16)
    qrow = lax.broadcasted_iota(jnp.int32, (LANES - HEAD_DIM, tm), 0)
    q_ones = (qrow < 3).astype(F32).astype(BF16)
    vt = _dot_nt(wvt_ref[...], hb)
    for hd in range(N_HEADS):
        pair = slice((hd // 2) * LANES, (hd // 2 + 1) * LANES)
        own_half = (lane < HEAD_DIM) == (hd % 2 == 0)
        kp_ref[0, hd] = jnp.where(own_half, k[:, pair], kb[:, pair]).astype(BF16)
        q_rows = slice(0, HEAD_DIM) if hd % 2 == 0 else slice(HEAD_DIM, LANES)
        one_rows = slice(HEAD_DIM, LANES) if hd % 2 == 0 else slice(0, HEAD_DIM)
        qt_ref[0, hd, q_rows, :] = qt[hd * HEAD_DIM:(hd + 1) * HEAD_DIM, :]
        qt_ref[0, hd, one_rows, :] = q_ones
        for c in range(tm // KEY_CHUNK):
            vt_ref[0, hd, c, 0:HEAD_DIM, :] = vt[hd * HEAD_DIM:(hd + 1) * HEAD_DIM,
                                                 c * KEY_CHUNK:(c + 1) * KEY_CHUNK].astype(BF16)
            vt_ref[0, hd, c, HEAD_DIM:V_ROWS, :] = _ones_row_tail(KEY_CHUNK)


def _fox_proj(x, sc, sh, w_in, b_f):
    B, S, _ = x.shape
    tm = ROW_TILE
    scale = LOG2E * HEAD_DIM ** -0.5
    wq = (w_in[:, :D_MODEL] * scale).reshape(D_MODEL, N_HEADS, HEAD_DIM)
    wk = w_in[:, D_MODEL:2 * D_MODEL].astype(BF16)
    wv = w_in[:, 2 * D_MODEL:3 * D_MODEL]
    wf = w_in[:, 3 * D_MODEL:]
    wqt_p = wq.reshape(D_MODEL, N_HEADS * HEAD_DIM).T.astype(BF16)
    wvt = wv.T.astype(BF16)
    wf_rep = jnp.pad(jnp.tile(wf, (1, 3)), ((0, 0), (0, LANES - 3 * N_HEADS)))
    wf_hi = wf_rep.astype(BF16)
    wf_lo = (wf_rep - wf_hi.astype(F32)).astype(BF16)
    bf_rep = jnp.pad(jnp.tile(b_f, 3), (0, LANES - 3 * N_HEADS)).reshape(1, LANES)
    place = np.zeros((LANES, N_HEADS * HEAD_DIM), np.float32)
    for r in range(3):
        for hd in range(N_HEADS):
            place[r * N_HEADS + hd, (hd ^ 1) * HEAD_DIM + r] = 1.0
    place = jnp.asarray(place, BF16)

    full = lambda a: pl.BlockSpec(a.shape, lambda b, i: (0,) * a.ndim)
    return pl.pallas_call(
        _fox_proj_kernel, name="fox_proj",
        grid=(B, S // tm),
        in_specs=[
            pl.BlockSpec((1, tm, D_MODEL), lambda b, i: (b, i, 0)),
            pl.BlockSpec((1, 1, D_MODEL), lambda b, i: (b, 0, 0)),
            pl.BlockSpec((1, 1, D_MODEL), lambda b, i: (b, 0, 0)),
            full(wk), full(wqt_p), full(wvt), full(wf_hi), full(wf_lo), full(bf_rep), full(place),
        ],
        out_specs=[
            pl.BlockSpec((1, N_HEADS, tm, LANES), lambda b, i: (b, 0, i, 0)),
            pl.BlockSpec((1, N_HEADS, LANES, tm), lambda b, i: (b, 0, 0, i)),
            pl.BlockSpec((1, N_HEADS, tm // KEY_CHUNK, V_ROWS, KEY_CHUNK), lambda b, i: (b, 0, i, 0, 0)),
        ],
        out_shape=[
            jax.ShapeDtypeStruct((B, N_HEADS, S, LANES), BF16),
            jax.ShapeDtypeStruct((B, N_HEADS, LANES, S), BF16),
            jax.ShapeDtypeStruct((B, N_HEADS, S // KEY_CHUNK, V_ROWS, KEY_CHUNK), BF16),
        ],
        scratch_shapes=[pltpu.VMEM((SUBLANES, LANES), F32)],
        compiler_params=_params("arbitrary", "arbitrary"),
    )(x, sc, sh, wk, wqt_p, wvt, wf_hi, wf_lo, bf_rep, place)


def _ones_row_tail(width):
    row = lax.broadcasted_iota(jnp.int32, (V_ROWS - HEAD_DIM, width), 0)
    return (row == 0).astype(F32).astype(BF16)


def _flash_sweep(first, n_far_pairs, scores, values, tail, s_buf, p_buf, acc_ref, far_pairs_even=False):
    width = acc_ref.shape[1]
    n_blocks = width // QUERY_SUB
    assert len(tail) % 2 == 0 and width % QUERY_SUB == 0
    assert tail[0][0] is None and tail[1][0] is None
    cat = lambda parts: jnp.concatenate(parts, axis=1)

    def fetch(c, par, bias, may_be_missing):
        block_scores = scores(c)
        missing = jnp.where(c < 0, NEG, 0.0) if may_be_missing else None
        col_max = []
        for blk in range(n_blocks):
            cols = slice(blk * QUERY_SUB, (blk + 1) * QUERY_SUB)
            part = block_scores(cols)
            extra = None if bias is None else bias(blk)
            if extra is not None:
                part = part + extra
            if missing is not None:
                part = part + missing
            s_buf[par, :, cols] = part
            col_max.append(jnp.max(part, axis=0, keepdims=True))
        return cat(col_max)

    def step(j, par, stats, nxt=None):
        m, corr1, corr2, cmax = stats
        acc_ref[...] = acc_ref[...] * corr2 + _dot(values(j - 2), p_buf[par])
        m_new = jnp.maximum(m, cmax[par])
        for blk in range(n_blocks):
            cols = slice(blk * QUERY_SUB, (blk + 1) * QUERY_SUB)
            p_buf[par, :, cols] = jnp.exp2((s_buf[par, :, cols] - m_new[:, cols]).astype(BF16))
        if nxt is not None:
            cmax = tuple(fetch(j + 2, par, *nxt) if q == par else cmax[q] for q in range(2))
        return m_new, jnp.exp2(m - m_new), corr1, cmax

    far = (None, False)

    def pairs(j, n, stats):
        for k in range(n):
            stats = step(j + 2 * k + 1, 1, step(j + 2 * k, 0, stats, far), far)
        return stats

    p_buf[...] = jnp.zeros_like(p_buf)
    acc_ref[...] = jnp.zeros_like(acc_ref)
    ones = jnp.ones((1, width), F32)
    cmax = (fetch(first, 0, None, True), fetch(first + 1, 1, None, True))
    stats = (jnp.full((1, width), NEG, F32), ones, ones, cmax)
    start = first
    for n in (1, 2):
        if n == 1 and far_pairs_even:
            continue
        group = (n_far_pairs // n) % 2
        stats = lax.cond(group == 1, lambda st, j=start, n=n: pairs(j, n, st), lambda st: st, stats)
        start = start + 2 * n * group
    stats = lax.fori_loop(0, n_far_pairs // PAIRS_PER_TRIP,
                          lambda jj, st: pairs(start + 2 * PAIRS_PER_TRIP * jj, PAIRS_PER_TRIP, st), stats)
    last = first + 2 * n_far_pairs + len(tail)
    for r in range(len(tail)):
        stats = step(last - len(tail) + r, r % 2, stats, tail[r + 2] if r + 2 < len(tail) else None)
    _, corr1, corr2, _ = stats
    acc = acc_ref[...] * corr2 + _dot(values(last - 2), p_buf[0])
    acc = acc * corr1 + _dot(values(last - 1), p_buf[1])
    return acc[:HEAD_DIM] / acc[HEAD_DIM:HEAD_DIM + 1]


def _fox_attn_kernel(qt_ref, kp_ref, vt_ref, o_ref, s_buf, p_buf, acc_ref):
    i = pl.program_id(2)
    t = qt_ref.shape[3]
    per_tile = t // KEY_CHUNK

    def scores(c):
        rows = pl.ds(pl.multiple_of(jnp.maximum(c, 0) * KEY_CHUNK, KEY_CHUNK), KEY_CHUNK)
        k = kp_ref[0, 0, rows, :]
        return lambda cols: _dot(k, qt_ref[0, 0, :, cols])

    def values(c):
        return vt_ref[0, 0, jnp.maximum(c, 0)]

    key = lax.broadcasted_iota(jnp.int32, (KEY_CHUNK, QUERY_SUB), 0)
    qry = lax.broadcasted_iota(jnp.int32, (KEY_CHUNK, QUERY_SUB), 1)
    causal = lambda r: (lambda c: jnp.where(key + r * KEY_CHUNK <= qry + c * QUERY_SUB, 0.0, NEG))
    assert per_tile % 4 == 0
    tail = [(None, True)] * per_tile + [(causal(r), False) for r in range(per_tile)]
    out = _flash_sweep(per_tile * jnp.minimum(i - 1, 0), (per_tile // 2) * jnp.maximum(i - 1, 0),
                       scores, values, tail, s_buf, p_buf, acc_ref, far_pairs_even=True)
    o_ref[0] = out.astype(o_ref.dtype)


def _fox_attention(qt, kp, vt):
    B, H, S, _ = kp.shape
    t = FOX_T
    return pl.pallas_call(
        _fox_attn_kernel, name="fox_attn",
        grid=(B, H, S // t),
        in_specs=[
            pl.BlockSpec((1, 1, LANES, t), lambda b, h, i: (b, h, 0, i)),
            pl.BlockSpec((1, 1, S, LANES), lambda b, h, i: (b, h, 0, 0)),
            pl.BlockSpec((1, 1, S // KEY_CHUNK, V_ROWS, KEY_CHUNK), lambda b, h, i: (b, h, 0, 0, 0)),
        ],
        out_specs=pl.BlockSpec((1, HEAD_DIM, t), lambda b, h, i: (b, h, i)),
        out_shape=jax.ShapeDtypeStruct((B, H * HEAD_DIM, S), BF16),
        scratch_shapes=[pltpu.VMEM((2, KEY_CHUNK, t), F32), pltpu.VMEM((2, KEY_CHUNK, t), BF16),
                        pltpu.VMEM((V_ROWS, t), F32)],
        compiler_params=_params("arbitrary", "arbitrary", "arbitrary"),
    )(qt, kp, vt)


def _block_tail_kernel(at_ref, wo_ref, x_ref, ga_ref, g1_ref, b1_ref, sc_ref, sh_ref, gf_ref,
                       wa_ref, wb_ref, w2_ref, g2_ref, b2_ref, o_ref):
    y = lax.dot_general(at_ref[0], wo_ref[...], (((0,), (0,)), ((), ())), preferred_element_type=F32)
    x1 = _layer_norm(DN_ALPHA * x_ref[0] + (1.0 + ga_ref[0]) * y, g1_ref[...], b1_ref[...])
    hb = (x1 * (1.0 + sc_ref[0]) + sh_ref[0]).astype(BF16)
    a = _dot(hb, wa_ref[...])
    b = _dot(hb, wb_ref[...])
    u = (a / (1.0 + jnp.exp(-a)) * b).astype(BF16)
    z = DN_ALPHA * x1 + (1.0 + gf_ref[0]) * _dot(u, w2_ref[...])
    o_ref[0] = _layer_norm(z, g2_ref[...], b2_ref[...])


def _block_tail(attn_t, w_out, x, gate_a, ln1, sc, sh, gate_f, w_in, w_out_ffn, ln2):
    B, S, _ = x.shape
    tm = ROW_TILE
    w_in = w_in.astype(BF16)
    row = lambda r: r.reshape(1, -1)
    mod = pl.BlockSpec((1, 1, D_MODEL), lambda b, i: (b, 0, 0))
    vec = pl.BlockSpec((1, D_MODEL), lambda b, i: (0, 0))
    resident = lambda shape, index: pl.BlockSpec(shape, index, pipeline_mode=pl.Buffered(1))
    return pl.pallas_call(
        _block_tail_kernel, name="outproj_ffn_ln",
        grid=(B, S // tm),
        in_specs=[
            pl.BlockSpec((1, D_MODEL, tm), lambda b, i: (b, 0, i)),
            resident((D_MODEL, D_MODEL), lambda b, i: (0, 0)),
            pl.BlockSpec((1, tm, D_MODEL), lambda b, i: (b, i, 0)),
            mod, vec, vec,
            mod, mod, mod,
            resident((D_MODEL, FF_HIDDEN), lambda b, i: (0, 0)),
            resident((D_MODEL, FF_HIDDEN), lambda b, i: (0, 1)),
            resident((FF_HIDDEN, D_MODEL), lambda b, i: (0, 0)),
            vec, vec,
        ],
        out_specs=pl.BlockSpec((1, tm, D_MODEL), lambda b, i: (b, i, 0)),
        out_shape=jax.ShapeDtypeStruct((B, S, D_MODEL), F32),
        compiler_params=_params("arbitrary", "arbitrary"),
    )(attn_t, w_out.astype(BF16), x, gate_a, row(ln1[0]), row(ln1[1]), sc, sh, gate_f,
      w_in, w_in, w_out_ffn.astype(BF16), row(ln2[0]), row(ln2[1]))


def _nsa_proj_kernel(x_ref, sc_ref, sh_ref, wqt_ref, wnat_ref, wvt_ref, wgt_ref, bg_ref,
                     qt_ref, kc_ref, vc_ref, ks_ref, kw_ref, vst_ref, vwt_ref, gt_ref):
    tm = x_ref.shape[1]
    t = NSA_T
    hb = (x_ref[0] * (1.0 + sc_ref[0]) + sh_ref[0]).astype(BF16)

    qt = _dot_nt(wqt_ref[...], hb).astype(BF16)
    for g in range(N_GROUPS):
        for hh in range(HEADS_PER_GROUP):
            r0 = (g * HEADS_PER_GROUP + hh) * HEAD_DIM
            for c in range(tm // t):
                qt_ref[0, g, c, :, hh * t:(hh + 1) * t] = qt[r0:r0 + HEAD_DIM, c * t:(c + 1) * t]

    nat = _dot(hb, wnat_ref[...])
    lane = lax.broadcasted_iota(jnp.int32, (tm, LANES), 1)
    ones = ((lane == HEAD_DIM) | (lane == HEAD_DIM + 1)).astype(F32)
    off_ks = 2 * KV_WIDTH
    off_kw = off_ks + N_GROUPS * LANES
    for g in range(N_GROUPS):
        kc_ref[0, g] = nat[:, g * HEAD_DIM:(g + 1) * HEAD_DIM].astype(BF16)
        vc_ref[0, g] = nat[:, KV_WIDTH + g * HEAD_DIM:KV_WIDTH + (g + 1) * HEAD_DIM].astype(BF16)
        ks_ref[0, g] = (nat[:, off_ks + g * LANES:off_ks + (g + 1) * LANES] + ones).astype(BF16)
        kw_ref[0, g] = nat[:, off_kw + g * LANES:off_kw + (g + 1) * LANES].astype(BF16)

    vt = _dot_nt(wvt_ref[...], hb).astype(BF16)
    for g in range(N_GROUPS):
        for c in range(tm // KEY_CHUNK):
            vst_ref[0, g, c, 0:HEAD_DIM, :] = vt[g * HEAD_DIM:(g + 1) * HEAD_DIM, c * KEY_CHUNK:(c + 1) * KEY_CHUNK]
            vst_ref[0, g, c, HEAD_DIM:V_ROWS, :] = _ones_row_tail(KEY_CHUNK)
        for c in range(tm // t):
            vwt_ref[0, g, c, 0:HEAD_DIM, :] = vt[KV_WIDTH + g * HEAD_DIM:KV_WIDTH + (g + 1) * HEAD_DIM,
                                                 c * t:(c + 1) * t]
            vwt_ref[0, g, c, HEAD_DIM:V_ROWS, :] = _ones_row_tail(t)

    gl = _dot_nt(wgt_ref[...], hb) + bg_ref[...]
    gates = 1.0 / (1.0 + jnp.exp(-gl))
    assert tm == t
    gt_ref[...] = jnp.zeros_like(gt_ref)
    for br in range(3):
        for g in range(N_GROUPS):
            for hh in range(HEADS_PER_GROUP):
                row = br * N_HEADS + g * HEADS_PER_GROUP + hh
                gt_ref[0, g, 0, br:br + 1, hh * t:(hh + 1) * t] = gates[row:row + 1, :]


def _nsa_proj(x, sc, sh, w_in, b_gate):
    B, S, _ = x.shape
    tm = ROW_TILE
    t = NSA_T
    scale = LOG2E * HEAD_DIM ** -0.5
    cuts = [D_MODEL + n * KV_WIDTH for n in range(7)]
    wq, wkc, wvc, wks, wvs, wkw, wvw, wg = jnp.split(w_in, cuts, axis=1)
    wqt = (wq * scale).T.astype(BF16)
    padk = lambda w: jnp.pad(w.reshape(D_MODEL, N_GROUPS, HEAD_DIM),
                             ((0, 0), (0, 0), (0, LANES - HEAD_DIM))).reshape(D_MODEL, N_GROUPS * LANES)
    wnat = jnp.concatenate([wkc, wvc, padk(wks), padk(wkw)], axis=1).astype(BF16)
    wvt = jnp.concatenate([wvs, wvw], axis=1).T.astype(BF16)
    n_gate = 3 * N_HEADS
    wgt = wg.T.astype(BF16)
    bg = jnp.broadcast_to(b_gate.reshape(n_gate, 1), (n_gate, tm))

    full = lambda a: pl.BlockSpec(a.shape, lambda b, i: (0,) * a.ndim)
    nat_spec = pl.BlockSpec((1, N_GROUPS, tm, HEAD_DIM), lambda b, i: (b, 0, i, 0))
    pad_spec = pl.BlockSpec((1, N_GROUPS, tm, LANES), lambda b, i: (b, 0, i, 0))
    vt_spec = lambda n, rows: pl.BlockSpec((1, N_GROUPS, tm // n, rows, n), lambda b, i: (b, 0, i, 0, 0))
    return pl.pallas_call(
        _nsa_proj_kernel, name="nsa_proj",
        grid=(B, S // tm),
        in_specs=[
            pl.BlockSpec((1, tm, D_MODEL), lambda b, i: (b, i, 0)),
            pl.BlockSpec((1, 1, D_MODEL), lambda b, i: (b, 0, 0)),
            pl.BlockSpec((1, 1, D_MODEL), lambda b, i: (b, 0, 0)),
            full(wqt), full(wnat), full(wvt), full(wgt), full(bg),
        ],
        out_specs=[
            pl.BlockSpec((1, N_GROUPS, tm // t, HEAD_DIM, HEADS_PER_GROUP * t), lambda b, i: (b, 0, i, 0, 0)),
            nat_spec, nat_spec, pad_spec, pad_spec, vt_spec(KEY_CHUNK, V_ROWS), vt_spec(t, V_ROWS),
            pl.BlockSpec((1, N_GROUPS, 1, SUBLANES, HEADS_PER_GROUP * t), lambda b, i: (b, 0, i, 0, 0)),
        ],
        out_shape=[
            jax.ShapeDtypeStruct((B, N_GROUPS, S // t, HEAD_DIM, HEADS_PER_GROUP * t), BF16),
            jax.ShapeDtypeStruct((B, N_GROUPS, S, HEAD_DIM), BF16),
            jax.ShapeDtypeStruct((B, N_GROUPS, S, HEAD_DIM), BF16),
            jax.ShapeDtypeStruct((B, N_GROUPS, S, LANES), BF16),
            jax.ShapeDtypeStruct((B, N_GROUPS, S, LANES), BF16),
            jax.ShapeDtypeStruct((B, N_GROUPS, S // KEY_CHUNK, V_ROWS, KEY_CHUNK), BF16),
            jax.ShapeDtypeStruct((B, N_GROUPS, S // t, V_ROWS, t), BF16),
            jax.ShapeDtypeStruct((B, N_GROUPS, S // t, SUBLANES, HEADS_PER_GROUP * t), F32),
        ],
        compiler_params=_params("arbitrary", "arbitrary"),
    )(x, sc, sh, wqt, wnat, wvt, wgt, bg)


def _compress_kernel(tk_ref, tv_ref, pos_ref, w1_ref, w2k_ref, w2vt_ref, kc_ref, vct_ref):
    n = tk_ref.shape[2]
    half = CMP_STRIDE * HEAD_DIM

    def hidden(t_ref, idx):
        t16 = t_ref[0, 0]
        xa = (t16 + pos_ref[idx, 0:1, :]).astype(BF16)
        xb = (t16 + pos_ref[idx, 1:2, :]).astype(BF16)
        first = _dot(xa, w1_ref[idx, :half, :])
        second = _dot(xb, w1_ref[idx, half:, :])
        pre = first + pltpu.roll(second, n - 1, 0)
        return (pre / (1.0 + jnp.exp(-pre))).astype(BF16)

    kc = _dot(hidden(tk_ref, 0), w2k_ref[...])
    lane = lax.broadcasted_iota(jnp.int32, kc.shape, 1)
    ones = ((lane == HEAD_DIM) | (lane == HEAD_DIM + 1)).astype(F32)
    kc_ref[0, 0] = (kc + ones).astype(BF16)
    vct = _dot_nt(w2vt_ref[...], hidden(tv_ref, 1)).astype(BF16)
    for r in range(n // CMP_ROWS):
        vct_ref[0, 0, r] = vct[:, r * CMP_ROWS:(r + 1) * CMP_ROWS]


def _compress(kc, vc, cmp_pos, cmp_w1, cmp_w2):
    B, G, S, _ = kc.shape
    n = S // CMP_STRIDE
    width = CMP_STRIDE * HEAD_DIM
    tk = kc.reshape(B, G, n, width)
    tv = vc.reshape(B, G, n, width)
    pos = cmp_pos.reshape(2, 2, width)
    w1 = cmp_w1.astype(BF16)
    w2k = jnp.pad(cmp_w2[0], ((0, 0), (0, LANES - HEAD_DIM))).astype(BF16)
    w2vt = cmp_w2[1].T.astype(BF16)
    full = lambda a: pl.BlockSpec(a.shape, lambda b, g: (0,) * a.ndim)
    t_spec = pl.BlockSpec((1, 1, n, width), lambda b, g: (b, g, 0, 0))
    return pl.pallas_call(
        _compress_kernel, name="nsa_compress",
        grid=(B, G),
        in_specs=[t_spec, t_spec, full(pos), full(w1), full(w2k), full(w2vt)],
        out_specs=[
            pl.BlockSpec((1, 1, n, LANES), lambda b, g: (b, g, 0, 0)),
            pl.BlockSpec((1, 1, n // CMP_ROWS, HEAD_DIM, CMP_ROWS), lambda b, g: (b, g, 0, 0, 0)),
        ],
        out_shape=[
            jax.ShapeDtypeStruct((B, G, n, LANES), BF16),
            jax.ShapeDtypeStruct((B, G, n // CMP_ROWS, HEAD_DIM, CMP_ROWS), BF16),
        ],
        compiler_params=_params("arbitrary", "arbitrary"),
    )(tk, tv, pos, w1, w2k, w2vt)


def _cmp_band_rows(t):
    return t // CMP_STRIDE + NEAR_CMP_BLOCKS


def _cmp_band_start(i, t):
    return jnp.maximum(i * t // CMP_STRIDE - NEAR_CMP_BLOCKS, 0)


def _bias_tiles_kernel(rb_ref, sel_ref, win_ref, cmp_ref):
    hd = pl.program_id(0)
    far = rb_ref[FAR_BUCKET, hd]

    def dist(n, back):
        key = lax.broadcasted_iota(jnp.int32, (n, n), 0)
        qry = lax.broadcasted_iota(jnp.int32, (n, n), 1)
        return qry - key + back * n

    kc = sel_ref.shape[2]
    d_prev, d_diag = dist(kc, 1), dist(kc, 0)
    sel_ref[0, 0] = LOG2E * (_bucket_bias(d_prev, rb_ref, hd) - far)
    sel_ref[1, 0] = jnp.where(d_diag >= 0, LOG2E * (_bucket_bias(d_diag, rb_ref, hd) - far), NEG)
    t = win_ref.shape[2]
    n_back = win_ref.shape[0] - 1
    for r in range(n_back + 1):
        d = dist(t, n_back - r)
        win_ref[r, 0] = jnp.where((d >= 0) & (d < WINDOW), LOG2E * _bucket_bias(d, rb_ref, hd), NEG)
    band = cmp_ref.shape[2]
    blk = lax.broadcasted_iota(jnp.int32, (band, t), 0)
    qry = lax.broadcasted_iota(jnp.int32, (band, t), 1)
    for later in range(2):
        d = (later * t + qry) - ((_cmp_band_start(later, t) + blk) * CMP_STRIDE + CMP_LEN - 1)
        cmp_ref[later, 0] = LOG2E * (_bucket_bias(d, rb_ref, hd) - far)


def _bias_tiles(rel_bias):
    t = NSA_T
    n_win = WINDOW // t + 1
    band = _cmp_band_rows(t)
    assert WINDOW % t == 0 and (FAR_DIST + CMP_LEN - 1) <= NEAR_CMP_BLOCKS * CMP_STRIDE
    spec = lambda n, rows, cols: pl.BlockSpec((n, 1, rows, cols), lambda h: (0, h, 0, 0))
    return pl.pallas_call(
        _bias_tiles_kernel, name="rel_bias_tiles",
        grid=(N_HEADS,),
        in_specs=[pl.BlockSpec(memory_space=pltpu.SMEM)],
        out_specs=[spec(2, KEY_CHUNK, KEY_CHUNK), spec(n_win, t, t), spec(2, band, t)],
        out_shape=[
            jax.ShapeDtypeStruct((2, N_HEADS, KEY_CHUNK, KEY_CHUNK), F32),
            jax.ShapeDtypeStruct((n_win, N_HEADS, t, t), F32),
            jax.ShapeDtypeStruct((2, N_HEADS, band, t), F32),
        ],
        compiler_params=_params("arbitrary"),
    )(rel_bias)


def _far_bias_rows(rb_ref, g, width):
    t = width // HEADS_PER_GROUP
    lane = lax.broadcasted_iota(jnp.int32, (HEAD_DIM, width), 1)
    row = lax.broadcasted_iota(jnp.int32, (HEAD_DIM, width), 0)
    far = jnp.zeros((HEAD_DIM, width), F32)
    for hh in range(HEADS_PER_GROUP):
        far = jnp.where(lane >= hh * t, LOG2E * rb_ref[FAR_BUCKET, g * HEADS_PER_GROUP + hh], far)
    hi = far.astype(BF16).astype(F32)
    return jnp.where(row == 0, hi, jnp.where(row == 1, far - hi, 0.0)).astype(BF16)


def _nsa_cmp_kernel(rb_ref, qt_ref, kc_ref, vct_ref, at_ref, band_ref, oc_ref, sb_ref,
                    qp_ref, s_ref, imp_ref):
    g = pl.program_id(1)
    i = pl.program_id(2)
    width = qt_ref.shape[4]
    t = width // HEADS_PER_GROUP
    n = kc_ref.shape[2]
    nb = sb_ref.shape[3]
    t0 = i * t
    assert t & (t - 1) == 0 and n % CMP_ROWS == 0

    n_vis = (t0 + t - CMP_LEN) // CMP_STRIDE // CMP_ROWS + 1
    n_full = jnp.maximum((t0 - (CMP_LEN - 1)) // CMP_STRIDE + 1, 0) // CMP_ROWS
    rows_of = lambda r: pl.ds(pl.multiple_of(r * CMP_ROWS, CMP_ROWS), CMP_ROWS)

    def valid(r):
        blk = r * CMP_ROWS + lax.broadcasted_iota(jnp.int32, (CMP_ROWS, width), 0)
        qry = t0 + (lax.broadcasted_iota(jnp.int32, (CMP_ROWS, width), 1) & (t - 1))
        return qry >= blk * CMP_STRIDE + CMP_LEN - 1

    qp_ref[0:HEAD_DIM, :] = qt_ref[0, 0, 0]
    qp_ref[HEAD_DIM:LANES, :] = _far_bias_rows(rb_ref, g, width)

    n_early = _cmp_band_start(i, t) // CMP_ROWS

    def score_early(r, m):
        s = _dot(kc_ref[0, 0, rows_of(r), :], qp_ref[...])
        s_ref[rows_of(r), :] = s
        return jnp.maximum(m, jnp.max(s, axis=0, keepdims=True))

    def score_late(r, carry):
        s_ref[rows_of(r), :] = _dot(kc_ref[0, 0, rows_of(r), :], qp_ref[...])
        return carry

    m = lax.fori_loop(0, n_early, score_early, jnp.full((1, width), NEG, F32))
    lax.fori_loop(n_early, n_vis, score_late, 0)

    band = band_ref.shape[2]
    assert band == _cmp_band_rows(t) and band <= n
    r0 = pl.multiple_of(_cmp_band_start(i, t), SUBLANES)
    for hh in range(HEADS_PER_GROUP):
        s_ref[pl.ds(r0, band), hh * t:(hh + 1) * t] += band_ref[jnp.minimum(i, 1), hh]

    def col_max(masked):
        def body(r, m):
            s = s_ref[rows_of(r), :]
            if masked:
                s = jnp.where(valid(r), s, NEG)
            return jnp.maximum(m, jnp.max(s, axis=0, keepdims=True))
        return body

    m = lax.fori_loop(n_early, n_full, col_max(False), m)
    m = lax.fori_loop(jnp.maximum(n_full, n_early), n_vis, col_max(True), m)

    def exp_sum(masked):
        def body(r, l):
            p = jnp.exp2(s_ref[rows_of(r), :] - m)
            if masked:
                p = jnp.where(valid(r), p, 0.0)
            s_ref[rows_of(r), :] = p
            return l + jnp.sum(p, axis=0, keepdims=True)
        return body

    l = lax.fori_loop(0, n_full, exp_sum(False), jnp.zeros((1, width), F32))
    l = lax.fori_loop(n_full, n_vis, exp_sum(True), l)
    inv = 1.0 / jnp.maximum(l, TINY)

    oc_ref[0, 0, 0] = jnp.zeros((HEAD_DIM, width), F32)
    imp_ref[...] = jnp.zeros_like(imp_ref)

    def finish(r, carry):
        p = s_ref[rows_of(r), :] * inv
        oc_ref[0, 0, 0] += _dot(vct_ref[0, 0, r], p.astype(BF16))
        imp = p[:, 0:t]
        for hh in range(1, HEADS_PER_GROUP):
            imp = imp + p[:, hh * t:(hh + 1) * t]
        hi, mid, lo = _split3(imp)
        a = at_ref[...]
        band = pl.ds(pl.multiple_of(r * (CMP_ROWS * CMP_STRIDE // SEL_BLOCK), SUBLANES), a.shape[0])
        imp_ref[band, :] += _dot(a, hi) + _dot(a, mid) + _dot(a, lo)
        return carry

    lax.fori_loop(0, n_vis, finish, 0)

    def select_among(rows):
        def run():
            sblk = lax.broadcasted_iota(jnp.int32, (rows, t), 0)
            cur = (t0 + lax.broadcasted_iota(jnp.int32, (rows, t), 1)) // SEL_BLOCK
            forced = (sblk == 0) | (sblk == cur) | (sblk == cur - 1)
            is_cand = (sblk >= 1) & (sblk <= cur - 2)
            cand = jnp.where(is_cand, imp_ref[0:rows, :], -1.0)
            sblk_f = sblk.astype(F32)
            for _ in range(SEL_TOPK - N_FORCED):
                best = jnp.max(cand, axis=0, keepdims=True)
                first = jnp.min(jnp.where(cand == best, sblk_f, float(nb)), axis=0, keepdims=True)
                cand = jnp.where(sblk_f == first, -1.0, cand)
            chosen = forced | (is_cand & (cand < 0.0))
            sb_ref[0, 0, 0, 0:rows, :] = jnp.where(chosen, 0.0, NEG).astype(BF16)
            if rows < nb:
                sb_ref[0, 0, 0, rows:nb, :] = jnp.full((nb - rows, t), NEG, BF16)
        return run

    last_block = (t0 + t - 1) // SEL_BLOCK
    lax.switch(jnp.minimum(last_block // TOPK_ROWS, nb // TOPK_ROWS - 1),
               [select_among(rows) for rows in range(TOPK_ROWS, nb + 1, TOPK_ROWS)])


def _nsa_cmp(rel_bias, qt, kcmp, vcmp_t, cmp_band, n_sel):
    B, G, nq, _, width = qt.shape
    t = width // HEADS_PER_GROUP
    n = kcmp.shape[2]
    n_cmp = n - 1
    nb = -(-n_sel // LANES) * LANES
    R = SEL_BLOCK // CMP_STRIDE
    assert n_cmp == n - 1 and nb >= n_sel
    band_rows = -(-(CMP_ROWS // R + 1) // 8) * 8
    at = np.zeros((band_rows, CMP_ROWS), np.float32)
    for j in range(CMP_ROWS // R + 1):
        at[j, max(R * j - 1, 0):min(R * j + R, CMP_ROWS)] = 1.0
    at = jnp.asarray(at, BF16)
    return pl.pallas_call(
        _nsa_cmp_kernel, name="nsa_cmp_topk",
        grid=(B, G, nq),
        in_specs=[
            pl.BlockSpec(memory_space=pltpu.SMEM),
            pl.BlockSpec((1, 1, 1, HEAD_DIM, width), lambda b, g, i: (b, g, i, 0, 0)),
            pl.BlockSpec((1, 1, n, LANES), lambda b, g, i: (b, g, 0, 0)),
            pl.BlockSpec((1, 1, n // CMP_ROWS, HEAD_DIM, CMP_ROWS), lambda b, g, i: (b, g, 0, 0, 0)),
            pl.BlockSpec((band_rows, CMP_ROWS), lambda b, g, i: (0, 0)),
            pl.BlockSpec((2, HEADS_PER_GROUP) + cmp_band.shape[2:], lambda b, g, i: (0, g, 0, 0)),
        ],
        out_specs=[
            pl.BlockSpec((1, 1, 1, HEAD_DIM, width), lambda b, g, i: (b, g, i, 0, 0)),
            pl.BlockSpec((1, 1, 1, nb, t), lambda b, g, i: (b, g, i, 0, 0)),
        ],
        out_shape=[
            jax.ShapeDtypeStruct((B, G, nq, HEAD_DIM, width), F32),
            jax.ShapeDtypeStruct((B, G, nq, nb, t), BF16),
        ],
        scratch_shapes=[pltpu.VMEM((LANES, width), BF16), pltpu.VMEM((n, width), F32),
                        pltpu.VMEM((nb + band_rows, t), F32)],
        compiler_params=_params("arbitrary", "arbitrary", "arbitrary"),
    )(rel_bias, qt, kcmp, vcmp_t, at, cmp_band)


def _nsa_win_kernel(qt_ref, *refs):
    n_win = (len(refs) - 5) // 2
    k_refs, v_refs = refs[:n_win], refs[n_win:2 * n_win]
    wb_ref, ow_ref, qp_ref, s_ref, p_ref = refs[2 * n_win:]
    i = pl.program_id(2)
    width = qt_ref.shape[4]
    t = width // HEADS_PER_GROUP
    qt = qt_ref[0, 0, 0]
    qp_ref[0:HEAD_DIM, :] = qt
    qp_ref[HEAD_DIM:LANES, :] = jnp.zeros_like(qt)
    m = None
    for c in range(n_win):
        k = k_refs[c][0, 0]
        missing = jnp.where(i - (n_win - 1 - c) < 0, NEG, 0.0)
        col_max = []
        for blk in range(width // QUERY_SUB):
            cols = slice(blk * QUERY_SUB, (blk + 1) * QUERY_SUB)
            hh, off = divmod(blk * QUERY_SUB, t)
            part = _dot(k, qp_ref[:, cols]) + wb_ref[c, hh, :, off:off + QUERY_SUB] + missing
            s_ref[c, :, cols] = part
            col_max.append(jnp.max(part, axis=0, keepdims=True))
        col_max = jnp.concatenate(col_max, axis=1)
        m = col_max if m is None else jnp.maximum(m, col_max)
    acc = jnp.zeros((V_ROWS, width), F32)
    for c in range(n_win):
        for blk in range(width // QUERY_SUB):
            cols = slice(blk * QUERY_SUB, (blk + 1) * QUERY_SUB)
            p_ref[c, :, cols] = jnp.exp2((s_ref[c, :, cols] - m[:, cols]).astype(BF16))
        acc = acc + _dot(v_refs[c][0, 0, 0], p_ref[c])
    ow_ref[0, 0, 0] = acc[:HEAD_DIM] / acc[HEAD_DIM:HEAD_DIM + 1]


def _nsa_window(qt, kw, vwt, win_bias):
    B, G, nq, _, width = qt.shape
    t = width // HEADS_PER_GROUP
    n_win = win_bias.shape[0]
    backs = list(range(n_win - 1, -1, -1))
    k_spec = lambda back: pl.BlockSpec((1, 1, t, LANES), lambda b, g, i: (b, g, jnp.maximum(i - back, 0), 0))
    v_spec = lambda back: pl.BlockSpec((1, 1, 1, V_ROWS, t),
                                       lambda b, g, i: (b, g, jnp.maximum(i - back, 0), 0, 0))
    return pl.pallas_call(
        _nsa_win_kernel, name="nsa_window",
        grid=(B, G, nq),
        in_specs=[pl.BlockSpec((1, 1, 1, HEAD_DIM, width), lambda b, g, i: (b, g, i, 0, 0))]
        + [k_spec(back) for back in backs] + [v_spec(back) for back in backs]
        + [pl.BlockSpec((n_win, HEADS_PER_GROUP, t, t), lambda b, g, i: (0, g, 0, 0))],
        out_specs=pl.BlockSpec((1, 1, 1, HEAD_DIM, width), lambda b, g, i: (b, g, i, 0, 0)),
        out_shape=jax.ShapeDtypeStruct((B, G, nq, HEAD_DIM, width), F32),
        scratch_shapes=[pltpu.VMEM((LANES, width), BF16),
                        pltpu.VMEM((n_win, t, width), F32), pltpu.VMEM((n_win, t, width), BF16)],
        compiler_params=_params("arbitrary", "arbitrary", "arbitrary"),
    )(qt, *([kw] * n_win), *([vwt] * n_win), win_bias)


def _nsa_sel_kernel(rb_ref, qt_ref, sb_ref, ks_ref, e_ref, vst_ref, cb_ref, oc_ref, ow_ref, gt_ref,
                    o_ref, qp_ref, s_buf, p_buf, acc_ref):
    g = pl.program_id(1)
    i = pl.program_id(2)
    width = qt_ref.shape[4]
    t = width // HEADS_PER_GROUP
    per_tile = t // KEY_CHUNK
    chunks_per_slab = LANES * SEL_BLOCK // KEY_CHUNK
    blocks_per_head = t // QUERY_SUB
    assert per_tile == 2 and t % QUERY_SUB == 0

    far_rows = _far_bias_rows(rb_ref, g, width)
    for slab in range(qp_ref.shape[0]):
        qp_ref[slab, 0:HEAD_DIM, :] = qt_ref[0, 0, 0]
        qp_ref[slab, HEAD_DIM:LANES, :] = far_rows
        sb = sb_ref[0, 0, 0, slab * LANES:(slab + 1) * LANES, :]
        qp_ref[slab, LANES:2 * LANES, :] = jnp.concatenate([sb] * HEADS_PER_GROUP, axis=1)

    def scores(c):
        c = jnp.maximum(c, 0)
        rows = pl.ds(pl.multiple_of(c * KEY_CHUNK, KEY_CHUNK), KEY_CHUNK)
        kp = jnp.concatenate([ks_ref[0, 0, rows, :], e_ref[rows, :]], axis=1)
        return lambda cols: _dot(kp, qp_ref[c // chunks_per_slab, :, cols])

    def values(c):
        return vst_ref[0, 0, jnp.maximum(c, 0)]

    def near_bias(r):
        def bias(c):
            hh, part = divmod(c, blocks_per_head)
            ahead = part * QUERY_SUB // KEY_CHUNK - (r - per_tile)
            if ahead == 0:
                return cb_ref[1, hh]
            if ahead == 1:
                return cb_ref[0, hh]
            if ahead < 0:
                return jnp.full((KEY_CHUNK, QUERY_SUB), NEG, F32)
            return None
        return bias

    tail = [(None, True), (None, True)] + [(near_bias(r), r < per_tile) for r in range(2 * per_tile)]
    o_sel = _flash_sweep(per_tile * jnp.minimum(i - 2, 0), jnp.maximum(i - 2, 0), scores, values, tail,
                         s_buf, p_buf, acc_ref)

    out = gt_ref[0, 0, 0, 0:1, :] * oc_ref[0, 0, 0] + gt_ref[0, 0, 0, 1:2, :] * o_sel \
        + gt_ref[0, 0, 0, 2:3, :] * ow_ref[0, 0, 0]
    for hh in range(HEADS_PER_GROUP):
        o_ref[0, hh * HEAD_DIM:(hh + 1) * HEAD_DIM, :] = out[:, hh * t:(hh + 1) * t].astype(o_ref.dtype)


def _nsa_select_combine(rel_bias, qt, sel_bias, ks, vst, sel_corr, oc, ow, gates):
    B, G, nq, _, width = qt.shape
    t = width // HEADS_PER_GROUP
    S = nq * t
    nb = sel_bias.shape[3]
    blocks = (np.arange(S) // SEL_BLOCK) % LANES
    onehot = jnp.asarray(blocks[:, None] == np.arange(LANES)[None, :], BF16)
    tile = pl.BlockSpec((1, 1, 1, HEAD_DIM, width), lambda b, g, i: (b, g, i, 0, 0))
    return pl.pallas_call(
        _nsa_sel_kernel, name="nsa_select",
        grid=(B, G, nq),
        in_specs=[
            pl.BlockSpec(memory_space=pltpu.SMEM),
            tile,
            pl.BlockSpec((1, 1, 1, nb, t), lambda b, g, i: (b, g, i, 0, 0)),
            pl.BlockSpec((1, 1, S, LANES), lambda b, g, i: (b, g, 0, 0)),
            pl.BlockSpec((S, LANES), lambda b, g, i: (0, 0)),
            pl.BlockSpec((1, 1, S // KEY_CHUNK, V_ROWS, KEY_CHUNK), lambda b, g, i: (b, g, 0, 0, 0)),
            pl.BlockSpec((2, HEADS_PER_GROUP, KEY_CHUNK, KEY_CHUNK), lambda b, g, i: (0, g, 0, 0)),
            tile, tile,
            pl.BlockSpec((1, 1, 1, SUBLANES, width), lambda b, g, i: (b, g, i, 0, 0)),
        ],
        out_specs=pl.BlockSpec((1, HEADS_PER_GROUP * HEAD_DIM, t), lambda b, g, i: (b, g, i)),
        out_shape=jax.ShapeDtypeStruct((B, D_MODEL, S), BF16),
        scratch_shapes=[pltpu.VMEM((nb // LANES, 2 * LANES, width), BF16),
                        pltpu.VMEM((2, KEY_CHUNK, width), F32), pltpu.VMEM((2, KEY_CHUNK, width), BF16),
                        pltpu.VMEM((V_ROWS, width), F32)],
        compiler_params=_params("arbitrary", "arbitrary", "arbitrary"),
    )(rel_bias, qt, sel_bias, ks, onehot, vst, sel_corr, oc, ow, gates)


def _nsa_attention_t(x, sc, sh, w_in, b_gate, cmp_pos, cmp_w1, cmp_w2, rel_bias):
    B, S, _ = x.shape
    t = NSA_T
    n_sel = S // SEL_BLOCK
    assert S % ROW_TILE == 0 and n_sel >= SEL_TOPK and S // CMP_STRIDE >= t // CMP_STRIDE + NEAR_CMP_BLOCKS
    qt, kc, vc, ks, kw, vst, vwt, gt = _nsa_proj(x, sc, sh, w_in, b_gate)
    kcmp, vcmp_t = _compress(kc, vc, cmp_pos, cmp_w1, cmp_w2)
    sel_corr, win_bias, cmp_band = _bias_tiles(rel_bias)
    oc, sel_bias = _nsa_cmp(rel_bias, qt, kcmp, vcmp_t, cmp_band, n_sel)
    ow = _nsa_window(qt, kw, vwt, win_bias)
    return _nsa_select_combine(rel_bias, qt, sel_bias, ks, vst, sel_corr, oc, ow, gt)


def kernel(x, c, ada_w, ada_b, ln_g, ln_b, fox_w_in, fox_b_f, fox_w_out, nsa_w_in, nsa_b_gate,
           nsa_cmp_pos, nsa_cmp_w1, nsa_cmp_w2, nsa_w_out, rel_bias, ffn_w_in, ffn_w_out):
    B, S, _ = x.shape
    assert S % FOX_T == 0 and S % ROW_TILE == 0 and ROW_TILE == NSA_T
    mod = _modulation(c, ada_w, ada_b)
    for layer in range(DEPTH):
        sh_a, sc_a, g_a, sh_f, sc_f, g_f = [m.reshape(B, 1, D_MODEL) for m in jnp.split(mod[layer], 6, axis=-1)]
        j = layer // 2
        if layer % 2 == 0:
            kp, qt, vt = _fox_proj(x, sc_a, sh_a, fox_w_in[j], fox_b_f[j])
            attn_t = _fox_attention(qt, kp, vt)
            w_out = fox_w_out[j]
        else:
            attn_t = _nsa_attention_t(x, sc_a, sh_a, nsa_w_in[j], nsa_b_gate[j], nsa_cmp_pos[j],
                                      nsa_cmp_w1[j], nsa_cmp_w2[j], rel_bias)
            w_out = nsa_w_out[j]
        x = _block_tail(attn_t, w_out, x, g_a, (ln_g[layer, 0], ln_b[layer, 0]), sc_f, sh_f, g_f,
                        ffn_w_in[layer], ffn_w_out[layer], (ln_g[layer, 1], ln_b[layer, 1]))
    return x
```

```python
import math

import numpy as np
import jax
import jax.numpy as jnp
from jax import lax
from jax.experimental import pallas as pl
from jax.experimental.pallas import tpu as pltpu

F32 = jnp.float32
BF16 = jnp.bfloat16
HIGHEST = lax.Precision.HIGHEST

D_MODEL = 1024
HEAD_DIM = 64
N_HEADS = 16
N_GROUPS = 4
HEADS_PER_GROUP = 4
KV_WIDTH = N_GROUPS * HEAD_DIM
CMP_LEN = 32
CMP_STRIDE = 16
SEL_BLOCK = 64
SEL_TOPK = 16
WINDOW = 512
REL_BUCKETS = 32
FF_HIDDEN = 2816
DEPTH = 2
DN_ALPHA = (2 * DEPTH) ** 0.25
LN_EPS = 1e-5
NEG = -1e30
TINY = 1e-30

LANES = 128
SUBLANES = 8
VMEM_LIMIT = 56 * 1024 * 1024

ROW_TILE = 512
FOX_T = 1024
NSA_T = 512
MOD_TILE = 1536
N_FORCED = 3
NEAR_CMP_BLOCKS = 16

BUCKET_START = (0, 1, 2, 3, 4, 5, 6, 7, 8, 9, 10, 11, 12, 13, 14, 15,
                16, 19, 21, 24, 27, 31, 35, 40, 46, 52, 59, 67, 77, 87, 99, 113)
FAR_BUCKET = REL_BUCKETS - 1
FAR_DIST = BUCKET_START[FAR_BUCKET]

LOG2E = math.log2(math.e)
QUERY_SUB = 256
KEY_CHUNK = 256
PAIRS_PER_TRIP = 4
TOPK_ROWS = 64
CMP_ROWS = 128
BF16_ROWS = 2 * SUBLANES
V_ROWS = HEAD_DIM + BF16_ROWS
NT_DIMS = (((1,), (1,)), ((), ()))


def _params(*sem):
    return pltpu.CompilerParams(dimension_semantics=sem, vmem_limit_bytes=VMEM_LIMIT)


def _dot(a, b, **kw):
    return jnp.dot(a, b, preferred_element_type=F32, **kw)


def _dot_nt(a, b):
    return lax.dot_general(a, b, NT_DIMS, preferred_element_type=F32)


def _split3(v):
    hi = v.astype(BF16)
    r = v - hi.astype(F32)
    mid = r.astype(BF16)
    lo = (r - mid.astype(F32)).astype(BF16)
    return hi, mid, lo


def _layer_norm(z, g, b):
    mu = jnp.mean(z, axis=-1, keepdims=True)
    zc = z - mu
    var = jnp.mean(zc * zc, axis=-1, keepdims=True)
    return zc * lax.rsqrt(var + LN_EPS) * g + b


def _bucket_bias(dist, rb_ref, head):
    bias = jnp.full(dist.shape, rb_ref[0, head], F32)
    for k in range(1, REL_BUCKETS):
        bias = jnp.where(dist >= BUCKET_START[k], rb_ref[k, head], bias)
    return bias


def _mod_kernel(c_ref, w_ref, b_ref, o_ref):
    c = c_ref[...]
    cs = c / (1.0 + jnp.exp(-c))
    o_ref[0] = _dot(cs, w_ref[0], precision=HIGHEST) + b_ref[0]


def _modulation(c, ada_w, ada_b):
    B = c.shape[0]
    depth, _, n = ada_w.shape
    rows = SUBLANES
    c_pad = jnp.pad(c, ((0, rows - B), (0, 0)))
    tn = MOD_TILE
    out = pl.pallas_call(
        _mod_kernel, name="adaln_mod",
        grid=(depth, n // tn),
        in_specs=[
            pl.BlockSpec((rows, D_MODEL), lambda l, j: (0, 0)),
            pl.BlockSpec((1, D_MODEL, tn), lambda l, j: (l, 0, j)),
            pl.BlockSpec((1, 1, tn), lambda l, j: (l, 0, j)),
        ],
        out_specs=pl.BlockSpec((1, rows, tn), lambda l, j: (l, 0, j)),
        out_shape=jax.ShapeDtypeStruct((depth, rows, n), F32),
        compiler_params=_params("arbitrary", "arbitrary"),
    )(c_pad, ada_w, ada_b.reshape(depth, 1, n))
    return out[:, :B]


def _fox_proj_kernel(x_ref, sc_ref, sh_ref, wk_ref, wqt_ref, wvt_ref, wfh_ref, wfl_ref,
                     bf_ref, place_ref, kp_ref, qt_ref, vt_ref, carry_ref):
    tm = x_ref.shape[1]

    @pl.when(pl.program_id(1) == 0)
    def _():
        carry_ref[...] = jnp.zeros_like(carry_ref)

    h = x_ref[0] * (1.0 + sc_ref[0]) + sh_ref[0]
    hb = h.astype(BF16)
    hl = (h - hb.astype(F32)).astype(BF16)

    f = _dot(hb, wfh_ref[...]) + _dot(hl, wfh_ref[...]) + _dot(hb, wfl_ref[...])
    z = f + bf_ref[...]
    logf = jnp.minimum(z, 0.0) - jnp.log(1.0 + jnp.exp(-jnp.abs(z)))
    row = lax.broadcasted_iota(jnp.int32, (tm, tm), 0)
    col = lax.broadcasted_iota(jnp.int32, (tm, tm), 1)
    lower = (col <= row).astype(F32)
    cum = _dot(lower, logf, precision=HIGHEST) + carry_ref[0:1, :]
    carry_ref[...] = jnp.broadcast_to(cum[tm - 1:tm, :], carry_ref.shape)

    hi, mid, lo = [p.astype(F32) for p in _split3(-LOG2E * cum)]
    lane = lax.broadcasted_iota(jnp.int32, (tm, LANES), 1)
    pieces = jnp.where(lane < N_HEADS, hi, jnp.where(lane < 2 * N_HEADS, mid, lo)).astype(BF16)
    k = _dot(hb, wk_ref[...])
    kb = _dot(pieces, place_ref[...])
    qt = _dot_nt(wqt_ref[...], hb).astype(BF16)
    qrow = lax.broadcasted_iota(jnp.int32, (LANES - HEAD_DIM, tm), 0)
    q_ones = (qrow < 3).astype(F32).astype(BF16)
    vt = _dot_nt(wvt_ref[...], hb)
    for hd in range(N_HEADS):
        pair = slice((hd // 2) * LANES, (hd // 2 + 1) * LANES)
        own_half = (lane < HEAD_DIM) == (hd % 2 == 0)
        kp_ref[0, hd] = jnp.where(own_half, k[:, pair], kb[:, pair]).astype(BF16)
        q_rows = slice(0, HEAD_DIM) if hd % 2 == 0 else slice(HEAD_DIM, LANES)
        one_rows = slice(HEAD_DIM, LANES) if hd % 2 == 0 else slice(0, HEAD_DIM)
        qt_ref[0, hd, q_rows, :] = qt[hd * HEAD_DIM:(hd + 1) * HEAD_DIM, :]
        qt_ref[0, hd, one_rows, :] = q_ones
        for c in range(tm // KEY_CHUNK):
            vt_ref[0, hd, c, 0:HEAD_DIM, :] = vt[hd * HEAD_DIM:(hd + 1) * HEAD_DIM,
                                                 c * KEY_CHUNK:(c + 1) * KEY_CHUNK].astype(BF16)
            vt_ref[0, hd, c, HEAD_DIM:V_ROWS, :] = _ones_row_tail(KEY_CHUNK)


def _fox_proj(x, sc, sh, w_in, b_f):
    B, S, _ = x.shape
    tm = ROW_TILE
    scale = LOG2E * HEAD_DIM ** -0.5
    wq = (w_in[:, :D_MODEL] * scale).reshape(D_MODEL, N_HEADS, HEAD_DIM)
    wk = w_in[:, D_MODEL:2 * D_MODEL].astype(BF16)
    wv = w_in[:, 2 * D_MODEL:3 * D_MODEL]
    wf = w_in[:, 3 * D_MODEL:]
    wqt_p = wq.reshape(D_MODEL, N_HEADS * HEAD_DIM).T.astype(BF16)
    wvt = wv.T.astype(BF16)
    wf_rep = jnp.pad(jnp.tile(wf, (1, 3)), ((0, 0), (0, LANES - 3 * N_HEADS)))
    wf_hi = wf_rep.astype(BF16)
    wf_lo = (wf_rep - wf_hi.astype(F32)).astype(BF16)
    bf_rep = jnp.pad(jnp.tile(b_f, 3), (0, LANES - 3 * N_HEADS)).reshape(1, LANES)
    place = np.zeros((LANES, N_HEADS * HEAD_DIM), np.float32)
    for r in range(3):
        for hd in range(N_HEADS):
            place[r * N_HEADS + hd, (hd ^ 1) * HEAD_DIM + r] = 1.0
    place = jnp.asarray(place, BF16)

    full = lambda a: pl.BlockSpec(a.shape, lambda b, i: (0,) * a.ndim)
    return pl.pallas_call(
        _fox_proj_kernel, name="fox_proj",
        grid=(B, S // tm),
        in_specs=[
            pl.BlockSpec((1, tm, D_MODEL), lambda b, i: (b, i, 0)),
            pl.BlockSpec((1, 1, D_MODEL), lambda b, i: (b, 0, 0)),
            pl.BlockSpec((1, 1, D_MODEL), lambda b, i: (b, 0, 0)),
            full(wk), full(wqt_p), full(wvt), full(wf_hi), full(wf_lo), full(bf_rep), full(place),
        ],
        out_specs=[
            pl.BlockSpec((1, N_HEADS, tm, LANES), lambda b, i: (b, 0, i, 0)),
            pl.BlockSpec((1, N_HEADS, LANES, tm), lambda b, i: (b, 0, 0, i)),
            pl.BlockSpec((1, N_HEADS, tm // KEY_CHUNK, V_ROWS, KEY_CHUNK), lambda b, i: (b, 0, i, 0, 0)),
        ],
        out_shape=[
            jax.ShapeDtypeStruct((B, N_HEADS, S, LANES), BF16),
            jax.ShapeDtypeStruct((B, N_HEADS, LANES, S), BF16),
            jax.ShapeDtypeStruct((B, N_HEADS, S // KEY_CHUNK, V_ROWS, KEY_CHUNK), BF16),
        ],
        scratch_shapes=[pltpu.VMEM((SUBLANES, LANES), F32)],
        compiler_params=_params("arbitrary", "arbitrary"),
    )(x, sc, sh, wk, wqt_p, wvt, wf_hi, wf_lo, bf_rep, place)


def _ones_row_tail(width):
    row = lax.broadcasted_iota(jnp.int32, (V_ROWS - HEAD_DIM, width), 0)
    return (row == 0).astype(F32).astype(BF16)


def _flash_sweep(first, n_far_pairs, scores, values, tail, s_buf, p_buf, acc_ref, far_pairs_even=False):
    width = acc_ref.shape[1]
    n_blocks = width // QUERY_SUB
    assert len(tail) % 2 == 0 and width % QUERY_SUB == 0
    assert tail[0][0] is None and tail[1][0] is None
    cat = lambda parts: jnp.concatenate(parts, axis=1)

    def fetch(c, par, bias, may_be_missing):
        block_scores = scores(c)
        missing = jnp.where(c < 0, NEG, 0.0) if may_be_missing else None
        col_max = []
        for blk in range(n_blocks):
            cols = slice(blk * QUERY_SUB, (blk + 1) * QUERY_SUB)
            part = block_scores(cols)
            extra = None if bias is None else bias(blk)
            if extra is not None:
                part = part + extra
            if missing is not None:
                part = part + missing
            s_buf[par, :, cols] = part
            col_max.append(jnp.max(part, axis=0, keepdims=True))
        return cat(col_max)

    def step(j, par, stats, nxt=None):
        m, corr1, corr2, cmax = stats
        acc_ref[...] = acc_ref[...] * corr2 + _dot(values(j - 2), p_buf[par])
        m_new = jnp.maximum(m, cmax[par])
        for blk in range(n_blocks):
            cols = slice(blk * QUERY_SUB, (blk + 1) * QUERY_SUB)
            p_buf[par, :, cols] = jnp.exp2((s_buf[par, :, cols] - m_new[:, cols]).astype(BF16))
        if nxt is not None:
            cmax = tuple(fetch(j + 2, par, *nxt) if q == par else cmax[q] for q in range(2))
        return m_new, jnp.exp2(m - m_new), corr1, cmax

    far = (None, False)

    def pairs(j, n, stats):
        for k in range(n):
            stats = step(j + 2 * k + 1, 1, step(j + 2 * k, 0, stats, far), far)
        return stats

    p_buf[...] = jnp.zeros_like(p_buf)
    acc_ref[...] = jnp.zeros_like(acc_ref)
    ones = jnp.ones((1, width), F32)
    cmax = (fetch(first, 0, None, True), fetch(first + 1, 1, None, True))
    stats = (jnp.full((1, width), NEG, F32), ones, ones, cmax)
    start = first
    for n in (1, 2):
        if n == 1 and far_pairs_even:
            continue
        group = (n_far_pairs // n) % 2
        stats = lax.cond(group == 1, lambda st, j=start, n=n: pairs(j, n, st), lambda st: st, stats)
        start = start + 2 * n * group
    stats = lax.fori_loop(0, n_far_pairs // PAIRS_PER_TRIP,
                          lambda jj, st: pairs(start + 2 * PAIRS_PER_TRIP * jj, PAIRS_PER_TRIP, st), stats)
    last = first + 2 * n_far_pairs + len(tail)
    for r in range(len(tail)):
        stats = step(last - len(tail) + r, r % 2, stats, tail[r + 2] if r + 2 < len(tail) else None)
    _, corr1, corr2, _ = stats
    acc = acc_ref[...] * corr2 + _dot(values(last - 2), p_buf[0])
    acc = acc * corr1 + _dot(values(last - 1), p_buf[1])
    return acc[:HEAD_DIM] / acc[HEAD_DIM:HEAD_DIM + 1]


def _fox_attn_kernel(qt_ref, kp_ref, vt_ref, o_ref, s_buf, p_buf, acc_ref):
    i = pl.program_id(2)
    t = qt_ref.shape[3]
    per_tile = t // KEY_CHUNK

    def scores(c):
        rows = pl.ds(pl.multiple_of(jnp.maximum(c, 0) * KEY_CHUNK, KEY_CHUNK), KEY_CHUNK)
        k = kp_ref[0, 0, rows, :]
        return lambda cols: _dot(k, qt_ref[0, 0, :, cols])

    def values(c):
        return vt_ref[0, 0, jnp.maximum(c, 0)]

    key = lax.broadcasted_iota(jnp.int32, (KEY_CHUNK, QUERY_SUB), 0)
    qry = lax.broadcasted_iota(jnp.int32, (KEY_CHUNK, QUERY_SUB), 1)
    causal = lambda r: (lambda c: jnp.where(key + r * KEY_CHUNK <= qry + c * QUERY_SUB, 0.0, NEG))
    assert per_tile % 4 == 0
    tail = [(None, True)] * per_tile + [(causal(r), False) for r in range(per_tile)]
    out = _flash_sweep(per_tile * jnp.minimum(i - 1, 0), (per_tile // 2) * jnp.maximum(i - 1, 0),
                       scores, values, tail, s_buf, p_buf, acc_ref, far_pairs_even=True)
    o_ref[0] = out.astype(o_ref.dtype)


def _fox_attention(qt, kp, vt):
    B, H, S, _ = kp.shape
    t = FOX_T
    return pl.pallas_call(
        _fox_attn_kernel, name="fox_attn",
        grid=(B, H, S // t),
        in_specs=[
            pl.BlockSpec((1, 1, LANES, t), lambda b, h, i: (b, h, 0, i)),
            pl.BlockSpec((1, 1, S, LANES), lambda b, h, i: (b, h, 0, 0)),
            pl.BlockSpec((1, 1, S // KEY_CHUNK, V_ROWS, KEY_CHUNK), lambda b, h, i: (b, h, 0, 0, 0)),
        ],
        out_specs=pl.BlockSpec((1, HEAD_DIM, t), lambda b, h, i: (b, h, i)),
        out_shape=jax.ShapeDtypeStruct((B, H * HEAD_DIM, S), BF16),
        scratch_shapes=[pltpu.VMEM((2, KEY_CHUNK, t), F32), pltpu.VMEM((2, KEY_CHUNK, t), BF16),
                        pltpu.VMEM((V_ROWS, t), F32)],
        compiler_params=_params("arbitrary", "arbitrary", "arbitrary"),
    )(qt, kp, vt)


def _block_tail_kernel(at_ref, wo_ref, x_ref, ga_ref, g1_ref, b1_ref, sc_ref, sh_ref, gf_ref,
                       wa_ref, wb_ref, w2_ref, g2_ref, b2_ref, o_ref):
    y = lax.dot_general(at_ref[0], wo_ref[...], (((0,), (0,)), ((), ())), preferred_element_type=F32)
    x1 = _layer_norm(DN_ALPHA * x_ref[0] + (1.0 + ga_ref[0]) * y, g1_ref[...], b1_ref[...])
    hb = (x1 * (1.0 + sc_ref[0]) + sh_ref[0]).astype(BF16)
    a = _dot(hb, wa_ref[...])
    b = _dot(hb, wb_ref[...])
    u = (a / (1.0 + jnp.exp(-a)) * b).astype(BF16)
    z = DN_ALPHA * x1 + (1.0 + gf_ref[0]) * _dot(u, w2_ref[...])
    o_ref[0] = _layer_norm(z, g2_ref[...], b2_ref[...])


def _block_tail(attn_t, w_out, x, gate_a, ln1, sc, sh, gate_f, w_in, w_out_ffn, ln2):
    B, S, _ = x.shape
    tm = ROW_TILE
    w_in = w_in.astype(BF16)
    row = lambda r: r.reshape(1, -1)
    mod = pl.BlockSpec((1, 1, D_MODEL), lambda b, i: (b, 0, 0))
    vec = pl.BlockSpec((1, D_MODEL), lambda b, i: (0, 0))
    resident = lambda shape, index: pl.BlockSpec(shape, index, pipeline_mode=pl.Buffered(1))
    return pl.pallas_call(
        _block_tail_kernel, name="outproj_ffn_ln",
        grid=(B, S // tm),
        in_specs=[
            pl.BlockSpec((1, D_MODEL, tm), lambda b, i: (b, 0, i)),
            resident((D_MODEL, D_MODEL), lambda b, i: (0, 0)),
            pl.BlockSpec((1, tm, D_MODEL), lambda b, i: (b, i, 0)),
            mod, vec, vec,
            mod, mod, mod,
            resident((D_MODEL, FF_HIDDEN), lambda b, i: (0, 0)),
            resident((D_MODEL, FF_HIDDEN), lambda b, i: (0, 1)),
            resident((FF_HIDDEN, D_MODEL), lambda b, i: (0, 0)),
            vec, vec,
        ],
        out_specs=pl.BlockSpec((1, tm, D_MODEL), lambda b, i: (b, i, 0)),
        out_shape=jax.ShapeDtypeStruct((B, S, D_MODEL), F32),
        compiler_params=_params("arbitrary", "arbitrary"),
    )(attn_t, w_out.astype(BF16), x, gate_a, row(ln1[0]), row(ln1[1]), sc, sh, gate_f,
      w_in, w_in, w_out_ffn.astype(BF16), row(ln2[0]), row(ln2[1]))


def _nsa_proj_kernel(x_ref, sc_ref, sh_ref, wqt_ref, wnat_ref, wvt_ref, wgt_ref, bg_ref,
                     qt_ref, kc_ref, vc_ref, ks_ref, kw_ref, vst_ref, vwt_ref, gt_ref):
    tm = x_ref.shape[1]
    t = NSA_T
    hb = (x_ref[0] * (1.0 + sc_ref[0]) + sh_ref[0]).astype(BF16)

    qt = _dot_nt(wqt_ref[...], hb).astype(BF16)
    for g in range(N_GROUPS):
        for hh in range(HEADS_PER_GROUP):
            r0 = (g * HEADS_PER_GROUP + hh) * HEAD_DIM
            for c in range(tm // t):
                qt_ref[0, g, c, :, hh * t:(hh + 1) * t] = qt[r0:r0 + HEAD_DIM, c * t:(c + 1) * t]

    nat = _dot(hb, wnat_ref[...])
    lane = lax.broadcasted_iota(jnp.int32, (tm, LANES), 1)
    ones = ((lane == HEAD_DIM) | (lane == HEAD_DIM + 1)).astype(F32)
    off_ks = 2 * KV_WIDTH
    off_kw = off_ks + N_GROUPS * LANES
    for g in range(N_GROUPS):
        kc_ref[0, g] = nat[:, g * HEAD_DIM:(g + 1) * HEAD_DIM].astype(BF16)
        vc_ref[0, g] = nat[:, KV_WIDTH + g * HEAD_DIM:KV_WIDTH + (g + 1) * HEAD_DIM].astype(BF16)
        ks_ref[0, g] = (nat[:, off_ks + g * LANES:off_ks + (g + 1) * LANES] + ones).astype(BF16)
        kw_ref[0, g] = nat[:, off_kw + g * LANES:off_kw + (g + 1) * LANES].astype(BF16)

    vt = _dot_nt(wvt_ref[...], hb).astype(BF16)
    for g in range(N_GROUPS):
        for c in range(tm // KEY_CHUNK):
            vst_ref[0, g, c, 0:HEAD_DIM, :] = vt[g * HEAD_DIM:(g + 1) * HEAD_DIM, c * KEY_CHUNK:(c + 1) * KEY_CHUNK]
            vst_ref[0, g, c, HEAD_DIM:V_ROWS, :] = _ones_row_tail(KEY_CHUNK)
        for c in range(tm // t):
            vwt_ref[0, g, c, 0:HEAD_DIM, :] = vt[KV_WIDTH + g * HEAD_DIM:KV_WIDTH + (g + 1) * HEAD_DIM,
                                                 c * t:(c + 1) * t]
            vwt_ref[0, g, c, HEAD_DIM:V_ROWS, :] = _ones_row_tail(t)

    gl = _dot_nt(wgt_ref[...], hb) + bg_ref[...]
    gates = 1.0 / (1.0 + jnp.exp(-gl))
    assert tm == t
    gt_ref[...] = jnp.zeros_like(gt_ref)
    for br in range(3):
        for g in range(N_GROUPS):
            for hh in range(HEADS_PER_GROUP):
                row = br * N_HEADS + g * HEADS_PER_GROUP + hh
                gt_ref[0, g, 0, br:br + 1, hh * t:(hh + 1) * t] = gates[row:row + 1, :]


def _nsa_proj(x, sc, sh, w_in, b_gate):
    B, S, _ = x.shape
    tm = ROW_TILE
    t = NSA_T
    scale = LOG2E * HEAD_DIM ** -0.5
    cuts = [D_MODEL + n * KV_WIDTH for n in range(7)]
    wq, wkc, wvc, wks, wvs, wkw, wvw, wg = jnp.split(w_in, cuts, axis=1)
    wqt = (wq * scale).T.astype(BF16)
    padk = lambda w: jnp.pad(w.reshape(D_MODEL, N_GROUPS, HEAD_DIM),
                             ((0, 0), (0, 0), (0, LANES - HEAD_DIM))).reshape(D_MODEL, N_GROUPS * LANES)
    wnat = jnp.concatenate([wkc, wvc, padk(wks), padk(wkw)], axis=1).astype(BF16)
    wvt = jnp.concatenate([wvs, wvw], axis=1).T.astype(BF16)
    n_gate = 3 * N_HEADS
    wgt = wg.T.astype(BF16)
    bg = jnp.broadcast_to(b_gate.reshape(n_gate, 1), (n_gate, tm))

    full = lambda a: pl.BlockSpec(a.shape, lambda b, i: (0,) * a.ndim)
    nat_spec = pl.BlockSpec((1, N_GROUPS, tm, HEAD_DIM), lambda b, i: (b, 0, i, 0))
    pad_spec = pl.BlockSpec((1, N_GROUPS, tm, LANES), lambda b, i: (b, 0, i, 0))
    vt_spec = lambda n, rows: pl.BlockSpec((1, N_GROUPS, tm // n, rows, n), lambda b, i: (b, 0, i, 0, 0))
    return pl.pallas_call(
        _nsa_proj_kernel, name="nsa_proj",
        grid=(B, S // tm),
        in_specs=[
            pl.BlockSpec((1, tm, D_MODEL), lambda b, i: (b, i, 0)),
            pl.BlockSpec((1, 1, D_MODEL), lambda b, i: (b, 0, 0)),
            pl.BlockSpec((1, 1, D_MODEL), lambda b, i: (b, 0, 0)),
            full(wqt), full(wnat), full(wvt), full(wgt), full(bg),
        ],
        out_specs=[
            pl.BlockSpec((1, N_GROUPS, tm // t, HEAD_DIM, HEADS_PER_GROUP * t), lambda b, i: (b, 0, i, 0, 0)),
            nat_spec, nat_spec, pad_spec, pad_spec, vt_spec(KEY_CHUNK, V_ROWS), vt_spec(t, V_ROWS),
            pl.BlockSpec((1, N_GROUPS, 1, SUBLANES, HEADS_PER_GROUP * t), lambda b, i: (b, 0, i, 0, 0)),
        ],
        out_shape=[
            jax.ShapeDtypeStruct((B, N_GROUPS, S // t, HEAD_DIM, HEADS_PER_GROUP * t), BF16),
            jax.ShapeDtypeStruct((B, N_GROUPS, S, HEAD_DIM), BF16),
            jax.ShapeDtypeStruct((B, N_GROUPS, S, HEAD_DIM), BF16),
            jax.ShapeDtypeStruct((B, N_GROUPS, S, LANES), BF16),
            jax.ShapeDtypeStruct((B, N_GROUPS, S, LANES), BF16),
            jax.ShapeDtypeStruct((B, N_GROUPS, S // KEY_CHUNK, V_ROWS, KEY_CHUNK), BF16),
            jax.ShapeDtypeStruct((B, N_GROUPS, S // t, V_ROWS, t), BF16),
            jax.ShapeDtypeStruct((B, N_GROUPS, S // t, SUBLANES, HEADS_PER_GROUP * t), F32),
        ],
        compiler_params=_params("arbitrary", "arbitrary"),
    )(x, sc, sh, wqt, wnat, wvt, wgt, bg)


def _compress_kernel(tk_ref, tv_ref, pos_ref, w1_ref, w2k_ref, w2vt_ref, kc_ref, vct_ref):
    n = tk_ref.shape[2]
    half = CMP_STRIDE * HEAD_DIM

    def hidden(t_ref, idx):
        t16 = t_ref[0, 0]
        xa = (t16 + pos_ref[idx, 0:1, :]).astype(BF16)
        xb = (t16 + pos_ref[idx, 1:2, :]).astype(BF16)
        first = _dot(xa, w1_ref[idx, :half, :])
        second = _dot(xb, w1_ref[idx, half:, :])
        pre = first + pltpu.roll(second, n - 1, 0)
        return (pre / (1.0 + jnp.exp(-pre))).astype(BF16)

    kc = _dot(hidden(tk_ref, 0), w2k_ref[...])
    lane = lax.broadcasted_iota(jnp.int32, kc.shape, 1)
    ones = ((lane == HEAD_DIM) | (lane == HEAD_DIM + 1)).astype(F32)
    kc_ref[0, 0] = (kc + ones).astype(BF16)
    vct = _dot_nt(w2vt_ref[...], hidden(tv_ref, 1)).astype(BF16)
    for r in range(n // CMP_ROWS):
        vct_ref[0, 0, r] = vct[:, r * CMP_ROWS:(r + 1) * CMP_ROWS]


def _compress(kc, vc, cmp_pos, cmp_w1, cmp_w2):
    B, G, S, _ = kc.shape
    n = S // CMP_STRIDE
    width = CMP_STRIDE * HEAD_DIM
    tk = kc.reshape(B, G, n, width)
    tv = vc.reshape(B, G, n, width)
    pos = cmp_pos.reshape(2, 2, width)
    w1 = cmp_w1.astype(BF16)
    w2k = jnp.pad(cmp_w2[0], ((0, 0), (0, LANES - HEAD_DIM))).astype(BF16)
    w2vt = cmp_w2[1].T.astype(BF16)
    full = lambda a: pl.BlockSpec(a.shape, lambda b, g: (0,) * a.ndim)
    t_spec = pl.BlockSpec((1, 1, n, width), lambda b, g: (b, g, 0, 0))
    return pl.pallas_call(
        _compress_kernel, name="nsa_compress",
        grid=(B, G),
        in_specs=[t_spec, t_spec, full(pos), full(w1), full(w2k), full(w2vt)],
        out_specs=[
            pl.BlockSpec((1, 1, n, LANES), lambda b, g: (b, g, 0, 0)),
            pl.BlockSpec((1, 1, n // CMP_ROWS, HEAD_DIM, CMP_ROWS), lambda b, g: (b, g, 0, 0, 0)),
        ],
        out_shape=[
            jax.ShapeDtypeStruct((B, G, n, LANES), BF16),
            jax.ShapeDtypeStruct((B, G, n // CMP_ROWS, HEAD_DIM, CMP_ROWS), BF16),
        ],
        compiler_params=_params("arbitrary", "arbitrary"),
    )(tk, tv, pos, w1, w2k, w2vt)


def _cmp_band_rows(t):
    return t // CMP_STRIDE + NEAR_CMP_BLOCKS


def _cmp_band_start(i, t):
    return jnp.maximum(i * t // CMP_STRIDE - NEAR_CMP_BLOCKS, 0)


def _bias_tiles_kernel(rb_ref, sel_ref, win_ref, cmp_ref):
    hd = pl.program_id(0)
    far = rb_ref[FAR_BUCKET, hd]

    def dist(n, back):
        key = lax.broadcasted_iota(jnp.int32, (n, n), 0)
        qry = lax.broadcasted_iota(jnp.int32, (n, n), 1)
        return qry - key + back * n

    kc = sel_ref.shape[2]
    d_prev, d_diag = dist(kc, 1), dist(kc, 0)
    sel_ref[0, 0] = LOG2E * (_bucket_bias(d_prev, rb_ref, hd) - far)
    sel_ref[1, 0] = jnp.where(d_diag >= 0, LOG2E * (_bucket_bias(d_diag, rb_ref, hd) - far), NEG)
    t = win_ref.shape[2]
    n_back = win_ref.shape[0] - 1
    for r in range(n_back + 1):
        d = dist(t, n_back - r)
        win_ref[r, 0] = jnp.where((d >= 0) & (d < WINDOW), LOG2E * _bucket_bias(d, rb_ref, hd), NEG)
    band = cmp_ref.shape[2]
    blk = lax.broadcasted_iota(jnp.int32, (band, t), 0)
    qry = lax.broadcasted_iota(jnp.int32, (band, t), 1)
    for later in range(2):
        d = (later * t + qry) - ((_cmp_band_start(later, t) + blk) * CMP_STRIDE + CMP_LEN - 1)
        cmp_ref[later, 0] = LOG2E * (_bucket_bias(d, rb_ref, hd) - far)


def _bias_tiles(rel_bias):
    t = NSA_T
    n_win = WINDOW // t + 1
    band = _cmp_band_rows(t)
    assert WINDOW % t == 0 and (FAR_DIST + CMP_LEN - 1) <= NEAR_CMP_BLOCKS * CMP_STRIDE
    spec = lambda n, rows, cols: pl.BlockSpec((n, 1, rows, cols), lambda h: (0, h, 0, 0))
    return pl.pallas_call(
        _bias_tiles_kernel, name="rel_bias_tiles",
        grid=(N_HEADS,),
        in_specs=[pl.BlockSpec(memory_space=pltpu.SMEM)],
        out_specs=[spec(2, KEY_CHUNK, KEY_CHUNK), spec(n_win, t, t), spec(2, band, t)],
        out_shape=[
            jax.ShapeDtypeStruct((2, N_HEADS, KEY_CHUNK, KEY_CHUNK), F32),
            jax.ShapeDtypeStruct((n_win, N_HEADS, t, t), F32),
            jax.ShapeDtypeStruct((2, N_HEADS, band, t), F32),
        ],
        compiler_params=_params("arbitrary"),
    )(rel_bias)


def _far_bias_rows(rb_ref, g, width):
    t = width // HEADS_PER_GROUP
    lane = lax.broadcasted_iota(jnp.int32, (HEAD_DIM, width), 1)
    row = lax.broadcasted_iota(jnp.int32, (HEAD_DIM, width), 0)
    far = jnp.zeros((HEAD_DIM, width), F32)
    for hh in range(HEADS_PER_GROUP):
        far = jnp.where(lane >= hh * t, LOG2E * rb_ref[FAR_BUCKET, g * HEADS_PER_GROUP + hh], far)
    hi = far.astype(BF16).astype(F32)
    return jnp.where(row == 0, hi, jnp.where(row == 1, far - hi, 0.0)).astype(BF16)


def _nsa_cmp_kernel(rb_ref, qt_ref, kc_ref, vct_ref, at_ref, band_ref, oc_ref, sb_ref,
                    qp_ref, s_ref, imp_ref):
    g = pl.program_id(1)
    i = pl.program_id(2)
    width = qt_ref.shape[4]
    t = width // HEADS_PER_GROUP
    n = kc_ref.shape[2]
    nb = sb_ref.shape[3]
    t0 = i * t
    assert t & (t - 1) == 0 and n % CMP_ROWS == 0

    n_vis = (t0 + t - CMP_LEN) // CMP_STRIDE // CMP_ROWS + 1
    n_full = jnp.maximum((t0 - (CMP_LEN - 1)) // CMP_STRIDE + 1, 0) // CMP_ROWS
    rows_of = lambda r: pl.ds(pl.multiple_of(r * CMP_ROWS, CMP_ROWS), CMP_ROWS)

    def valid(r):
        blk = r * CMP_ROWS + lax.broadcasted_iota(jnp.int32, (CMP_ROWS, width), 0)
        qry = t0 + (lax.broadcasted_iota(jnp.int32, (CMP_ROWS, width), 1) & (t - 1))
        return qry >= blk * CMP_STRIDE + CMP_LEN - 1

    qp_ref[0:HEAD_DIM, :] = qt_ref[0, 0, 0]
    qp_ref[HEAD_DIM:LANES, :] = _far_bias_rows(rb_ref, g, width)

    n_early = _cmp_band_start(i, t) // CMP_ROWS

    def score_early(r, m):
        s = _dot(kc_ref[0, 0, rows_of(r), :], qp_ref[...])
        s_ref[rows_of(r), :] = s
        return jnp.maximum(m, jnp.max(s, axis=0, keepdims=True))

    def score_late(r, carry):
        s_ref[rows_of(r), :] = _dot(kc_ref[0, 0, rows_of(r), :], qp_ref[...])
        return carry

    m = lax.fori_loop(0, n_early, score_early, jnp.full((1, width), NEG, F32))
    lax.fori_loop(n_early, n_vis, score_late, 0)

    band = band_ref.shape[2]
    assert band == _cmp_band_rows(t) and band <= n
    r0 = pl.multiple_of(_cmp_band_start(i, t), SUBLANES)
    for hh in range(HEADS_PER_GROUP):
        s_ref[pl.ds(r0, band), hh * t:(hh + 1) * t] += band_ref[jnp.minimum(i, 1), hh]

    def col_max(masked):
        def body(r, m):
            s = s_ref[rows_of(r), :]
            if masked:
                s = jnp.where(valid(r), s, NEG)
            return jnp.maximum(m, jnp.max(s, axis=0, keepdims=True))
        return body

    m = lax.fori_loop(n_early, n_full, col_max(False), m)
    m = lax.fori_loop(jnp.maximum(n_full, n_early), n_vis, col_max(True), m)

    def exp_sum(masked):
        def body(r, l):
            p = jnp.exp2(s_ref[rows_of(r), :] - m)
            if masked:
                p = jnp.where(valid(r), p, 0.0)
            s_ref[rows_of(r), :] = p
            return l + jnp.sum(p, axis=0, keepdims=True)
        return body

    l = lax.fori_loop(0, n_full, exp_sum(False), jnp.zeros((1, width), F32))
    l = lax.fori_loop(n_full, n_vis, exp_sum(True), l)
    inv = 1.0 / jnp.maximum(l, TINY)

    oc_ref[0, 0, 0] = jnp.zeros((HEAD_DIM, width), F32)
    imp_ref[...] = jnp.zeros_like(imp_ref)

    def finish(r, carry):
        p = s_ref[rows_of(r), :] * inv
        oc_ref[0, 0, 0] += _dot(vct_ref[0, 0, r], p.astype(BF16))
        imp = p[:, 0:t]
        for hh in range(1, HEADS_PER_GROUP):
            imp = imp + p[:, hh * t:(hh + 1) * t]
        hi, mid, lo = _split3(imp)
        a = at_ref[...]
        band = pl.ds(pl.multiple_of(r * (CMP_ROWS * CMP_STRIDE // SEL_BLOCK), SUBLANES), a.shape[0])
        imp_ref[band, :] += _dot(a, hi) + _dot(a, mid) + _dot(a, lo)
        return carry

    lax.fori_loop(0, n_vis, finish, 0)

    def select_among(rows):
        def run():
            sblk = lax.broadcasted_iota(jnp.int32, (rows, t), 0)
            cur = (t0 + lax.broadcasted_iota(jnp.int32, (rows, t), 1)) // SEL_BLOCK
            forced = (sblk == 0) | (sblk == cur) | (sblk == cur - 1)
            is_cand = (sblk >= 1) & (sblk <= cur - 2)
            cand = jnp.where(is_cand, imp_ref[0:rows, :], -1.0)
            sblk_f = sblk.astype(F32)
            for _ in range(SEL_TOPK - N_FORCED):
                best = jnp.max(cand, axis=0, keepdims=True)
                first = jnp.min(jnp.where(cand == best, sblk_f, float(nb)), axis=0, keepdims=True)
                cand = jnp.where(sblk_f == first, -1.0, cand)
            chosen = forced | (is_cand & (cand < 0.0))
            sb_ref[0, 0, 0, 0:rows, :] = jnp.where(chosen, 0.0, NEG).astype(BF16)
            if rows < nb:
                sb_ref[0, 0, 0, rows:nb, :] = jnp.full((nb - rows, t), NEG, BF16)
        return run

    last_block = (t0 + t - 1) // SEL_BLOCK
    lax.switch(jnp.minimum(last_block // TOPK_ROWS, nb // TOPK_ROWS - 1),
               [select_among(rows) for rows in range(TOPK_ROWS, nb + 1, TOPK_ROWS)])


def _nsa_cmp(rel_bias, qt, kcmp, vcmp_t, cmp_band, n_sel):
    B, G, nq, _, width = qt.shape
    t = width // HEADS_PER_GROUP
    n = kcmp.shape[2]
    n_cmp = n - 1
    nb = -(-n_sel // LANES) * LANES
    R = SEL_BLOCK // CMP_STRIDE
    assert n_cmp == n - 1 and nb >= n_sel
    band_rows = -(-(CMP_ROWS // R + 1) // 8) * 8
    at = np.zeros((band_rows, CMP_ROWS), np.float32)
    for j in range(CMP_ROWS // R + 1):
        at[j, max(R * j - 1, 0):min(R * j + R, CMP_ROWS)] = 1.0
    at = jnp.asarray(at, BF16)
    return pl.pallas_call(
        _nsa_cmp_kernel, name="nsa_cmp_topk",
        grid=(B, G, nq),
        in_specs=[
            pl.BlockSpec(memory_space=pltpu.SMEM),
            pl.BlockSpec((1, 1, 1, HEAD_DIM, width), lambda b, g, i: (b, g, i, 0, 0)),
            pl.BlockSpec((1, 1, n, LANES), lambda b, g, i: (b, g, 0, 0)),
            pl.BlockSpec((1, 1, n // CMP_ROWS, HEAD_DIM, CMP_ROWS), lambda b, g, i: (b, g, 0, 0, 0)),
            pl.BlockSpec((band_rows, CMP_ROWS), lambda b, g, i: (0, 0)),
            pl.BlockSpec((2, HEADS_PER_GROUP) + cmp_band.shape[2:], lambda b, g, i: (0, g, 0, 0)),
        ],
        out_specs=[
            pl.BlockSpec((1, 1, 1, HEAD_DIM, width), lambda b, g, i: (b, g, i, 0, 0)),
            pl.BlockSpec((1, 1, 1, nb, t), lambda b, g, i: (b, g, i, 0, 0)),
        ],
        out_shape=[
            jax.ShapeDtypeStruct((B, G, nq, HEAD_DIM, width), F32),
            jax.ShapeDtypeStruct((B, G, nq, nb, t), BF16),
        ],
        scratch_shapes=[pltpu.VMEM((LANES, width), BF16), pltpu.VMEM((n, width), F32),
                        pltpu.VMEM((nb + band_rows, t), F32)],
        compiler_params=_params("arbitrary", "arbitrary", "arbitrary"),
    )(rel_bias, qt, kcmp, vcmp_t, at, cmp_band)


def _nsa_win_kernel(qt_ref, *refs):
    n_win = (len(refs) - 5) // 2
    k_refs, v_refs = refs[:n_win], refs[n_win:2 * n_win]
    wb_ref, ow_ref, qp_ref, s_ref, p_ref = refs[2 * n_win:]
    i = pl.program_id(2)
    width = qt_ref.shape[4]
    t = width // HEADS_PER_GROUP
    qt = qt_ref[0, 0, 0]
    qp_ref[0:HEAD_DIM, :] = qt
    qp_ref[HEAD_DIM:LANES, :] = jnp.zeros_like(qt)
    key_blocks = t // QUERY_SUB

    def outside(c, kb, blk):
        ahead = blk % key_blocks - (kb - (n_win - 1 - c) * key_blocks)
        return ahead < 0 or (ahead - 1) * QUERY_SUB + 1 >= WINDOW

    n_blocks = width // QUERY_SUB
    col_max = [None] * n_blocks
    for c in range(n_win):
        missing = jnp.where(i - (n_win - 1 - c) < 0, NEG, 0.0)
        for kb in range(key_blocks):
            rows = slice(kb * QUERY_SUB, (kb + 1) * QUERY_SUB)
            k = k_refs[c][0, 0, rows, :]
            for blk in range(n_blocks):
                cols = slice(blk * QUERY_SUB, (blk + 1) * QUERY_SUB)
                if outside(c, kb, blk):
                    p_ref[c, rows, cols] = jnp.zeros((QUERY_SUB, QUERY_SUB), BF16)
                    continue
                hh, off = divmod(blk * QUERY_SUB, t)
                part = _dot(k, qp_ref[:, cols]) + wb_ref[c, hh, rows, off:off + QUERY_SUB] + missing
                s_ref[c, rows, cols] = part
                block_max = jnp.max(part, axis=0, keepdims=True)
                col_max[blk] = block_max if col_max[blk] is None else jnp.maximum(col_max[blk], block_max)
    acc = jnp.zeros((V_ROWS, width), F32)
    for c in range(n_win):
        for kb in range(key_blocks):
            rows = slice(kb * QUERY_SUB, (kb + 1) * QUERY_SUB)
            for blk in range(n_blocks):
                if not outside(c, kb, blk):
                    cols = slice(blk * QUERY_SUB, (blk + 1) * QUERY_SUB)
                    p_ref[c, rows, cols] = jnp.exp2((s_ref[c, rows, cols] - col_max[blk]).astype(BF16))
        acc = acc + _dot(v_refs[c][0, 0, 0], p_ref[c])
    ow_ref[0, 0, 0] = acc[:HEAD_DIM] / acc[HEAD_DIM:HEAD_DIM + 1]


def _nsa_window(qt, kw, vwt, win_bias):
    B, G, nq, _, width = qt.shape
    t = width // HEADS_PER_GROUP
    n_win = win_bias.shape[0]
    backs = list(range(n_win - 1, -1, -1))
    k_spec = lambda back: pl.BlockSpec((1, 1, t, LANES), lambda b, g, i: (b, g, jnp.maximum(i - back, 0), 0))
    v_spec = lambda back: pl.BlockSpec((1, 1, 1, V_ROWS, t),
                                       lambda b, g, i: (b, g, jnp.maximum(i - back, 0), 0, 0))
    return pl.pallas_call(
        _nsa_win_kernel, name="nsa_window",
        grid=(B, G, nq),
        in_specs=[pl.BlockSpec((1, 1, 1, HEAD_DIM, width), lambda b, g, i: (b, g, i, 0, 0))]
        + [k_spec(back) for back in backs] + [v_spec(back) for back in backs]
        + [pl.BlockSpec((n_win, HEADS_PER_GROUP, t, t), lambda b, g, i: (0, g, 0, 0))],
        out_specs=pl.BlockSpec((1, 1, 1, HEAD_DIM, width), lambda b, g, i: (b, g, i, 0, 0)),
        out_shape=jax.ShapeDtypeStruct((B, G, nq, HEAD_DIM, width), F32),
        scratch_shapes=[pltpu.VMEM((LANES, width), BF16),
                        pltpu.VMEM((n_win, t, width), F32), pltpu.VMEM((n_win, t, width), BF16)],
        compiler_params=_params("arbitrary", "arbitrary", "arbitrary"),
    )(qt, *([kw] * n_win), *([vwt] * n_win), win_bias)


def _nsa_sel_kernel(rb_ref, qt_ref, sb_ref, ks_ref, e_ref, vst_ref, cb_ref, oc_ref, ow_ref, gt_ref,
                    o_ref, qp_ref, s_buf, p_buf, acc_ref):
    g = pl.program_id(1)
    i = pl.program_id(2)
    width = qt_ref.shape[4]
    t = width // HEADS_PER_GROUP
    per_tile = t // KEY_CHUNK
    chunks_per_slab = LANES * SEL_BLOCK // KEY_CHUNK
    blocks_per_head = t // QUERY_SUB
    assert per_tile == 2 and t % QUERY_SUB == 0

    far_rows = _far_bias_rows(rb_ref, g, width)
    for slab in range(qp_ref.shape[0]):
        qp_ref[slab, 0:HEAD_DIM, :] = qt_ref[0, 0, 0]
        qp_ref[slab, HEAD_DIM:LANES, :] = far_rows
        sb = sb_ref[0, 0, 0, slab * LANES:(slab + 1) * LANES, :]
        qp_ref[slab, LANES:2 * LANES, :] = jnp.concatenate([sb] * HEADS_PER_GROUP, axis=1)

    def scores(c):
        c = jnp.maximum(c, 0)
        rows = pl.ds(pl.multiple_of(c * KEY_CHUNK, KEY_CHUNK), KEY_CHUNK)
        kp = jnp.concatenate([ks_ref[0, 0, rows, :], e_ref[rows, :]], axis=1)
        return lambda cols: _dot(kp, qp_ref[c // chunks_per_slab, :, cols])

    def values(c):
        return vst_ref[0, 0, jnp.maximum(c, 0)]

    def near_bias(r):
        def bias(c):
            hh, part = divmod(c, blocks_per_head)
            ahead = part * QUERY_SUB // KEY_CHUNK - (r - per_tile)
            if ahead == 0:
                return cb_ref[1, hh]
            if ahead == 1:
                return cb_ref[0, hh]
            if ahead < 0:
                return jnp.full((KEY_CHUNK, QUERY_SUB), NEG, F32)
            return None
        return bias

    tail = [(None, True), (None, True)] + [(near_bias(r), r < per_tile) for r in range(2 * per_tile)]
    o_sel = _flash_sweep(per_tile * jnp.minimum(i - 2, 0), jnp.maximum(i - 2, 0), scores, values, tail,
                         s_buf, p_buf, acc_ref)

    out = gt_ref[0, 0, 0, 0:1, :] * oc_ref[0, 0, 0] + gt_ref[0, 0, 0, 1:2, :] * o_sel \
        + gt_ref[0, 0, 0, 2:3, :] * ow_ref[0, 0, 0]
    for hh in range(HEADS_PER_GROUP):
        o_ref[0, hh * HEAD_DIM:(hh + 1) * HEAD_DIM, :] = out[:, hh * t:(hh + 1) * t].astype(o_ref.dtype)


def _nsa_select_combine(rel_bias, qt, sel_bias, ks, vst, sel_corr, oc, ow, gates):
    B, G, nq, _, width = qt.shape
    t = width // HEADS_PER_GROUP
    S = nq * t
    nb = sel_bias.shape[3]
    blocks = (np.arange(S) // SEL_BLOCK) % LANES
    onehot = jnp.asarray(blocks[:, None] == np.arange(LANES)[None, :], BF16)
    tile = pl.BlockSpec((1, 1, 1, HEAD_DIM, width), lambda b, g, i: (b, g, i, 0, 0))
    return pl.pallas_call(
        _nsa_sel_kernel, name="nsa_select",
        grid=(B, G, nq),
        in_specs=[
            pl.BlockSpec(memory_space=pltpu.SMEM),
            tile,
            pl.BlockSpec((1, 1, 1, nb, t), lambda b, g, i: (b, g, i, 0, 0)),
            pl.BlockSpec((1, 1, S, LANES), lambda b, g, i: (b, g, 0, 0)),
            pl.BlockSpec((S, LANES), lambda b, g, i: (0, 0)),
            pl.BlockSpec((1, 1, S // KEY_CHUNK, V_ROWS, KEY_CHUNK), lambda b, g, i: (b, g, 0, 0, 0)),
            pl.BlockSpec((2, HEADS_PER_GROUP, KEY_CHUNK, KEY_CHUNK), lambda b, g, i: (0, g, 0, 0)),
            tile, tile,
            pl.BlockSpec((1, 1, 1, SUBLANES, width), lambda b, g, i: (b, g, i, 0, 0)),
        ],
        out_specs=pl.BlockSpec((1, HEADS_PER_GROUP * HEAD_DIM, t), lambda b, g, i: (b, g, i)),
        out_shape=jax.ShapeDtypeStruct((B, D_MODEL, S), BF16),
        scratch_shapes=[pltpu.VMEM((nb // LANES, 2 * LANES, width), BF16),
                        pltpu.VMEM((2, KEY_CHUNK, width), F32), pltpu.VMEM((2, KEY_CHUNK, width), BF16),
                        pltpu.VMEM((V_ROWS, width), F32)],
        compiler_params=_params("arbitrary", "arbitrary", "arbitrary"),
    )(rel_bias, qt, sel_bias, ks, onehot, vst, sel_corr, oc, ow, gates)


def _nsa_attention_t(x, sc, sh, w_in, b_gate, cmp_pos, cmp_w1, cmp_w2, rel_bias):
    B, S, _ = x.shape
    t = NSA_T
    n_sel = S // SEL_BLOCK
    assert S % ROW_TILE == 0 and n_sel >= SEL_TOPK and S // CMP_STRIDE >= t // CMP_STRIDE + NEAR_CMP_BLOCKS
    qt, kc, vc, ks, kw, vst, vwt, gt = _nsa_proj(x, sc, sh, w_in, b_gate)
    kcmp, vcmp_t = _compress(kc, vc, cmp_pos, cmp_w1, cmp_w2)
    sel_corr, win_bias, cmp_band = _bias_tiles(rel_bias)
    oc, sel_bias = _nsa_cmp(rel_bias, qt, kcmp, vcmp_t, cmp_band, n_sel)
    ow = _nsa_window(qt, kw, vwt, win_bias)
    return _nsa_select_combine(rel_bias, qt, sel_bias, ks, vst, sel_corr, oc, ow, gt)


def kernel(x, c, ada_w, ada_b, ln_g, ln_b, fox_w_in, fox_b_f, fox_w_out, nsa_w_in, nsa_b_gate,
           nsa_cmp_pos, nsa_cmp_w1, nsa_cmp_w2, nsa_w_out, rel_bias, ffn_w_in, ffn_w_out):
    B, S, _ = x.shape
    assert S % FOX_T == 0 and S % ROW_TILE == 0 and ROW_TILE == NSA_T
    mod = _modulation(c, ada_w, ada_b)
    for layer in range(DEPTH):
        sh_a, sc_a, g_a, sh_f, sc_f, g_f = [m.reshape(B, 1, D_MODEL) for m in jnp.split(mod[layer], 6, axis=-1)]
        j = layer // 2
        if layer % 2 == 0:
            kp, qt, vt = _fox_proj(x, sc_a, sh_a, fox_w_in[j], fox_b_f[j])
            attn_t = _fox_attention(qt, kp, vt)
            w_out = fox_w_out[j]
        else:
            attn_t = _nsa_attention_t(x, sc_a, sh_a, nsa_w_in[j], nsa_b_gate[j], nsa_cmp_pos[j],
                                      nsa_cmp_w1[j], nsa_cmp_w2[j], rel_bias)
            w_out = nsa_w_out[j]
        x = _block_tail(attn_t, w_out, x, g_a, (ln_g[layer, 0], ln_b[layer, 0]), sc_f, sh_f, g_f,
                        ffn_w_in[layer], ffn_w_out[layer], (ln_g[layer, 1], ln_b[layer, 1]))
    return x
```
